```python
import math
import jax
import jax.numpy as jnp
from jax import lax
import numpy as np

D_MODEL = 1024
BATCH = 4
SEQ = 4096
DEPTH = 2
DEC_BATCH = 32
DEC_SEQ = 8
PAST_LEN = 8192
PAGE_SIZE = 128

N_A_LAYERS = (DEPTH + 1) // 2
N_C_LAYERS = DEPTH // 2
MIX_WIDTH = D_MODEL
S5_WIDTH = MIX_WIDTH // 2
S5_GROUP = 16
S5_GROUPS = S5_WIDTH // S5_GROUP
S5_STATE = 64
S5_DT_MIN = 0.001
S5_DT_MAX = 0.1
MOBA_HEADS = 8
MOBA_HEAD_DIM = (MIX_WIDTH - S5_WIDTH) // MOBA_HEADS
MOBA_BLOCK = 256
MOBA_TOPK = 3
MOBA_QCHUNK = 16
ROPE_THETA = 500000.0
ROPE_DIM = MOBA_HEAD_DIM // 4
HGRN_WIDTH = MIX_WIDTH // 2
HGRN_HEADS = 4
HGRN_HEAD_DIM = HGRN_WIDTH // HGRN_HEADS
HGRN_CHUNK = 64
SB_HEADS = 8
SB_HEAD_DIM = (MIX_WIDTH - HGRN_WIDTH) // SB_HEADS
SB_QBLOCK = 128
IN_A = S5_WIDTH + 3 * MOBA_HEADS * MOBA_HEAD_DIM
IN_C = 4 * HGRN_WIDTH + 3 * SB_HEADS * SB_HEAD_DIM
D_FF = 2816
CONV_W = 3
PLE_DIM = 256
POOL_NUM = 5
POOL_DEN = 4
EPS = 1e-6
NEG = -1e30

kernel_name = 'hybrid_s5_moba_hgrn2_stickbreak_step'


def rmsnorm(x, g):
    xf = x.astype(jnp.float32)
    y = xf * lax.rsqrt(jnp.mean(xf * xf, axis=-1, keepdims=True) + EPS)
    return (y * g.astype(jnp.float32)).astype(x.dtype)


def partial_rope(x, pos):
    half = ROPE_DIM // 2
    inv = ROPE_THETA ** (-jnp.arange(half, dtype=jnp.float32) / half)
    ang = pos.astype(jnp.float32)[:, None] * inv[None, :]
    cos = jnp.cos(ang)[None, :, None, :]
    sin = jnp.sin(ang)[None, :, None, :]
    xf = x.astype(jnp.float32)
    x1 = xf[..., :half]
    x2 = xf[..., half:ROPE_DIM]
    out = jnp.concatenate([x1 * cos - x2 * sin, x2 * cos + x1 * sin, xf[..., ROPE_DIM:]], axis=-1)
    return out.astype(x.dtype)


def _cmul(ar, ai, br, bi):
    return ar * br - ai * bi, ar * bi + ai * br


def _ssm_combine(e1, e2):
    a1r, a1i, b1r, b1i = e1
    a2r, a2i, b2r, b2i = e2
    ar, ai = _cmul(a2r, a2i, a1r, a1i)
    br, bi = _cmul(a2r, a2i, b1r, b1i)
    return ar, ai, br + b2r, bi + b2i


def s5_mixer(u, h0_re, h0_im, a_re, a_im, log_dt, b_re, b_im, c_re, c_im, d, w_glu, b_glu):
    f32 = jnp.float32
    bsz, t, _ = u.shape
    uf = u.astype(f32).reshape(bsz, t, S5_GROUPS, S5_GROUP)
    ar = a_re.astype(f32)
    ai = a_im.astype(f32)
    dt = jnp.exp(log_dt.astype(f32))[:, None]
    mag = jnp.exp(ar * dt)
    abr = mag * jnp.cos(ai * dt)
    abi = mag * jnp.sin(ai * dt)
    den = ar * ar + ai * ai
    nr = abr - 1.0
    cr = (nr * ar + abi * ai) / den
    ci = (abi * ar - nr * ai) / den
    bur = jnp.einsum('btgh,gph->btgp', uf, b_re.astype(f32))
    bui = jnp.einsum('btgh,gph->btgp', uf, b_im.astype(f32))
    xr, xi = _cmul(cr, ci, bur, bui)
    h0r, h0i = _cmul(abr, abi, h0_re.astype(f32), h0_im.astype(f32))
    xr = xr.at[:, 0].add(h0r)
    xi = xi.at[:, 0].add(h0i)
    a_r_t = jnp.broadcast_to(abr, xr.shape)
    a_i_t = jnp.broadcast_to(abi, xr.shape)
    _, _, hr, hi = lax.associative_scan(_ssm_combine, (a_r_t, a_i_t, xr, xi), axis=1)
    y = (jnp.einsum('btgp,ghp->btgh', hr, c_re.astype(f32))
         - jnp.einsum('btgp,ghp->btgh', hi, c_im.astype(f32))
         + d.astype(f32) * uf).reshape(bsz, t, S5_WIDTH)
    y = jax.nn.gelu(y)
    y = y * jax.nn.sigmoid(y @ w_glu.astype(f32) + b_glu.astype(f32))
    return y.astype(u.dtype), hr[:, -1].astype(u.dtype), hi[:, -1].astype(u.dtype)


def moba_attend(q, k_all, v_all, q_start):
    bsz, tq, nh, hd = q.shape
    tk = k_all.shape[1]
    nblk = -(-tk // MOBA_BLOCK)
    kpad = nblk * MOBA_BLOCK - tk

    def blocks(z):
        z = jnp.pad(z, ((0, 0), (0, kpad), (0, 0), (0, 0)))
        return z.reshape(bsz, nblk, MOBA_BLOCK, nh, hd).transpose(0, 3, 1, 2, 4)

    kb = blocks(k_all)
    vb = blocks(v_all)
    qc = min(MOBA_QCHUNK, tq)
    nq = -(-tq // qc)
    tqp = nq * qc
    qh = jnp.pad(q, ((0, 0), (0, tqp - tq), (0, 0), (0, 0))).transpose(0, 2, 1, 3).astype(jnp.float32)
    qpos = q_start + jnp.arange(tqp, dtype=jnp.int32)
    own = jnp.minimum(qpos // MOBA_BLOCK, nblk - 1)
    own_sel = jnp.broadcast_to(own[None, None, :, None], (bsz, nh, tqp, 1))
    n_past_sel = min(MOBA_TOPK, nblk - 1)
    if n_past_sel > 0:
        kmean = jnp.mean(kb.astype(jnp.float32), axis=3)
        gate = jnp.einsum('bhtd,bhnd->bhtn', qh, kmean)
        fully_past = jnp.arange(nblk)[None, :] < own[:, None]
        gate = jnp.where(fully_past[None, None], gate, NEG)
        top_val, top_idx = lax.top_k(gate, n_past_sel)
        sel = jnp.concatenate([top_idx.astype(jnp.int32), own_sel], axis=-1)
        valid = jnp.concatenate([top_val > 0.5 * NEG, jnp.ones_like(own_sel, dtype=bool)], axis=-1)
    else:
        sel = own_sel
        valid = jnp.ones_like(own_sel, dtype=bool)
    nsel = sel.shape[-1]

    def to_chunks(z):
        return jnp.moveaxis(z.reshape(bsz, nh, nq, qc, *z.shape[3:]), 2, 0)

    bi = jnp.arange(bsz)[:, None, None, None]
    hi = jnp.arange(nh)[None, :, None, None]
    offs = jnp.arange(MOBA_BLOCK, dtype=jnp.int32)
    scale = hd ** -0.5

    def attend_chunk(args):
        q_c, sel_c, val_c, pos_c = args
        kg = kb[bi, hi, sel_c].astype(jnp.float32)
        vg = vb[bi, hi, sel_c].astype(jnp.float32)
        logits = jnp.einsum('bhqd,bhqnkd->bhqnk', q_c, kg) * scale
        kpos = sel_c[..., None] * MOBA_BLOCK + offs
        mask = val_c[..., None] & (kpos <= pos_c[None, None, :, None, None])
        logits = jnp.where(mask, logits, NEG)
        w = jax.nn.softmax(logits.reshape(bsz, nh, qc, nsel * MOBA_BLOCK), axis=-1).reshape(logits.shape)
        return jnp.einsum('bhqnk,bhqnkd->bhqd', w, vg)

    out = lax.map(attend_chunk, (to_chunks(qh), to_chunks(sel), to_chunks(valid), qpos.reshape(nq, qc)))
    out = jnp.moveaxis(out, 0, 2).reshape(bsz, nh, tqp, hd)[:, :, :tq]
    return out.transpose(0, 2, 1, 3).astype(q.dtype)


def stick_breaking_attend(q, k_all, v_all, q_start):
    bsz, tq, nh, hd = q.shape
    scale = hd ** -0.5
    outs = []
    for s0 in range(0, tq, SB_QBLOCK):
        s1 = min(s0 + SB_QBLOCK, tq)
        kend = max(q_start + s1 - 1, 1)
        qb = q[:, s0:s1].astype(jnp.float32)
        kb = k_all[:, :kend].astype(jnp.float32)
        vb = v_all[:, :kend].astype(jnp.float32)
        z = jnp.einsum('bqhd,bkhd->bhqk', qb, kb) * scale
        tpos = q_start + jnp.arange(s0, s1)
        strict = jnp.arange(kend)[None, :] < tpos[:, None]
        log_beta = jnp.where(strict, jax.nn.log_sigmoid(z), NEG)
        log_keep = jnp.where(strict, jax.nn.log_sigmoid(-z), 0.0)
        later = lax.cumsum(log_keep, axis=3, reverse=True) - log_keep
        w = jnp.exp(log_beta + later)
        outs.append(jnp.einsum('bhqk,bkhd->bqhd', w, vb))
    return jnp.concatenate(outs, axis=1).astype(q.dtype)


def hgrn2(q, f_logit, v, g, lb, s0, g_norm):
    f32 = jnp.float32
    bsz, t, _ = q.shape
    f = lb + (1.0 - lb) * jax.nn.sigmoid(f_logit.astype(f32))
    log_f = jnp.log(f)
    k = 1.0 - f

    def heads(z):
        return z.astype(f32).reshape(bsz, t, HGRN_HEADS, HGRN_HEAD_DIM).transpose(0, 2, 1, 3)

    L = min(HGRN_CHUNK, t)
    nc = -(-t // L)
    pad = nc * L - t

    def chunks(z):
        z = jnp.pad(z, ((0, 0), (0, 0), (0, pad), (0, 0)))
        return jnp.moveaxis(z.reshape(bsz, HGRN_HEADS, nc, L, HGRN_HEAD_DIM), 2, 0)

    tri = jnp.tril(jnp.ones((L, L), dtype=bool))

    def step(S, xs):
        qc, kc, vc, lfc = xs
        b = jnp.cumsum(lfc, axis=2)
        diff = b[:, :, :, None, :] - b[:, :, None, :, :]
        decay = jnp.exp(jnp.where(tri[:, :, None], diff, NEG))
        att = jnp.einsum('bhtd,bhtsd,bhsd->bhts', qc, decay, kc)
        o = jnp.einsum('bhts,bhsv->bhtv', att, vc) + jnp.einsum('bhtd,bhdv->bhtv', qc * jnp.exp(b), S)
        bl = b[:, :, -1:, :]
        S = jnp.exp(bl[:, :, 0, :])[..., None] * S + jnp.einsum('bhsd,bhsv->bhdv', kc * jnp.exp(bl - b), vc)
        return S, o

    s_fin, o = lax.scan(step, s0.astype(f32), (chunks(heads(q)), chunks(heads(k)), chunks(heads(v)), chunks(heads(log_f))))
    o = jnp.moveaxis(o, 0, 2).reshape(bsz, HGRN_HEADS, nc * L, HGRN_HEAD_DIM)[:, :, :t].transpose(0, 2, 1, 3)
    o = o * lax.rsqrt(jnp.mean(o * o, axis=-1, keepdims=True) + EPS) * g_norm.astype(f32)
    o = o.reshape(bsz, t, HGRN_WIDTH) * jax.nn.silu(g.astype(f32))
    return o.astype(q.dtype), s_fin.astype(q.dtype)


def conv_ffn(a, buf, w_up, w_conv, b_conv, w_down):
    t = a.shape[1]
    u = a @ w_up
    full = jnp.concatenate([buf.astype(u.dtype), u], axis=1)
    c = b_conv
    for j in range(CONV_W):
        c = c + w_conv[j] * full[:, j:j + t]
    gate, val = jnp.split(c, 2, axis=-1)
    y = (jax.nn.gelu(gate) * val) @ w_down
    return y, full[:, t:]


def setup_inputs(seed: int = 0) -> dict:
    key = jax.random.key(seed)
    keys = jax.random.split(key, 48)
    counter = iter(range(48))
    f32 = jnp.float32

    def nrm(shape, scale=1.0):
        return scale * jax.random.normal(keys[next(counter)], shape, f32)

    def gain(shape):
        return 1.0 + nrm(shape, 0.1)

    n_pages = PAST_LEN // PAGE_SIZE
    n_used = DEC_BATCH * n_pages
    n_phys = (POOL_NUM * n_used + POOL_DEN - 1) // POOL_DEN
    inp = {}
    inp['x_prompt'] = nrm((BATCH, SEQ, D_MODEL))
    inp['x_sample'] = nrm((DEC_BATCH, DEC_SEQ, D_MODEL))
    inp['p_prompt'] = nrm((DEPTH, BATCH, SEQ, PLE_DIM))
    inp['p_sample'] = nrm((DEPTH, DEC_BATCH, DEC_SEQ, PLE_DIM))
    inp['cache_moba_k'] = nrm((N_A_LAYERS, n_phys, PAGE_SIZE, MOBA_HEADS, MOBA_HEAD_DIM))
    inp['cache_moba_v'] = nrm((N_A_LAYERS, n_phys, PAGE_SIZE, MOBA_HEADS, MOBA_HEAD_DIM))
    inp['state_s5_re'] = nrm((N_A_LAYERS, DEC_BATCH, S5_GROUPS, S5_STATE), 0.1)
    inp['state_s5_im'] = nrm((N_A_LAYERS, DEC_BATCH, S5_GROUPS, S5_STATE), 0.1)
    inp['state_hgrn'] = nrm((N_C_LAYERS, DEC_BATCH, HGRN_HEADS, HGRN_HEAD_DIM, HGRN_HEAD_DIM), 0.5)
    inp['cache_sb_k'] = nrm((N_C_LAYERS, n_phys, PAGE_SIZE, SB_HEADS, SB_HEAD_DIM))
    inp['cache_sb_v'] = nrm((N_C_LAYERS, n_phys, PAGE_SIZE, SB_HEADS, SB_HEAD_DIM))
    inp['state_ffn'] = nrm((DEPTH, DEC_BATCH, CONV_W - 1, 2 * D_FF))
    perm = jax.random.permutation(keys[next(counter)], n_phys)
    inp['page_table'] = perm[:n_used].reshape(DEC_BATCH, n_pages).astype(jnp.int32)
    inp['g_mix_pre'] = gain((DEPTH, D_MODEL))
    inp['g_mix_post'] = gain((DEPTH, D_MODEL))
    inp['g_ffn_pre'] = gain((DEPTH, D_MODEL))
    inp['g_ffn_post'] = gain((DEPTH, D_MODEL))
    inp['w_ffn_up'] = nrm((DEPTH, D_MODEL, 2 * D_FF), D_MODEL ** -0.5)
    inp['w_ffn_conv'] = nrm((DEPTH, CONV_W, 2 * D_FF), CONV_W ** -0.5)
    inp['b_ffn_conv'] = nrm((DEPTH, 2 * D_FF), 0.02)
    inp['w_ffn_down'] = nrm((DEPTH, D_FF, D_MODEL), D_FF ** -0.5)
    inp['w_ple_proj'] = nrm((DEPTH, PLE_DIM, D_MODEL), PLE_DIM ** -0.5)
    inp['w_ple_gate'] = nrm((DEPTH, D_MODEL, D_MODEL), D_MODEL ** -0.5)
    inp['g_ple'] = gain((DEPTH, D_MODEL))
    inp['w_in_a'] = nrm((N_A_LAYERS, D_MODEL, IN_A), D_MODEL ** -0.5)
    inp['w_out_a'] = nrm((N_A_LAYERS, MIX_WIDTH, D_MODEL), MIX_WIDTH ** -0.5)
    inp['s5_a_re'] = -0.5 + nrm((N_A_LAYERS, S5_GROUPS, S5_STATE), 0.01)
    inp['s5_a_im'] = jnp.pi * jnp.arange(S5_STATE, dtype=f32) + nrm((N_A_LAYERS, S5_GROUPS, S5_STATE), 0.01)
    inp['s5_log_dt'] = jax.random.uniform(keys[next(counter)], (N_A_LAYERS, S5_GROUPS), f32,
                                          minval=math.log(S5_DT_MIN), maxval=math.log(S5_DT_MAX))
    inp['s5_b_re'] = nrm((N_A_LAYERS, S5_GROUPS, S5_STATE, S5_GROUP), (2 * S5_GROUP) ** -0.5)
    inp['s5_b_im'] = nrm((N_A_LAYERS, S5_GROUPS, S5_STATE, S5_GROUP), (2 * S5_GROUP) ** -0.5)
    inp['s5_c_re'] = nrm((N_A_LAYERS, S5_GROUPS, S5_GROUP, S5_STATE), (2 * S5_STATE) ** -0.5)
    inp['s5_c_im'] = nrm((N_A_LAYERS, S5_GROUPS, S5_GROUP, S5_STATE), (2 * S5_STATE) ** -0.5)
    inp['s5_d'] = nrm((N_A_LAYERS, S5_GROUPS, S5_GROUP))
    inp['s5_w_glu'] = nrm((N_A_LAYERS, S5_WIDTH, S5_WIDTH), S5_WIDTH ** -0.5)
    inp['s5_b_glu'] = nrm((N_A_LAYERS, S5_WIDTH), 0.02)
    inp['w_in_c'] = nrm((N_C_LAYERS, D_MODEL, IN_C), D_MODEL ** -0.5)
    inp['w_out_c'] = nrm((N_C_LAYERS, MIX_WIDTH, D_MODEL), MIX_WIDTH ** -0.5)
    inp['hgrn_lb'] = nrm((DEPTH, HGRN_WIDTH))
    inp['g_hgrn_norm'] = gain((N_C_LAYERS, HGRN_HEAD_DIM))
    return inp


def reference(x_prompt, x_sample, p_prompt, p_sample, cache_moba_k, cache_moba_v, state_s5_re, state_s5_im,
              state_hgrn, cache_sb_k, cache_sb_v, state_ffn, page_table,
              g_mix_pre, g_mix_post, g_ffn_pre, g_ffn_post, w_ffn_up, w_ffn_conv, b_ffn_conv, w_ffn_down,
              w_ple_proj, w_ple_gate, g_ple,
              w_in_a, w_out_a, s5_a_re, s5_a_im, s5_log_dt, s5_b_re, s5_b_im, s5_c_re, s5_c_im, s5_d,
              s5_w_glu, s5_b_glu,
              w_in_c, w_out_c, hgrn_lb, g_hgrn_norm):
    f32 = jnp.float32
    n_pages = page_table.shape[1]
    past_len = n_pages * cache_moba_k.shape[2]
    lb_soft = jax.nn.softmax(hgrn_lb.astype(f32), axis=0)
    lower_bounds = jnp.cumsum(lb_soft, axis=0) - lb_soft[0:1]
    mw = MOBA_HEADS * MOBA_HEAD_DIM
    sw = SB_HEADS * SB_HEAD_DIM

    def paged_rows(pool, j):
        rows = pool[j, page_table]
        return rows.reshape(rows.shape[0], past_len, rows.shape[3], rows.shape[4])

    def run(x, p, q_start, s5_re0, s5_im0, hgrn0, ffn0, has_past):
        bsz, t, _ = x.shape
        pos = q_start + jnp.arange(t, dtype=jnp.int32)
        h = x
        mk, mv, sr, si, hs, sk, sv, fb = [], [], [], [], [], [], [], []
        for i in range(DEPTH):
            j = i // 2
            a = rmsnorm(h, g_mix_pre[i])
            if i % 2 == 0:
                z = a @ w_in_a[j]
                u = z[..., :S5_WIDTH]
                qkv = z[..., S5_WIDTH:].reshape(bsz, t, 3, MOBA_HEADS, MOBA_HEAD_DIM)
                qm = partial_rope(qkv[:, :, 0], pos)
                km = partial_rope(qkv[:, :, 1], pos)
                vm = qkv[:, :, 2]
                y_s5, hr, hi = s5_mixer(u, s5_re0[j], s5_im0[j], s5_a_re[j], s5_a_im[j], s5_log_dt[j],
                                        s5_b_re[j], s5_b_im[j], s5_c_re[j], s5_c_im[j], s5_d[j],
                                        s5_w_glu[j], s5_b_glu[j])
                if has_past:
                    k_all = jnp.concatenate([paged_rows(cache_moba_k, j).astype(km.dtype), km], axis=1)
                    v_all = jnp.concatenate([paged_rows(cache_moba_v, j).astype(vm.dtype), vm], axis=1)
                else:
                    k_all, v_all = km, vm
                y_mb = moba_attend(qm, k_all, v_all, q_start)
                mix = jnp.concatenate([y_s5, y_mb.reshape(bsz, t, mw)], axis=-1) @ w_out_a[j]
                mk.append(km)
                mv.append(vm)
                sr.append(hr)
                si.append(hi)
            else:
                z = a @ w_in_c[j]
                q_h = z[..., 0:HGRN_WIDTH]
                f_h = z[..., HGRN_WIDTH:2 * HGRN_WIDTH]
                i_h = z[..., 2 * HGRN_WIDTH:3 * HGRN_WIDTH]
                g_h = z[..., 3 * HGRN_WIDTH:4 * HGRN_WIDTH]
                qkv = z[..., 4 * HGRN_WIDTH:].reshape(bsz, t, 3, SB_HEADS, SB_HEAD_DIM)
                qd = qkv[:, :, 0]
                kd = qkv[:, :, 1]
                vd = qkv[:, :, 2]
                y_hg, s_fin = hgrn2(q_h, f_h, i_h, g_h, lower_bounds[i], hgrn0[j], g_hgrn_norm[j])
                if has_past:
                    k_all = jnp.concatenate([paged_rows(cache_sb_k, j).astype(kd.dtype), kd], axis=1)
                    v_all = jnp.concatenate([paged_rows(cache_sb_v, j).astype(vd.dtype), vd], axis=1)
                else:
                    k_all, v_all = kd, vd
                y_sb = stick_breaking_attend(qd, k_all, v_all, q_start)
                mix = jnp.concatenate([y_hg, y_sb.reshape(bsz, t, sw)], axis=-1) @ w_out_c[j]
                hs.append(s_fin)
                sk.append(kd)
                sv.append(vd)
            h = h + rmsnorm(mix, g_mix_post[i])
            f, buf = conv_ffn(rmsnorm(h, g_ffn_pre[i]), ffn0[i], w_ffn_up[i], w_ffn_conv[i], b_ffn_conv[i], w_ffn_down[i])
            h = h + rmsnorm(f, g_ffn_post[i])
            fb.append(buf)
            ple = (p[i] @ w_ple_proj[i]) * jax.nn.sigmoid(h @ w_ple_gate[i])
            h = h + rmsnorm(ple, g_ple[i])
        return (h, jnp.stack(mk), jnp.stack(mv), jnp.stack(sr), jnp.stack(si), jnp.stack(hs),
                jnp.stack(sk), jnp.stack(sv), jnp.stack(fb))

    bp = x_prompt.shape[0]
    dt = x_prompt.dtype
    (y_prompt, moba_k_prompt, moba_v_prompt, s5_re_prompt, s5_im_prompt, hgrn_prompt,
     sb_k_prompt, sb_v_prompt, ffn_prompt) = run(
        x_prompt, p_prompt, 0,
        jnp.zeros((N_A_LAYERS, bp, S5_GROUPS, S5_STATE), dt),
        jnp.zeros((N_A_LAYERS, bp, S5_GROUPS, S5_STATE), dt),
        jnp.zeros((N_C_LAYERS, bp, HGRN_HEADS, HGRN_HEAD_DIM, HGRN_HEAD_DIM), dt),
        jnp.zeros((DEPTH, bp, CONV_W - 1, 2 * D_FF), dt),
        False)
    (y_sample, moba_k_sample, moba_v_sample, s5_re_sample, s5_im_sample, hgrn_sample,
     sb_k_sample, sb_v_sample, ffn_sample) = run(
        x_sample, p_sample, past_len, state_s5_re, state_s5_im, state_hgrn, state_ffn, True)
    return (y_prompt, y_sample,
            moba_k_prompt, moba_v_prompt, s5_re_prompt, s5_im_prompt, hgrn_prompt, sb_k_prompt, sb_v_prompt, ffn_prompt,
            moba_k_sample, moba_v_sample, s5_re_sample, s5_im_sample, hgrn_sample, sb_k_sample, sb_v_sample, ffn_sample)
```

```python
import functools
import math

import jax
import jax.numpy as jnp
from jax import lax
from jax.experimental import pallas as pl
from jax.experimental.pallas import tpu as pltpu

F32 = jnp.float32
BF16 = jnp.bfloat16

S5_WIDTH = 512
S5_GROUPS = 32
S5_GROUP = 16
S5_STATE = 64
S5_LANES = S5_GROUPS * S5_STATE
HEADS = 8
HEAD_DIM = 64
ATT_WIDTH = HEADS * HEAD_DIM
ROPE_DIM = 16
ROPE_THETA = 500000.0
MOBA_BLOCK = 256
MOBA_TOPK = 3
HGRN_HEADS = 4
HGRN_DIM = 128
HGRN_CHUNK = 64
HGRN_SUB = 16
D_FF = 2816
FF_CHUNK = 256
N_FF_CHUNKS = D_FF // FF_CHUNK
PAGE = 128
EPS = 1e-6
NEG = -1e30
SLAB = 512
LANES = 128
SUBLANES = 8
ROW_TILE = 256
SB_BLOCK = 128
PAGES_PER_STEP = 8
EXP_UNDERFLOW = -104.0


def _dot(a, b):
    return jnp.dot(a, b, preferred_element_type=F32)


def _dot_nt(a, b):
    return lax.dot_general(a, b, (((1,), (1,)), ((), ())), preferred_element_type=F32)


def _dot_tn(a, b):
    return lax.dot_general(a, b, (((0,), (0,)), ((), ())), preferred_element_type=F32)


def _split3(x):
    hi = x.astype(BF16)
    r1 = x - hi.astype(F32)
    mid = r1.astype(BF16)
    lo = (r1 - mid.astype(F32)).astype(BF16)
    return hi, mid, lo


def _rms(x, g):
    return x * lax.rsqrt(jnp.mean(x * x, axis=-1, keepdims=True) + EPS) * g


def _sigmoid(x):
    return 1.0 / (1.0 + jnp.exp(-x))


def _gelu(x):
    return 0.5 * x * (1.0 + jnp.tanh(0.7978845608028654 * (x + 0.044715 * (x * x * x))))


def _full(shape):
    nd = len(shape)
    return pl.BlockSpec(shape, lambda *_: (0,) * nd)


def _const(shape):
    nd = len(shape)
    return pl.BlockSpec(shape, lambda *_: (0,) * nd, pipeline_mode=pl.Buffered(1))


def _rope_table_kernel(c_ref, s1_ref, s2_ref, *, q_start):
    rows, lanes = c_ref.shape
    i = pl.program_id(0)
    lane = lax.broadcasted_iota(jnp.int32, (rows, lanes), 1)
    pos = lax.broadcasted_iota(jnp.int32, (rows, lanes), 0) + i * rows + q_start
    d = lane % HEAD_DIM
    half = ROPE_DIM // 2
    inv = jnp.exp((d % half).astype(F32) * (-math.log(ROPE_THETA) / half))
    ang = pos.astype(F32) * inv
    cs = jnp.cos(ang)
    sn = jnp.sin(ang)
    c_ref[...] = jnp.where(d < ROPE_DIM, cs, 1.0)
    s1_ref[...] = jnp.where(d < half, -sn, 0.0)
    s2_ref[...] = jnp.where((d >= half) & (d < ROPE_DIM), sn, 0.0)


def _rope_tables(t, q_start):
    rows = min(t, 512)
    sds = jax.ShapeDtypeStruct((t, LANES), F32)
    spec = pl.BlockSpec((rows, LANES), lambda i: (i, 0))
    return pl.pallas_call(
        functools.partial(_rope_table_kernel, q_start=q_start),
        grid=(t // rows,), out_shape=(sds, sds, sds), out_specs=(spec, spec, spec),
        name="rope_tables")()


def _inproj_kernel(h_ref, g_ref, w_ref, c_ref, s1_ref, s2_ref, *out_refs, rope_slabs):
    a = _rms(h_ref[...], g_ref[...]).astype(BF16)
    half = ROPE_DIM // 2
    for s, o_ref in enumerate(out_refs):
        z = _dot(a, w_ref[:, s * SLAB:(s + 1) * SLAB])
        if s in rope_slabs:
            c = c_ref[...]
            s1 = s1_ref[...]
            s2 = s2_ref[...]
            for j in range(SLAB // LANES):
                x = z[:, j * LANES:(j + 1) * LANES]
                o_ref[:, j * LANES:(j + 1) * LANES] = (
                    x * c + pltpu.roll(x, LANES - half, 1) * s1 + pltpu.roll(x, half, 1) * s2)
        else:
            o_ref[...] = z


def _inproj(h, g, w, tables, rope_slabs):
    n, d = h.shape
    n_slabs = w.shape[1] // SLAB
    tm = min(ROW_TILE, n)
    t_tiles = tables[0].shape[0] // tm
    tab_spec = pl.BlockSpec((tm, LANES), lambda i: (i % t_tiles, 0))
    out_spec = pl.BlockSpec((tm, SLAB), lambda i: (i, 0))
    return pl.pallas_call(
        functools.partial(_inproj_kernel, rope_slabs=rope_slabs),
        grid=(n // tm,),
        in_specs=[pl.BlockSpec((tm, d), lambda i: (i, 0)), _const((1, d)), _const(w.shape),
                  tab_spec, tab_spec, tab_spec],
        out_specs=[out_spec] * n_slabs,
        out_shape=[jax.ShapeDtypeStruct((n, SLAB), F32)] * n_slabs,
        name="inproj")(h, g.reshape(1, d), w, *tables)


def _s5_disc_kernel(ar_ref, ai_ref, ldt_ref, pre_ref, pim_ref, cc_ref):
    ar = ar_ref[...]
    ai = ai_ref[...]
    dt = jnp.exp(ldt_ref[...])
    row = lax.broadcasted_iota(jnp.int32, (2 * SUBLANES, 1), 0)
    n = jnp.where(row < SUBLANES, row + 1,
                  jnp.where(row == SUBLANES + 1, 2, jnp.where(row == SUBLANES + 2, 4, 1))).astype(F32)
    mag = jnp.exp(ar * dt * n)
    ang = ai * dt * n
    pre = mag * jnp.cos(ang)
    pim = mag * jnp.sin(ang)
    pre_ref[...] = pre
    pim_ref[...] = pim
    abr = pre[0:1]
    abi = pim[0:1]
    den = ar * ar + ai * ai
    nr = abr - 1.0
    cc_ref[0:1, :] = (nr * ar + abi * ai) / den
    cc_ref[1:2, :] = (abi * ar - nr * ai) / den


def _s5_discretize(a_re, a_im, log_dt):
    ar = a_re.reshape(1, S5_LANES)
    ai = a_im.reshape(1, S5_LANES)
    ldt = jnp.repeat(log_dt, S5_STATE).reshape(1, S5_LANES)
    return pl.pallas_call(
        _s5_disc_kernel,
        out_shape=(jax.ShapeDtypeStruct((2 * SUBLANES, S5_LANES), F32),
                   jax.ShapeDtypeStruct((2 * SUBLANES, S5_LANES), F32),
                   jax.ShapeDtypeStruct((2, S5_LANES), F32)),
        name="s5_discretize")(ar, ai, ldt)


SCAN_LANES = 512


def _s5_kernel(u_ref, h0r_ref, h0i_ref, pre_ref, pim_ref, cc_ref, wb_ref, wc_ref, d_ref,
               wglu_ref, bglu_ref, y_ref, hr_out_ref, hi_out_ref, xr_s, xi_s, car_s, *, chain):
    tm = u_ref.shape[0]
    nb = tm // SUBLANES
    u = u_ref[...]
    ub = u.astype(BF16)
    half_in = S5_WIDTH // 2
    half_st = S5_LANES // 2
    for hf in range(2):
        bu = _dot(ub[:, hf * half_in:(hf + 1) * half_in], wb_ref[hf])
        bur = bu[:, :half_st]
        bui = bu[:, half_st:]
        ls = slice(hf * half_st, (hf + 1) * half_st)
        cr = cc_ref[0:1, ls]
        ci = cc_ref[1:2, ls]
        xr_s[:, ls] = cr * bur - ci * bui
        xi_s[:, ls] = cr * bui + ci * bur

    if chain:
        t = pl.program_id(1)

        @pl.when(t == 0)
        def _():
            car_s[0:1, :] = h0r_ref[0]
            car_s[1:2, :] = h0i_ref[0]

    rowi = lax.broadcasted_iota(jnp.int32, (SUBLANES, SCAN_LANES), 0)
    for lc in range(S5_LANES // SCAN_LANES):
        ls = slice(lc * SCAN_LANES, (lc + 1) * SCAN_LANES)
        pre = pre_ref[:, ls]
        pim = pim_ref[:, ls]

        def body(i, carry, ls=ls, pre=pre, pim=pim):
            r0 = pl.multiple_of(i * SUBLANES, SUBLANES)
            hr = xr_s[pl.ds(r0, SUBLANES), ls]
            hi = xi_s[pl.ds(r0, SUBLANES), ls]
            for k, prow in ((1, SUBLANES), (2, SUBLANES + 1), (4, SUBLANES + 2)):
                ar = pre[prow:prow + 1]
                ai = pim[prow:prow + 1]
                sr = jnp.where(rowi >= k, pltpu.roll(hr, k, 0), 0.0)
                si = jnp.where(rowi >= k, pltpu.roll(hi, k, 0), 0.0)
                hr, hi = hr + ar * sr - ai * si, hi + ar * si + ai * sr
            if chain:
                cr_, ci_ = carry
            else:
                cr_ = h0r_ref[i, :, ls]
                ci_ = h0i_ref[i, :, ls]
            pr8 = pre[0:SUBLANES]
            pi8 = pim[0:SUBLANES]
            hr, hi = hr + pr8 * cr_ - pi8 * ci_, hi + pr8 * ci_ + pi8 * cr_
            xr_s[pl.ds(r0, SUBLANES), ls] = hr
            xi_s[pl.ds(r0, SUBLANES), ls] = hi
            last_r = hr[SUBLANES - 1:SUBLANES]
            last_i = hi[SUBLANES - 1:SUBLANES]
            if chain:
                return last_r, last_i
            hr_out_ref[i, :, ls] = last_r
            hi_out_ref[i, :, ls] = last_i
            return carry

        if chain:
            cr_, ci_ = lax.fori_loop(0, nb, body, (car_s[0:1, ls], car_s[1:2, ls]))
            car_s[0:1, ls] = cr_
            car_s[1:2, ls] = ci_
        else:
            lax.fori_loop(0, nb, body, 0)

    if chain:
        hr_out_ref[0] = car_s[0:1, :]
        hi_out_ref[0] = car_s[1:2, :]

    ys = []
    for hf in range(2):
        ls = slice(hf * half_st, (hf + 1) * half_st)
        hcat = jnp.concatenate([xr_s[:, ls], xi_s[:, ls]], axis=1).astype(BF16)
        ys.append(_dot(hcat, wc_ref[hf]))
    y = jnp.concatenate(ys, axis=1) + d_ref[...] * u
    y = _gelu(y)
    y_ref[...] = y * _sigmoid(_dot(y.astype(BF16), wglu_ref[...]) + bglu_ref[...])


def _s5_weights(b_re, b_im, c_re, c_im):
    gh = S5_GROUPS // 2
    eye = jnp.eye(gh, dtype=F32)

    def bmat(b):
        bt = b.astype(F32).transpose(0, 2, 1).reshape(2, gh, S5_GROUP, S5_STATE)
        return jnp.einsum('xghp,gk->xghkp', bt, eye).reshape(2, gh * S5_GROUP, gh * S5_STATE)

    def cmat(c):
        ct = c.astype(F32).transpose(0, 2, 1).reshape(2, gh, S5_STATE, S5_GROUP)
        return jnp.einsum('xgph,gk->xgpkh', ct, eye).reshape(2, gh * S5_STATE, gh * S5_GROUP)

    wb = jnp.concatenate([bmat(b_re), bmat(b_im)], axis=2).astype(BF16)
    wc = jnp.concatenate([cmat(c_re), -cmat(c_im)], axis=1).astype(BF16)
    return wb, wc


def _s5(u, h0r, h0i, disc, wb, wc, dvec, wglu, bglu, *, bsz, t):
    n = bsz * t
    pre, pim, cc = disc
    chain = t > SUBLANES
    if chain:
        tm = min(ROW_TILE, t)
        grid = (bsz, t // tm)
        row_spec = pl.BlockSpec((tm, S5_WIDTH), lambda b, i: (b * (t // tm) + i, 0))
        st_spec = pl.BlockSpec((1, 1, S5_LANES), lambda b, i: (b, 0, 0))
    else:
        assert t == SUBLANES
        tm = n
        grid = (1, 1)
        row_spec = pl.BlockSpec((tm, S5_WIDTH), lambda b, i: (0, 0))
        st_spec = pl.BlockSpec((bsz, 1, S5_LANES), lambda b, i: (0, 0, 0))
    st_sds = jax.ShapeDtypeStruct((bsz, 1, S5_LANES), F32)
    return pl.pallas_call(
        functools.partial(_s5_kernel, chain=chain),
        grid=grid,
        in_specs=[row_spec, st_spec, st_spec, _const(pre.shape), _const(pim.shape), _const(cc.shape),
                  _const(wb.shape), _const(wc.shape), _const((1, S5_WIDTH)), _const(wglu.shape),
                  _const((1, S5_WIDTH))],
        out_specs=[row_spec, st_spec, st_spec],
        out_shape=[jax.ShapeDtypeStruct((n, S5_WIDTH), F32), st_sds, st_sds],
        scratch_shapes=[pltpu.VMEM((tm, S5_LANES), F32), pltpu.VMEM((tm, S5_LANES), F32),
                        pltpu.VMEM((2, S5_LANES), F32)],
        name="s5_mixer")(u, h0r, h0i, pre, pim, cc, wb, wc, dvec.reshape(1, S5_WIDTH), wglu,
                         bglu.reshape(1, S5_WIDTH))


def _top_blocks(gate, n_past, col):
    g = jnp.where(col < n_past, gate, NEG)
    sel = jnp.zeros(gate.shape, F32)
    colf = col.astype(F32)
    for _ in range(MOBA_TOPK):
        m = jnp.max(g, axis=-1, keepdims=True)
        first = jnp.min(jnp.where(g == m, colf, float(LANES)), axis=-1, keepdims=True)
        pick = colf == first
        sel = jnp.where(pick & (m > 0.5 * NEG), 1.0, sel)
        g = jnp.where(pick, -3e38, g)
    return sel


def _moba_prompt_kernel(q_ref, k_ref, v_ref, o_ref, kmean_s, *, nblk):
    bs = MOBA_BLOCK
    lane = lax.broadcasted_iota(jnp.int32, (1, LANES), 1)
    kmean_s[...] = jnp.zeros(kmean_s.shape, F32)
    for j in range(nblk):
        kmean_s[j:j + 1, :] = jnp.sum(k_ref[j * bs:(j + 1) * bs, :], axis=0, keepdims=True) * (1.0 / bs)
    kmb = kmean_s[...].astype(BF16)
    colb = lax.broadcasted_iota(jnp.int32, (bs, LANES), 1)
    rowq = lax.broadcasted_iota(jnp.int32, (bs, bs), 0)
    colk = lax.broadcasted_iota(jnp.int32, (bs, bs), 1)

    def qblock(i, _):
        r0 = pl.multiple_of(i * bs, bs)
        qf = q_ref[pl.ds(r0, bs), :]
        ki = k_ref[pl.ds(r0, bs), :].astype(BF16)
        vi = v_ref[pl.ds(r0, bs), :].astype(BF16)
        outs = []
        for hd in range(2):
            lm = (lane >= HEAD_DIM * hd) & (lane < HEAD_DIM * (hd + 1))
            qh = jnp.where(lm, qf, 0.0)
            qs = (qh * HEAD_DIM ** -0.5).astype(BF16)
            sel = _top_blocks(_dot_nt(qh.astype(BF16), kmb), i, colb)
            s = jnp.where(colk <= rowq, _dot_nt(qs, ki), NEG)
            m = jnp.max(s, axis=-1, keepdims=True)
            p = jnp.exp(s - m)
            l = jnp.sum(p, axis=-1, keepdims=True)
            acc = _dot(p.astype(BF16), vi)

            def kvblock(j, carry, qs=qs, sel=sel):
                m, l, acc = carry
                c0 = pl.multiple_of(j * bs, bs)
                kj = k_ref[pl.ds(c0, bs), :].astype(BF16)
                vj = v_ref[pl.ds(c0, bs), :].astype(BF16)
                selc = jnp.sum(jnp.where(colb == j, sel, 0.0), axis=-1, keepdims=True)
                s = jnp.where(selc > 0.5, _dot_nt(qs, kj), NEG)
                m_new = jnp.maximum(m, jnp.max(s, axis=-1, keepdims=True))
                alpha = jnp.exp(m - m_new)
                p = jnp.exp(s - m_new)
                l = alpha * l + jnp.sum(p, axis=-1, keepdims=True)
                acc = alpha * acc + _dot(p.astype(BF16), vj)
                return m_new, l, acc

            m, l, acc = lax.fori_loop(0, i, kvblock, (m, l, acc))
            outs.append(acc / l)
        o_ref[pl.ds(r0, bs), :] = jnp.where(lane < HEAD_DIM, outs[0], outs[1])
        return 0

    lax.fori_loop(0, nblk, qblock, 0)


def _moba_prompt(q, k, v, *, bsz, t):
    assert t % MOBA_BLOCK == 0 and t // MOBA_BLOCK <= LANES
    n = bsz * t
    spec = pl.BlockSpec((t, LANES), lambda b, hp: (b, hp))
    return pl.pallas_call(
        functools.partial(_moba_prompt_kernel, nblk=t // MOBA_BLOCK),
        grid=(bsz, ATT_WIDTH // LANES),
        in_specs=[spec, spec, spec], out_specs=spec,
        out_shape=jax.ShapeDtypeStruct((n, ATT_WIDTH), F32),
        scratch_shapes=[pltpu.VMEM((LANES, LANES), F32)],
        name="moba_prompt")(q, k, v)


def _expand_heads(q, qexp_s):
    t = q.shape[0]
    lane = lax.broadcasted_iota(jnp.int32, (1, ATT_WIDTH), 1)
    for h in range(HEADS):
        lm = (lane >= h * HEAD_DIM) & (lane < (h + 1) * HEAD_DIM)
        qexp_s[h * t:(h + 1) * t, :] = jnp.where(lm, q, 0.0)


def _collapse_heads(acc, t):
    lane = lax.broadcasted_iota(jnp.int32, (1, ATT_WIDTH), 1)
    out = jnp.zeros((t, ATT_WIDTH), F32)
    for h in range(HEADS):
        lm = (lane >= h * HEAD_DIM) & (lane < (h + 1) * HEAD_DIM)
        out = out + jnp.where(lm, acc[h * t:(h + 1) * t, :], 0.0)
    return out


def _moba_sample_kernel(pt_ref, q_ref, kn_ref, vn_ref, *rest, n_pages, t):
    pps = PAGES_PER_STEP
    k_refs = rest[:pps]
    v_refs = rest[pps:2 * pps]
    o_ref = rest[2 * pps]
    qexp_s, s_all, p_all, gate_s, new_s, stat_s, acc_s = rest[2 * pps + 1:]
    ns = n_pages // pps
    s_id = pl.program_id(1)
    rows = HEADS * t
    col = lax.broadcasted_iota(jnp.int32, (rows, LANES), 1)
    scale = HEAD_DIM ** -0.5
    pages_per_block = MOBA_BLOCK // PAGE

    @pl.when(s_id == 0)
    def _():
        _expand_heads(q_ref[...], qexp_s)
        gate_s[...] = jnp.zeros(gate_s.shape, F32)

    @pl.when(s_id < ns)
    def _():
        qe = qexp_s[...].astype(BF16)
        g = gate_s[...]
        for i in range(pps):
            pg = s_id * pps + i
            s = _dot_nt(qe, k_refs[i][...].astype(BF16))
            s_all[pg] = s
            g = g + jnp.where(col == pg // pages_per_block,
                              jnp.sum(s, axis=-1, keepdims=True) * (1.0 / MOBA_BLOCK), 0.0)
        gate_s[...] = g

    @pl.when(s_id == ns)
    def _():
        sel = _top_blocks(gate_s[...], n_pages // pages_per_block, col)
        qe = qexp_s[...].astype(BF16)
        new_s[...] = jnp.zeros(new_s.shape, F32)
        new_s[0:t, :] = kn_ref[...]
        sn = _dot_nt(qe, new_s[...].astype(BF16)) * scale
        rowt = lax.broadcasted_iota(jnp.int32, (rows, LANES), 0) % t
        sn = jnp.where(col <= rowt, sn, NEG)
        m0 = jnp.max(sn, axis=-1, keepdims=True)

        def mx(pg, m):
            selc = jnp.sum(jnp.where(col == pg // pages_per_block, sel, 0.0), axis=-1, keepdims=True)
            s = jnp.where(selc > 0.5, s_all[pg] * scale, NEG)
            return jnp.maximum(m, jnp.max(s, axis=-1, keepdims=True))

        m = lax.fori_loop(0, n_pages, mx, m0)
        pn = jnp.exp(sn - m)
        l0 = jnp.sum(pn, axis=-1, keepdims=True)

        def pr(pg, l):
            selc = jnp.sum(jnp.where(col == pg // pages_per_block, sel, 0.0), axis=-1, keepdims=True)
            p = jnp.exp(jnp.where(selc > 0.5, s_all[pg] * scale, NEG) - m)
            p_all[pg] = p.astype(BF16)
            return l + jnp.sum(p, axis=-1, keepdims=True)

        l = lax.fori_loop(0, n_pages, pr, l0)
        stat_s[...] = jnp.broadcast_to(l, stat_s.shape)
        new_s[...] = jnp.zeros(new_s.shape, F32)
        new_s[0:t, :] = vn_ref[...]
        acc_s[...] = _dot(pn.astype(BF16), new_s[...].astype(BF16))

    @pl.when(s_id >= ns)
    def _():
        acc = acc_s[...]
        for i in range(pps):
            pg = (s_id - ns) * pps + i
            acc = acc + _dot(p_all[pg], v_refs[i][...].astype(BF16))
        acc_s[...] = acc

    @pl.when(s_id == 2 * ns - 1)
    def _():
        o_ref[...] = _collapse_heads(acc_s[...] / stat_s[:, 0:1], t)


def _moba_sample(q, k_new, v_new, k_pool, v_pool, page_table, *, bsz, t):
    n_pages = page_table.shape[1]
    pps = PAGES_PER_STEP
    assert n_pages % pps == 0 and (n_pages * PAGE) % MOBA_BLOCK == 0 and t <= SUBLANES
    assert n_pages * PAGE // MOBA_BLOCK <= LANES
    ns = n_pages // pps
    rows = HEADS * t
    row_spec = pl.BlockSpec((t, ATT_WIDTH), lambda b, s, pt: (b, 0))

    def k_spec(i):
        return pl.BlockSpec((None, PAGE, ATT_WIDTH),
                            lambda b, s, pt: (pt[b, jnp.minimum(s, ns - 1) * pps + i], 0, 0))

    def v_spec(i):
        return pl.BlockSpec((None, PAGE, ATT_WIDTH),
                            lambda b, s, pt: (pt[b, jnp.maximum(s - ns, 0) * pps + i], 0, 0))

    grid_spec = pltpu.PrefetchScalarGridSpec(
        num_scalar_prefetch=1, grid=(bsz, 2 * ns),
        in_specs=[row_spec, row_spec, row_spec] + [k_spec(i) for i in range(pps)]
                 + [v_spec(i) for i in range(pps)],
        out_specs=row_spec,
        scratch_shapes=[pltpu.VMEM((rows, ATT_WIDTH), F32),
                        pltpu.VMEM((n_pages, rows, PAGE), F32),
                        pltpu.VMEM((n_pages, rows, PAGE), BF16),
                        pltpu.VMEM((rows, LANES), F32),
                        pltpu.VMEM((PAGE, ATT_WIDTH), F32),
                        pltpu.VMEM((rows, LANES), F32),
                        pltpu.VMEM((rows, ATT_WIDTH), F32)])
    return pl.pallas_call(
        functools.partial(_moba_sample_kernel, n_pages=n_pages, t=t),
        grid_spec=grid_spec,
        out_shape=jax.ShapeDtypeStruct((bsz * t, ATT_WIDTH), F32),
        name="moba_sample")(page_table, q, k_new, v_new, *([k_pool] * pps), *([v_pool] * pps))


def _sb_block(z, r, v_bf16, tri, strict):
    lg = jnp.log(1.0 + jnp.exp(-jnp.abs(z)))
    log_beta = jnp.minimum(z, 0.0) - lg
    log_keep = jnp.minimum(-z, 0.0) - lg
    if strict is not None:
        log_beta = jnp.where(strict, log_beta, NEG)
        log_keep = jnp.where(strict, log_keep, 0.0)
    hi, mid, lo = _split3(log_keep)
    later = _dot(hi, tri) + _dot(mid, tri) + _dot(lo, tri)
    w = jnp.exp(log_beta + later + r)
    return _dot(w.astype(BF16), v_bf16), r + jnp.sum(log_keep, axis=-1, keepdims=True)


def _suffix_matrix(n):
    return (lax.broadcasted_iota(jnp.int32, (n, n), 0)
            > lax.broadcasted_iota(jnp.int32, (n, n), 1)).astype(BF16)


def _sb_prompt_kernel(q_ref, k_ref, v_ref, o_ref):
    bs = SB_BLOCK
    i = pl.program_id(2)
    lane = lax.broadcasted_iota(jnp.int32, (1, LANES), 1)
    tri = _suffix_matrix(bs)
    strict = (lax.broadcasted_iota(jnp.int32, (bs, bs), 1)
              < lax.broadcasted_iota(jnp.int32, (bs, bs), 0))
    qf = q_ref[...]
    outs = []
    for hd in range(2):
        lm = (lane >= HEAD_DIM * hd) & (lane < HEAD_DIM * (hd + 1))
        qs = (jnp.where(lm, qf, 0.0) * HEAD_DIM ** -0.5).astype(BF16)

        def block(j, r, mask, qs=qs):
            c0 = pl.multiple_of(j * bs, bs)
            kj = k_ref[pl.ds(c0, bs), :].astype(BF16)
            vj = v_ref[pl.ds(c0, bs), :].astype(BF16)
            return _sb_block(_dot_nt(qs, kj), r, vj, tri, mask)

        acc, r = block(i, jnp.zeros((bs, 1), F32), strict)

        def cond(c):
            j, r, _ = c
            return (j >= 0) & (jnp.max(r) > EXP_UNDERFLOW)

        def body(c, block=block):
            j, r, acc = c
            da, r = block(j, r, None)
            return j - 1, r, acc + da

        _, _, acc = lax.while_loop(cond, body, (i - 1, r, acc))
        outs.append(acc)
    o_ref[...] = jnp.where(lane < HEAD_DIM, outs[0], outs[1])


def _sb_prompt(q, k, v, *, bsz, t):
    assert t % SB_BLOCK == 0
    n = bsz * t
    nq = t // SB_BLOCK
    q_spec = pl.BlockSpec((SB_BLOCK, LANES), lambda b, hp, i: (b * nq + i, hp))
    kv_spec = pl.BlockSpec((t, LANES), lambda b, hp, i: (b, hp))
    return pl.pallas_call(
        _sb_prompt_kernel,
        grid=(bsz, ATT_WIDTH // LANES, nq),
        in_specs=[q_spec, kv_spec, kv_spec], out_specs=q_spec,
        out_shape=jax.ShapeDtypeStruct((n, ATT_WIDTH), F32),
        name="sb_prompt")(q, k, v)


def _sb_sample_kernel(pt_ref, q_ref, kn_ref, vn_ref, *rest, n_pages, t):
    pps = PAGES_PER_STEP
    k_refs = rest[:pps]
    v_refs = rest[pps:2 * pps]
    o_ref = rest[2 * pps]
    qexp_s, new_s, r_s, acc_s = rest[2 * pps + 1:]
    s_id = pl.program_id(1)
    rows = HEADS * t
    scale = HEAD_DIM ** -0.5
    tri = _suffix_matrix(PAGE)

    @pl.when(s_id == 0)
    def _():
        _expand_heads(q_ref[...] * scale, qexp_s)
        qe = qexp_s[...].astype(BF16)
        new_s[...] = jnp.zeros(new_s.shape, F32)
        new_s[0:t, :] = kn_ref[...]
        z = _dot_nt(qe, new_s[...].astype(BF16))
        col = lax.broadcasted_iota(jnp.int32, (rows, PAGE), 1)
        rowt = lax.broadcasted_iota(jnp.int32, (rows, PAGE), 0) % t
        new_s[...] = jnp.zeros(new_s.shape, F32)
        new_s[0:t, :] = vn_ref[...]
        acc, r = _sb_block(z, jnp.zeros((rows, 1), F32), new_s[...].astype(BF16), tri, col < rowt)
        acc_s[...] = acc
        r_s[...] = jnp.broadcast_to(r, r_s.shape)

    for i in range(pps):
        @pl.when(jnp.max(r_s[...]) > EXP_UNDERFLOW)
        def _(i=i):
            qe = qexp_s[...].astype(BF16)
            z = _dot_nt(qe, k_refs[i][...].astype(BF16))
            da, r = _sb_block(z, r_s[:, 0:1], v_refs[i][...].astype(BF16), tri, None)
            acc_s[...] = acc_s[...] + da
            r_s[...] = jnp.broadcast_to(r, r_s.shape)

    @pl.when(s_id == pl.num_programs(1) - 1)
    def _():
        o_ref[...] = _collapse_heads(acc_s[...], t)


def _sb_sample(q, k_new, v_new, k_pool, v_pool, page_table, *, bsz, t):
    n_pages = page_table.shape[1]
    pps = PAGES_PER_STEP
    assert n_pages % pps == 0 and t <= SUBLANES
    ns = n_pages // pps
    rows = HEADS * t
    row_spec = pl.BlockSpec((t, ATT_WIDTH), lambda b, s, pt: (b, 0))

    def page_spec(i):
        return pl.BlockSpec((None, PAGE, ATT_WIDTH),
                            lambda b, s, pt: (pt[b, n_pages - 1 - (s * pps + i)], 0, 0))

    grid_spec = pltpu.PrefetchScalarGridSpec(
        num_scalar_prefetch=1, grid=(bsz, ns),
        in_specs=[row_spec, row_spec, row_spec] + [page_spec(i) for i in range(pps)] * 2,
        out_specs=row_spec,
        scratch_shapes=[pltpu.VMEM((rows, ATT_WIDTH), F32),
                        pltpu.VMEM((PAGE, ATT_WIDTH), F32),
                        pltpu.VMEM((rows, LANES), F32),
                        pltpu.VMEM((rows, ATT_WIDTH), F32)])
    return pl.pallas_call(
        functools.partial(_sb_sample_kernel, n_pages=n_pages, t=t),
        grid_spec=grid_spec,
        out_shape=jax.ShapeDtypeStruct((bsz * t, ATT_WIDTH), F32),
        name="sb_sample")(page_table, q, k_new, v_new, *([k_pool] * pps), *([v_pool] * pps))


def _hgrn_kernel(q_ref, f_ref, i_ref, g_ref, lb_ref, s0_ref, gn_ref, o_ref, s_out_ref, st_s,
                 *, layer, t_valid):
    tb = q_ref.shape[0]
    t = pl.program_id(1)

    @pl.when(t == 0)
    def _():
        for hh in range(HGRN_HEADS):
            st_s[hh] = s0_ref[0, hh].T

    lbp = lb_ref[...]
    e = jnp.exp(lbp - jnp.max(lbp, axis=0, keepdims=True))
    soft = e / jnp.sum(e, axis=0, keepdims=True)
    lbv = jnp.sum(soft[1:layer + 1], axis=0, keepdims=True)

    pad = max(HGRN_SUB - tb, 0)
    L = min(HGRN_CHUNK, tb + pad)
    c = min(HGRN_SUB, L)
    tri = (lax.broadcasted_iota(jnp.int32, (L, L), 0)
           >= lax.broadcasted_iota(jnp.int32, (L, L), 1)).astype(BF16)
    rowc = lax.broadcasted_iota(jnp.int32, (c, 1), 0)
    gn = gn_ref[...]

    def padrows(x):
        if pad == 0:
            return x
        return jnp.concatenate([x, jnp.zeros((pad, x.shape[1]), F32)], axis=0)

    def chunk(rows, n_valid):
        for hh in range(HGRN_HEADS):
            ls = slice(hh * HGRN_DIM, (hh + 1) * HGRN_DIM)
            lbh = lbv[:, ls]
            f = lbh + (1.0 - lbh) * _sigmoid(f_ref[rows, ls])
            lf = padrows(jnp.log(f))
            kk = padrows(1.0 - f)
            qv = padrows(q_ref[rows, ls])
            vv = padrows(i_ref[rows, ls])
            hi, mid, lo = _split3(lf)
            b = _dot(tri, hi) + _dot(tri, mid) + _dot(tri, lo)
            st = st_s[hh]
            o_inter = _dot_nt((qv * jnp.exp(b)).astype(BF16), st.astype(BF16))
            vb = vv.astype(BF16)
            parts = []
            for si in range(L // c):
                rs = slice(si * c, (si + 1) * c)
                b_i = b[rs]
                q_i = qv[rs]
                k_i = kk[rs]
                v_i = vv[rs]
                o_i = o_inter[rs]
                if si > 0:
                    b_prev = b[si * c - 1:si * c]
                    qt = (q_i * jnp.exp(b_i - b_prev)).astype(BF16)
                    kt = (kk[:si * c] * jnp.exp(b_prev - b[:si * c])).astype(BF16)
                    o_i = o_i + _dot(_dot_nt(qt, kt).astype(BF16), vb[:si * c])
                for s in range(min(c, max(n_valid - si * c, 0))):
                    dec = jnp.exp(jnp.minimum(b_i - b_i[s:s + 1], 0.0))
                    a = jnp.sum(q_i * k_i[s:s + 1] * dec, axis=-1, keepdims=True)
                    o_i = o_i + jnp.where(rowc >= s, a, 0.0) * v_i[s:s + 1]
                parts.append(o_i)
            o = jnp.concatenate(parts, axis=0) if len(parts) > 1 else parts[0]
            o = o[:n_valid]
            o = o * lax.rsqrt(jnp.mean(o * o, axis=-1, keepdims=True) + EPS) * gn
            gv = g_ref[rows, ls]
            o_ref[rows, ls] = o * (gv * _sigmoid(gv))
            b_last = b[L - 1:L]
            kdec = (kk * jnp.exp(b_last - b)).astype(BF16)
            st_s[hh] = st * jnp.exp(b_last) + _dot_tn(vb, kdec)

    if pad:
        chunk(slice(0, tb), tb)
    else:
        def body(ch, _):
            chunk(pl.ds(pl.multiple_of(ch * L, L), L), L)
            return 0

        lax.fori_loop(0, tb // L, body, 0)

    for hh in range(HGRN_HEADS):
        s_out_ref[0, hh] = st_s[hh].T


def _hgrn(qh, fh, ih, gh, lb, s0, gnorm, *, bsz, t, layer):
    n = bsz * t
    tb = min(ROW_TILE, t)
    nt = t // tb
    row_spec = pl.BlockSpec((tb, SLAB), lambda b, i: (b * nt + i, 0))
    st_spec = pl.BlockSpec((1, HGRN_HEADS, HGRN_DIM, HGRN_DIM), lambda b, i: (b, 0, 0, 0))
    return pl.pallas_call(
        functools.partial(_hgrn_kernel, layer=layer, t_valid=t),
        grid=(bsz, nt),
        in_specs=[row_spec] * 4 + [_const(lb.shape), st_spec, _const((1, HGRN_DIM))],
        out_specs=[row_spec, st_spec],
        out_shape=[jax.ShapeDtypeStruct((n, SLAB), F32),
                   jax.ShapeDtypeStruct((bsz, HGRN_HEADS, HGRN_DIM, HGRN_DIM), F32)],
        scratch_shapes=[pltpu.VMEM((HGRN_HEADS, HGRN_DIM, HGRN_DIM), F32)],
        name="hgrn2")(qh, fh, ih, gh, lb, s0, gnorm.reshape(1, HGRN_DIM))


def _tail_kernel(*refs, chain):
    if chain:
        (h_ref, ya_ref, yb_ref, p_ref, wo_ref, gmp_ref, gfp_ref, wug_ref, wuv_ref, wcv_ref, wdn_ref,
         gfo_ref, wpp_ref, wpg_ref, gpl_ref, ho_ref, ffn_ref, acc_s, car_s) = refs
    else:
        (h_ref, ya_ref, yb_ref, p_ref, p1_ref, p2_ref, wo_ref, gmp_ref, gfp_ref, wug_ref, wuv_ref,
         wcv_ref, wdn_ref, gfo_ref, wpp_ref, wpg_ref, gpl_ref, ho_ref, ffn_ref, acc_s) = refs
    tm = h_ref.shape[0]
    nch = wug_ref.shape[0]
    cw = wug_ref.shape[2]
    mix = _dot(ya_ref[...].astype(BF16), wo_ref[0]) + _dot(yb_ref[...].astype(BF16), wo_ref[1])
    h1 = h_ref[...] + _rms(mix, gmp_ref[...])
    a = _rms(h1, gfp_ref[...]).astype(BF16)
    acc_s[...] = jnp.zeros(acc_s.shape, F32)
    rowi = lax.broadcasted_iota(jnp.int32, (tm, cw), 0)

    if chain:
        @pl.when(pl.program_id(1) == 0)
        def _():
            car_s[...] = jnp.zeros(car_s.shape, F32)

    def conv(u, prev, w):
        r1 = pltpu.roll(u, 1, 0)
        r2 = pltpu.roll(u, 2, 0)
        if chain:
            last = prev[SUBLANES - 1:SUBLANES]
            last2 = prev[SUBLANES - 2:SUBLANES - 1]
            p1 = jnp.where(rowi == 0, last, r1)
            p2 = jnp.where(rowi == 0, last2, jnp.where(rowi == 1, last, r2))
        else:
            rm = rowi % SUBLANES
            p1 = jnp.where(rm == 0, prev[0], r1)
            p2 = jnp.where(rm < 2, prev[1], r2)
        return w[3:4] + w[0:1] * p2 + w[1:2] * p1 + w[2:3] * u

    def chunk(c, _):
        ug = _dot(a, wug_ref[c])
        uv = _dot(a, wuv_ref[c])
        w = wcv_ref[c]
        if chain:
            cg = conv(ug, car_s[c], w[0:4])
            cv = conv(uv, car_s[nch + c], w[4:8])
            car_s[c] = ug[tm - SUBLANES:tm]
            car_s[nch + c] = uv[tm - SUBLANES:tm]
            ffn_ref[0, c] = ug[tm - SUBLANES:tm]
            ffn_ref[0, nch + c] = uv[tm - SUBLANES:tm]
        else:
            cg = conv(ug, (p1_ref[c], p2_ref[c]), w[0:4])
            cv = conv(uv, (p1_ref[nch + c], p2_ref[nch + c]), w[4:8])
            ffn_ref[c] = ug
            ffn_ref[nch + c] = uv
        act = (_gelu(cg) * cv).astype(BF16)
        acc_s[...] += _dot(act, wdn_ref[c])
        return 0

    lax.fori_loop(0, nch, chunk, 0)
    h2 = h1 + _rms(acc_s[...], gfo_ref[...])
    ple = _dot(p_ref[...].astype(BF16), wpp_ref[...]) * _sigmoid(_dot(h2.astype(BF16), wpg_ref[...]))
    ho_ref[...] = h2 + _rms(ple, gpl_ref[...])


def _tail(h, ya, yb, p, ffn0, w, *, bsz, t):
    n, d = h.shape
    nch = N_FF_CHUNKS
    cw = FF_CHUNK
    chain = t > SUBLANES
    consts = [w['wo'], w['gmp'], w['gfp'], w['wug'], w['wuv'], w['wcv'], w['wdn'], w['gfo'], w['wpp'],
              w['wpg'], w['gpl']]
    const_specs = [_const(x.shape) for x in consts]
    if chain:
        tm = min(ROW_TILE, t)
        nt = t // tm
        grid = (bsz, nt)
        rows = lambda width: pl.BlockSpec((tm, width), lambda b, i: (b * nt + i, 0))
        ffn_spec = pl.BlockSpec((1, 2 * nch, SUBLANES, cw), lambda b, i: (b, 0, 0, 0))
        ffn_sds = jax.ShapeDtypeStruct((bsz, 2 * nch, SUBLANES, cw), F32)
        extra, extra_specs = [], []
        scratch = [pltpu.VMEM((tm, d), F32), pltpu.VMEM((2 * nch, SUBLANES, cw), F32)]
    else:
        assert t == SUBLANES
        tm = n
        grid = (1, 1)
        rows = lambda width: pl.BlockSpec((tm, width), lambda b, i: (0, 0))
        ffn_spec = _full((2 * nch, tm, cw))
        ffn_sds = jax.ShapeDtypeStruct((2 * nch, tm, cw), F32)
        buf = ffn0.astype(F32).reshape(bsz, 2, 2 * nch, cw).transpose(2, 0, 1, 3)
        zero = jnp.zeros((2 * nch, bsz, SUBLANES - 2, cw), F32)
        p1 = jnp.concatenate([buf[:, :, 1:2], zero, zero[:, :, :1]], axis=2).reshape(2 * nch, tm, cw)
        p2 = jnp.concatenate([buf, zero], axis=2).reshape(2 * nch, tm, cw)
        extra = [p1, p2]
        extra_specs = [_full(p1.shape), _full(p2.shape)]
        scratch = [pltpu.VMEM((tm, d), F32)]
    ho, ffn = pl.pallas_call(
        functools.partial(_tail_kernel, chain=chain),
        grid=grid,
        in_specs=[rows(d), rows(SLAB), rows(SLAB), rows(p.shape[1])] + extra_specs + const_specs,
        out_specs=[rows(d), ffn_spec],
        out_shape=[jax.ShapeDtypeStruct((n, d), F32), ffn_sds],
        scratch_shapes=scratch,
        name="layer_tail")(h, ya, yb, p, *extra, *consts)
    if chain:
        st = ffn[:, :, SUBLANES - 2:, :]
    else:
        st = ffn.reshape(2 * nch, bsz, SUBLANES, cw)[:, :, SUBLANES - 2:, :].transpose(1, 0, 2, 3)
    return ho, st.transpose(0, 2, 1, 3).reshape(bsz, 2, 2 * nch * cw)


def _tail_weights(i, w_out, g_mix_post, g_ffn_pre, g_ffn_post, w_ffn_up, w_ffn_conv, b_ffn_conv,
                  w_ffn_down, w_ple_proj, w_ple_gate, g_ple):
    d = w_out.shape[1]
    nch, cw = N_FF_CHUNKS, FF_CHUNK
    up = w_ffn_up[i].astype(BF16).reshape(d, 2, nch, cw).transpose(1, 2, 0, 3)
    taps = jnp.concatenate([w_ffn_conv[i].astype(F32), b_ffn_conv[i].astype(F32)[None]], axis=0)
    wcv = taps.reshape(4, 2, nch, cw).transpose(2, 1, 0, 3).reshape(nch, 8, cw)
    row = lambda g: g.astype(F32).reshape(1, d)
    return dict(
        wo=w_out.astype(BF16).reshape(2, SLAB, d), gmp=row(g_mix_post[i]), gfp=row(g_ffn_pre[i]),
        wug=up[0], wuv=up[1], wcv=wcv, wdn=w_ffn_down[i].astype(BF16).reshape(nch, cw, d),
        gfo=row(g_ffn_post[i]), wpp=w_ple_proj[i].astype(BF16), wpg=w_ple_gate[i].astype(BF16),
        gpl=row(g_ple[i]))


def kernel(x_prompt, x_sample, p_prompt, p_sample, cache_moba_k, cache_moba_v, state_s5_re, state_s5_im,
           state_hgrn, cache_sb_k, cache_sb_v, state_ffn, page_table,
           g_mix_pre, g_mix_post, g_ffn_pre, g_ffn_post, w_ffn_up, w_ffn_conv, b_ffn_conv, w_ffn_down,
           w_ple_proj, w_ple_gate, g_ple,
           w_in_a, w_out_a, s5_a_re, s5_a_im, s5_log_dt, s5_b_re, s5_b_im, s5_c_re, s5_c_im, s5_d,
           s5_w_glu, s5_b_glu,
           w_in_c, w_out_c, hgrn_lb, g_hgrn_norm):
    depth = g_mix_pre.shape[0]
    d_model = x_prompt.shape[-1]
    n_pages = page_table.shape[1]
    past_len = n_pages * cache_moba_k.shape[2]
    page_table = page_table.astype(jnp.int32)

    layers = []
    for i in range(depth):
        j = i // 2
        lw = {}
        if i % 2 == 0:
            lw['w_in'] = w_in_a[j].astype(BF16)
            lw['disc'] = _s5_discretize(s5_a_re[j], s5_a_im[j], s5_log_dt[j])
            lw['wb'], lw['wc'] = _s5_weights(s5_b_re[j], s5_b_im[j], s5_c_re[j], s5_c_im[j])
            lw['d'] = s5_d[j].astype(F32).reshape(S5_WIDTH)
            lw['wglu'] = s5_w_glu[j].astype(BF16)
            lw['bglu'] = s5_b_glu[j].astype(F32)
            w_out = w_out_a[j]
        else:
            lw['w_in'] = w_in_c[j].astype(BF16)
            w_out = w_out_c[j]
        lw['tail'] = _tail_weights(i, w_out, g_mix_post, g_ffn_pre, g_ffn_post, w_ffn_up, w_ffn_conv,
                                   b_ffn_conv, w_ffn_down, w_ple_proj, w_ple_gate, g_ple)
        layers.append(lw)

    def pool2d(pool, j):
        return pool[j].reshape(pool.shape[1], pool.shape[2], ATT_WIDTH)

    def run(x, p, q_start, s5_re0, s5_im0, hgrn0, ffn0, has_past):
        bsz, t, _ = x.shape
        n = bsz * t
        h = x.astype(F32).reshape(n, d_model)
        tables = _rope_tables(t, q_start)
        if t < ROW_TILE:
            tables = tuple(jnp.tile(tb, (n // t, 1)) for tb in tables)
        mk, mv, sr, si, hs, sk, sv, fb = [], [], [], [], [], [], [], []
        for i in range(depth):
            j = i // 2
            lw = layers[i]
            g_pre = g_mix_pre[i].astype(F32)
            if i % 2 == 0:
                u, q, k, v = _inproj(h, g_pre, lw['w_in'], tables, rope_slabs=(1, 2))
                y_a, hr, hi = _s5(u, s5_re0[j].astype(F32).reshape(bsz, 1, S5_LANES),
                                  s5_im0[j].astype(F32).reshape(bsz, 1, S5_LANES), lw['disc'], lw['wb'],
                                  lw['wc'], lw['d'], lw['wglu'], lw['bglu'], bsz=bsz, t=t)
                if has_past:
                    y_b = _moba_sample(q, k, v, pool2d(cache_moba_k, j), pool2d(cache_moba_v, j),
                                       page_table, bsz=bsz, t=t)
                else:
                    y_b = _moba_prompt(q, k, v, bsz=bsz, t=t)
                mk.append(k.reshape(bsz, t, HEADS, HEAD_DIM))
                mv.append(v.reshape(bsz, t, HEADS, HEAD_DIM))
                sr.append(hr.reshape(bsz, S5_GROUPS, S5_STATE))
                si.append(hi.reshape(bsz, S5_GROUPS, S5_STATE))
            else:
                qh, fh, ih, gh, q, k, v = _inproj(h, g_pre, lw['w_in'], tables, rope_slabs=())
                y_a, s_fin = _hgrn(qh, fh, ih, gh, hgrn_lb.astype(F32), hgrn0[j].astype(F32),
                                   g_hgrn_norm[j].astype(F32), bsz=bsz, t=t, layer=i)
                if has_past:
                    y_b = _sb_sample(q, k, v, pool2d(cache_sb_k, j), pool2d(cache_sb_v, j), page_table,
                                     bsz=bsz, t=t)
                else:
                    y_b = _sb_prompt(q, k, v, bsz=bsz, t=t)
                hs.append(s_fin)
                sk.append(k.reshape(bsz, t, HEADS, HEAD_DIM))
                sv.append(v.reshape(bsz, t, HEADS, HEAD_DIM))
            h, buf = _tail(h, y_a, y_b, p[i].astype(F32).reshape(n, p.shape[-1]),
                           None if ffn0 is None else ffn0[i], lw['tail'], bsz=bsz, t=t)
            fb.append(buf)
        return (h.reshape(bsz, t, d_model), jnp.stack(mk), jnp.stack(mv), jnp.stack(sr), jnp.stack(si),
                jnp.stack(hs), jnp.stack(sk), jnp.stack(sv), jnp.stack(fb))

    bp = x_prompt.shape[0]
    n_a = (depth + 1) // 2
    n_c = depth // 2
    outs_p = run(x_prompt, p_prompt, 0,
                 jnp.zeros((n_a, bp, S5_GROUPS, S5_STATE), F32), jnp.zeros((n_a, bp, S5_GROUPS, S5_STATE), F32),
                 jnp.zeros((n_c, bp, HGRN_HEADS, HGRN_DIM, HGRN_DIM), F32), None, False)
    outs_s = run(x_sample, p_sample, past_len, state_s5_re, state_s5_im, state_hgrn, state_ffn, True)
    return (outs_p[0], outs_s[0]) + tuple(outs_p[1:]) + tuple(outs_s[1:])
```

```python
import functools
import math

import jax
import jax.numpy as jnp
from jax import lax
from jax.experimental import pallas as pl
from jax.experimental.pallas import tpu as pltpu

F32 = jnp.float32
BF16 = jnp.bfloat16

S5_WIDTH = 512
S5_GROUPS = 32
S5_GROUP = 16
S5_STATE = 64
S5_LANES = S5_GROUPS * S5_STATE
HEADS = 8
HEAD_DIM = 64
ATT_WIDTH = HEADS * HEAD_DIM
ROPE_DIM = 16
ROPE_THETA = 500000.0
MOBA_BLOCK = 256
MOBA_TOPK = 3
HGRN_HEADS = 4
HGRN_DIM = 128
HGRN_CHUNK = 64
HGRN_SUB = 16
D_FF = 2816
FF_CHUNK = 256
N_FF_CHUNKS = D_FF // FF_CHUNK
PAGE = 128
EPS = 1e-6
NEG = -1e30
SLAB = 512
LANES = 128
SUBLANES = 8
ROW_TILE = 256
MATMUL_TILE = 512
SB_BLOCK = 256
PAGES_PER_STEP = 8
EXP_UNDERFLOW = -104.0


def _dot(a, b):
    return jnp.dot(a, b, preferred_element_type=F32)


def _dot_nt(a, b):
    return lax.dot_general(a, b, (((1,), (1,)), ((), ())), preferred_element_type=F32)


def _dot_tn(a, b):
    return lax.dot_general(a, b, (((0,), (0,)), ((), ())), preferred_element_type=F32)


def _split3(x):
    hi = x.astype(BF16)
    r1 = x - hi.astype(F32)
    mid = r1.astype(BF16)
    lo = (r1 - mid.astype(F32)).astype(BF16)
    return hi, mid, lo


def _rms(x, g):
    return x * lax.rsqrt(jnp.mean(x * x, axis=-1, keepdims=True) + EPS) * g


def _sigmoid(x):
    return 1.0 / (1.0 + jnp.exp(-x))


def _gelu(x):
    return 0.5 * x * (1.0 + jnp.tanh(0.7978845608028654 * (x + 0.044715 * (x * x * x))))


def _full(shape):
    nd = len(shape)
    return pl.BlockSpec(shape, lambda *_: (0,) * nd)


def _const(shape):
    nd = len(shape)
    return pl.BlockSpec(shape, lambda *_: (0,) * nd, pipeline_mode=pl.Buffered(1))


def _rope_table_kernel(c_ref, s1_ref, s2_ref, *, q_start):
    rows, lanes = c_ref.shape
    i = pl.program_id(0)
    lane = lax.broadcasted_iota(jnp.int32, (rows, lanes), 1)
    pos = lax.broadcasted_iota(jnp.int32, (rows, lanes), 0) + i * rows + q_start
    d = lane % HEAD_DIM
    half = ROPE_DIM // 2
    inv = jnp.exp((d % half).astype(F32) * (-math.log(ROPE_THETA) / half))
    ang = pos.astype(F32) * inv
    cs = jnp.cos(ang)
    sn = jnp.sin(ang)
    c_ref[...] = jnp.where(d < ROPE_DIM, cs, 1.0)
    s1_ref[...] = jnp.where(d < half, -sn, 0.0)
    s2_ref[...] = jnp.where((d >= half) & (d < ROPE_DIM), sn, 0.0)


def _rope_tables(t, q_start):
    rows = min(t, 512)
    sds = jax.ShapeDtypeStruct((t, LANES), F32)
    spec = pl.BlockSpec((rows, LANES), lambda i: (i, 0))
    return pl.pallas_call(
        functools.partial(_rope_table_kernel, q_start=q_start),
        grid=(t // rows,), out_shape=(sds, sds, sds), out_specs=(spec, spec, spec),
        name="rope_tables")()


def _inproj_kernel(h_ref, g_ref, w_ref, c_ref, s1_ref, s2_ref, *out_refs, rope_slabs):
    a = _rms(h_ref[...], g_ref[...]).astype(BF16)
    half = ROPE_DIM // 2
    for s, o_ref in enumerate(out_refs):
        z = _dot(a, w_ref[:, s * SLAB:(s + 1) * SLAB])
        if s in rope_slabs:
            c = c_ref[...]
            s1 = s1_ref[...]
            s2 = s2_ref[...]
            for j in range(SLAB // LANES):
                x = z[:, j * LANES:(j + 1) * LANES]
                o_ref[:, j * LANES:(j + 1) * LANES] = (
                    x * c + pltpu.roll(x, LANES - half, 1) * s1 + pltpu.roll(x, half, 1) * s2)
        else:
            o_ref[...] = z


def _inproj(h, g, w, tables, rope_slabs):
    n, d = h.shape
    n_slabs = w.shape[1] // SLAB
    tm = min(MATMUL_TILE, n)
    t_tiles = tables[0].shape[0] // tm
    tab_spec = pl.BlockSpec((tm, LANES), lambda i: (i % t_tiles, 0))
    out_spec = pl.BlockSpec((tm, SLAB), lambda i: (i, 0))
    return pl.pallas_call(
        functools.partial(_inproj_kernel, rope_slabs=rope_slabs),
        grid=(n // tm,),
        in_specs=[pl.BlockSpec((tm, d), lambda i: (i, 0)), _const((1, d)), _const(w.shape),
                  tab_spec, tab_spec, tab_spec],
        out_specs=[out_spec] * n_slabs,
        out_shape=[jax.ShapeDtypeStruct((n, SLAB), F32)] * n_slabs,
        name="inproj")(h, g.reshape(1, d), w, *tables)


def _s5_disc_kernel(ar_ref, ai_ref, ldt_ref, pre_ref, pim_ref, cc_ref):
    ar = ar_ref[...]
    ai = ai_ref[...]
    dt = jnp.exp(ldt_ref[...])
    row = lax.broadcasted_iota(jnp.int32, (2 * SUBLANES, 1), 0)
    n = jnp.where(row < SUBLANES, row + 1,
                  jnp.where(row == SUBLANES + 1, 2, jnp.where(row == SUBLANES + 2, 4, 1))).astype(F32)
    mag = jnp.exp(ar * dt * n)
    ang = ai * dt * n
    pre = mag * jnp.cos(ang)
    pim = mag * jnp.sin(ang)
    pre_ref[...] = pre
    pim_ref[...] = pim
    abr = pre[0:1]
    abi = pim[0:1]
    den = ar * ar + ai * ai
    nr = abr - 1.0
    cc_ref[0:1, :] = (nr * ar + abi * ai) / den
    cc_ref[1:2, :] = (abi * ar - nr * ai) / den


def _s5_discretize(a_re, a_im, log_dt):
    ar = a_re.reshape(1, S5_LANES)
    ai = a_im.reshape(1, S5_LANES)
    ldt = jnp.repeat(log_dt, S5_STATE).reshape(1, S5_LANES)
    return pl.pallas_call(
        _s5_disc_kernel,
        out_shape=(jax.ShapeDtypeStruct((2 * SUBLANES, S5_LANES), F32),
                   jax.ShapeDtypeStruct((2 * SUBLANES, S5_LANES), F32),
                   jax.ShapeDtypeStruct((2, S5_LANES), F32)),
        name="s5_discretize")(ar, ai, ldt)


SCAN_LANES = 512


def _s5_kernel(u_ref, h0r_ref, h0i_ref, pre_ref, pim_ref, cc_ref, wb_ref, wc_ref, d_ref,
               wglu_ref, bglu_ref, y_ref, hr_out_ref, hi_out_ref, xr_s, xi_s, car_s, *, chain):
    tm = u_ref.shape[0]
    nb = tm // SUBLANES
    u = u_ref[...]
    ub = u.astype(BF16)
    half_in = S5_WIDTH // 2
    half_st = S5_LANES // 2
    for hf in range(2):
        bu = _dot(ub[:, hf * half_in:(hf + 1) * half_in], wb_ref[hf])
        bur = bu[:, :half_st]
        bui = bu[:, half_st:]
        ls = slice(hf * half_st, (hf + 1) * half_st)
        cr = cc_ref[0:1, ls]
        ci = cc_ref[1:2, ls]
        xr_s[:, ls] = cr * bur - ci * bui
        xi_s[:, ls] = cr * bui + ci * bur

    if chain:
        t = pl.program_id(1)

        @pl.when(t == 0)
        def _():
            car_s[0:1, :] = h0r_ref[0]
            car_s[1:2, :] = h0i_ref[0]

    rowi = lax.broadcasted_iota(jnp.int32, (SUBLANES, SCAN_LANES), 0)
    for lc in range(S5_LANES // SCAN_LANES):
        ls = slice(lc * SCAN_LANES, (lc + 1) * SCAN_LANES)
        pre = pre_ref[:, ls]
        pim = pim_ref[:, ls]

        def body(i, carry, ls=ls, pre=pre, pim=pim):
            r0 = pl.multiple_of(i * SUBLANES, SUBLANES)
            hr = xr_s[pl.ds(r0, SUBLANES), ls]
            hi = xi_s[pl.ds(r0, SUBLANES), ls]
            for k, prow in ((1, SUBLANES), (2, SUBLANES + 1), (4, SUBLANES + 2)):
                ar = pre[prow:prow + 1]
                ai = pim[prow:prow + 1]
                sr = jnp.where(rowi >= k, pltpu.roll(hr, k, 0), 0.0)
                si = jnp.where(rowi >= k, pltpu.roll(hi, k, 0), 0.0)
                hr, hi = hr + ar * sr - ai * si, hi + ar * si + ai * sr
            if chain:
                cr_, ci_ = carry
            else:
                cr_ = h0r_ref[i, :, ls]
                ci_ = h0i_ref[i, :, ls]
            pr8 = pre[0:SUBLANES]
            pi8 = pim[0:SUBLANES]
            hr, hi = hr + pr8 * cr_ - pi8 * ci_, hi + pr8 * ci_ + pi8 * cr_
            xr_s[pl.ds(r0, SUBLANES), ls] = hr
            xi_s[pl.ds(r0, SUBLANES), ls] = hi
            last_r = hr[SUBLANES - 1:SUBLANES]
            last_i = hi[SUBLANES - 1:SUBLANES]
            if chain:
                return last_r, last_i
            hr_out_ref[i, :, ls] = last_r
            hi_out_ref[i, :, ls] = last_i
            return carry

        if chain:
            cr_, ci_ = lax.fori_loop(0, nb, body, (car_s[0:1, ls], car_s[1:2, ls]))
            car_s[0:1, ls] = cr_
            car_s[1:2, ls] = ci_
        else:
            lax.fori_loop(0, nb, body, 0)

    if chain:
        hr_out_ref[0] = car_s[0:1, :]
        hi_out_ref[0] = car_s[1:2, :]

    ys = []
    for hf in range(2):
        ls = slice(hf * half_st, (hf + 1) * half_st)
        hcat = jnp.concatenate([xr_s[:, ls], xi_s[:, ls]], axis=1).astype(BF16)
        ys.append(_dot(hcat, wc_ref[hf]))
    y = jnp.concatenate(ys, axis=1) + d_ref[...] * u
    y = _gelu(y)
    y_ref[...] = y * _sigmoid(_dot(y.astype(BF16), wglu_ref[...]) + bglu_ref[...])


def _s5_weights(b_re, b_im, c_re, c_im):
    gh = S5_GROUPS // 2
    eye = jnp.eye(gh, dtype=F32)

    def bmat(b):
        bt = b.astype(F32).transpose(0, 2, 1).reshape(2, gh, S5_GROUP, S5_STATE)
        return jnp.einsum('xghp,gk->xghkp', bt, eye).reshape(2, gh * S5_GROUP, gh * S5_STATE)

    def cmat(c):
        ct = c.astype(F32).transpose(0, 2, 1).reshape(2, gh, S5_STATE, S5_GROUP)
        return jnp.einsum('xgph,gk->xgpkh', ct, eye).reshape(2, gh * S5_STATE, gh * S5_GROUP)

    wb = jnp.concatenate([bmat(b_re), bmat(b_im)], axis=2).astype(BF16)
    wc = jnp.concatenate([cmat(c_re), -cmat(c_im)], axis=1).astype(BF16)
    return wb, wc


def _s5(u, h0r, h0i, disc, wb, wc, dvec, wglu, bglu, *, bsz, t):
    n = bsz * t
    pre, pim, cc = disc
    chain = t > SUBLANES
    if chain:
        tm = min(ROW_TILE, t)
        grid = (bsz, t // tm)
        row_spec = pl.BlockSpec((tm, S5_WIDTH), lambda b, i: (b * (t // tm) + i, 0))
        st_spec = pl.BlockSpec((1, 1, S5_LANES), lambda b, i: (b, 0, 0))
    else:
        assert t == SUBLANES
        tm = n
        grid = (1, 1)
        row_spec = pl.BlockSpec((tm, S5_WIDTH), lambda b, i: (0, 0))
        st_spec = pl.BlockSpec((bsz, 1, S5_LANES), lambda b, i: (0, 0, 0))
    st_sds = jax.ShapeDtypeStruct((bsz, 1, S5_LANES), F32)
    return pl.pallas_call(
        functools.partial(_s5_kernel, chain=chain),
        grid=grid,
        in_specs=[row_spec, st_spec, st_spec, _const(pre.shape), _const(pim.shape), _const(cc.shape),
                  _const(wb.shape), _const(wc.shape), _const((1, S5_WIDTH)), _const(wglu.shape),
                  _const((1, S5_WIDTH))],
        out_specs=[row_spec, st_spec, st_spec],
        out_shape=[jax.ShapeDtypeStruct((n, S5_WIDTH), F32), st_sds, st_sds],
        scratch_shapes=[pltpu.VMEM((tm, S5_LANES), F32), pltpu.VMEM((tm, S5_LANES), F32),
                        pltpu.VMEM((2, S5_LANES), F32)],
        name="s5_mixer")(u, h0r, h0i, pre, pim, cc, wb, wc, dvec.reshape(1, S5_WIDTH), wglu,
                         bglu.reshape(1, S5_WIDTH))


def _top_blocks(gate, n_past, col):
    g = jnp.where(col < n_past, gate, NEG)
    sel = jnp.zeros(gate.shape, F32)
    colf = col.astype(F32)
    for _ in range(MOBA_TOPK):
        m = jnp.max(g, axis=-1, keepdims=True)
        first = jnp.min(jnp.where(g == m, colf, float(LANES)), axis=-1, keepdims=True)
        pick = colf == first
        sel = jnp.where(pick & (m > 0.5 * NEG), 1.0, sel)
        g = jnp.where(pick, -3e38, g)
    return sel


def _top_blocks_t(gate, n_past):
    rowf = lax.broadcasted_iota(jnp.int32, gate.shape, 0).astype(F32)
    g = jnp.where(rowf < n_past, gate, NEG)
    sel = jnp.zeros(gate.shape, F32)
    for _ in range(MOBA_TOPK):
        m = jnp.max(g, axis=0, keepdims=True)
        first = jnp.min(jnp.where(g == m, rowf, float(LANES)), axis=0, keepdims=True)
        pick = rowf == first
        sel = jnp.where(pick & (m > 0.5 * NEG), 1.0, sel)
        g = jnp.where(pick, -3e38, g)
    return sel


def _moba_prompt_kernel(q_ref, k_ref, v_ref, o_ref, kb_s, vt_s, kmean_s, sel_s, *, nblk):
    bs = MOBA_BLOCK
    i = pl.program_id(2)

    @pl.when(i == 0)
    def _():
        kmean_s[...] = jnp.zeros(kmean_s.shape, F32)
        for j in range(nblk):
            kj = k_ref[j * bs:(j + 1) * bs, :]
            kb_s[j * bs:(j + 1) * bs, :] = kj.astype(BF16)
            kmean_s[j:j + 1, :] = jnp.sum(kj, axis=0, keepdims=True) * (1.0 / bs)
            vt_s[j] = v_ref[j * bs:(j + 1) * bs, :].T.astype(BF16)

    lane = lax.broadcasted_iota(jnp.int32, (1, LANES), 1)
    rowd = lax.broadcasted_iota(jnp.int32, (LANES, 1), 0)
    causal = (lax.broadcasted_iota(jnp.int32, (bs, bs), 0)
              <= lax.broadcasted_iota(jnp.int32, (bs, bs), 1))
    kmb = kmean_s[...].astype(BF16)
    qf = q_ref[...]
    r0 = pl.multiple_of(i * bs, bs)
    ki = kb_s[pl.ds(r0, bs), :]
    vti = vt_s[i]
    qss = []
    carry = []
    for hd in range(2):
        lm = (lane >= HEAD_DIM * hd) & (lane < HEAD_DIM * (hd + 1))
        qh = jnp.where(lm, qf, 0.0)
        qs = (qh * HEAD_DIM ** -0.5).astype(BF16)
        sel = _top_blocks_t(_dot_nt(kmb, qh.astype(BF16)), i)
        for b in range(nblk):
            sel_s[hd, b] = sel[b:b + 1, :]
        s = jnp.where(causal, _dot_nt(ki, qs), NEG)
        m = jnp.max(s, axis=0, keepdims=True)
        p = jnp.exp(s - m)
        carry += [m, jnp.sum(p, axis=0, keepdims=True), _dot(vti, p.astype(BF16))]
        qss.append(qs)

    def kvpair(jj, carry):
        c0 = pl.multiple_of(jj * 2 * bs, 2 * bs)
        kw = kb_s[pl.ds(c0, 2 * bs), :]
        out = []
        for hd in range(2):
            m, l, acc = carry[3 * hd:3 * hd + 3]
            s = _dot_nt(kw, qss[hd])
            s0 = jnp.where(sel_s[hd, 2 * jj] > 0.5, s[:bs], NEG)
            s1 = jnp.where(sel_s[hd, 2 * jj + 1] > 0.5, s[bs:], NEG)
            m_new = jnp.maximum(m, jnp.maximum(jnp.max(s0, axis=0, keepdims=True),
                                               jnp.max(s1, axis=0, keepdims=True)))
            alpha = jnp.exp(m - m_new)
            p0 = jnp.exp(s0 - m_new)
            p1 = jnp.exp(s1 - m_new)
            l = alpha * l + jnp.sum(p0, axis=0, keepdims=True) + jnp.sum(p1, axis=0, keepdims=True)
            acc = (alpha * acc + _dot(vt_s[2 * jj], p0.astype(BF16))
                   + _dot(vt_s[2 * jj + 1], p1.astype(BF16)))
            out += [m_new, l, acc]
        return tuple(out)

    carry = lax.fori_loop(0, (i + 1) // 2, kvpair, tuple(carry))
    out_t = jnp.where(rowd < HEAD_DIM, carry[2] / carry[1], carry[5] / carry[4])
    o_ref[...] = out_t.T


def _moba_prompt(q, k, v, *, bsz, t):
    nblk = t // MOBA_BLOCK
    assert t % MOBA_BLOCK == 0 and nblk % 2 == 0 and nblk <= LANES
    nblk_rows = -(-nblk // SUBLANES) * SUBLANES
    n = bsz * t
    q_spec = pl.BlockSpec((MOBA_BLOCK, LANES), lambda b, hp, i: (b * nblk + i, hp))
    kv_spec = pl.BlockSpec((t, LANES), lambda b, hp, i: (b, hp))
    return pl.pallas_call(
        functools.partial(_moba_prompt_kernel, nblk=nblk),
        grid=(bsz, ATT_WIDTH // LANES, nblk),
        in_specs=[q_spec, kv_spec, kv_spec], out_specs=q_spec,
        out_shape=jax.ShapeDtypeStruct((n, ATT_WIDTH), F32),
        scratch_shapes=[pltpu.VMEM((t, LANES), BF16),
                        pltpu.VMEM((nblk, LANES, MOBA_BLOCK), BF16),
                        pltpu.VMEM((nblk_rows, LANES), F32),
                        pltpu.VMEM((2, nblk, 1, MOBA_BLOCK), F32)],
        name="moba_prompt")(q, k, v)


def _page_head(page_ref, h):
    return page_ref[pl.ds(h, PAGE, stride=HEADS), :].astype(BF16)


def _load_sample_rows(q_ref, kn_ref, vn_ref, qe_s, newk_s, newv_s, scale):
    t = q_ref.shape[0]
    newk_s[...] = jnp.zeros(newk_s.shape, F32)
    newv_s[...] = jnp.zeros(newv_s.shape, F32)
    for h in range(HEADS):
        ls = slice(h * HEAD_DIM, (h + 1) * HEAD_DIM)
        parts = [q_ref[:, ls] * scale]
        if h:
            parts.insert(0, jnp.zeros((h * t, HEAD_DIM), F32))
        if h < HEADS - 1:
            parts.append(jnp.zeros(((HEADS - 1 - h) * t, HEAD_DIM), F32))
        qe_s[h] = jnp.concatenate(parts, axis=0).astype(BF16)
        newk_s[h, 0:t, :] = kn_ref[:, ls]
        newv_s[h, 0:t, :] = vn_ref[:, ls]


def _head_scores(qe_s, get_keys):
    s = _dot_nt(qe_s[0], get_keys(0))
    for h in range(1, HEADS):
        s = s + _dot_nt(qe_s[h], get_keys(h))
    return s


def _moba_sample_kernel(pt_ref, q_ref, kn_ref, vn_ref, *rest, n_pages, t):
    pps = PAGES_PER_STEP
    k_refs = rest[:pps]
    v_refs = rest[pps:2 * pps]
    o_ref = rest[2 * pps]
    qe_s, newk_s, newv_s, s_all, gate_s, bmax_s, sel_s, m_s, l_s, acc_s = rest[2 * pps + 1:]
    ns = n_pages // pps
    s_id = pl.program_id(1)
    rows = HEADS * t
    col = lax.broadcasted_iota(jnp.int32, (rows, LANES), 1)
    scale = HEAD_DIM ** -0.5
    pages_per_block = MOBA_BLOCK // PAGE

    @pl.when(s_id == 0)
    def _():
        _load_sample_rows(q_ref, kn_ref, vn_ref, qe_s, newk_s, newv_s, 1.0)
        gate_s[...] = jnp.zeros(gate_s.shape, F32)
        bmax_s[...] = jnp.full(bmax_s.shape, NEG, F32)

    @pl.when(s_id < ns)
    def _():
        g = gate_s[...]
        bm = bmax_s[...]
        for i in range(pps):
            pg = s_id * pps + i
            s = _head_scores(qe_s, lambda h, i=i: _page_head(k_refs[i], h))
            s_all[pg] = s
            mine = col == pg // pages_per_block
            g = g + jnp.where(mine, jnp.sum(s, axis=-1, keepdims=True) * (1.0 / MOBA_BLOCK), 0.0)
            bm = jnp.where(mine, jnp.maximum(bm, jnp.max(s, axis=-1, keepdims=True)), bm)
        gate_s[...] = g
        bmax_s[...] = bm

    @pl.when(s_id == ns)
    def _():
        sel = _top_blocks(gate_s[...], n_pages // pages_per_block, col)
        sel_s[...] = sel
        sn = _head_scores(qe_s, lambda h: newk_s[h].astype(BF16)) * scale
        rowt = lax.broadcasted_iota(jnp.int32, (rows, LANES), 0) % t
        sn = jnp.where(col <= rowt, sn, NEG)
        m = jnp.maximum(jnp.max(sn, axis=-1, keepdims=True),
                        jnp.max(jnp.where(sel > 0.5, bmax_s[...] * scale, NEG), axis=-1, keepdims=True))
        pn = jnp.exp(sn - m)
        m_s[...] = jnp.broadcast_to(m, m_s.shape)
        l_s[...] = jnp.broadcast_to(jnp.sum(pn, axis=-1, keepdims=True), l_s.shape)
        pnb = pn.astype(BF16)
        for h in range(HEADS):
            acc_s[h] = _dot(pnb, newv_s[h].astype(BF16))

    @pl.when(s_id >= ns)
    def _():
        sel = sel_s[...]
        m = m_s[:, 0:1]
        l = l_s[:, 0:1]
        ps = []
        for i in range(pps):
            pg = (s_id - ns) * pps + i
            selc = jnp.sum(jnp.where(col == pg // pages_per_block, sel, 0.0), axis=-1, keepdims=True)
            p = jnp.exp(jnp.where(selc > 0.5, s_all[pg] * scale, NEG) - m)
            l = l + jnp.sum(p, axis=-1, keepdims=True)
            ps.append(p.astype(BF16))
        l_s[...] = jnp.broadcast_to(l, l_s.shape)
        for h in range(HEADS):
            acc = acc_s[h]
            for i in range(pps):
                acc = acc + _dot(ps[i], _page_head(v_refs[i], h))
            acc_s[h] = acc

    @pl.when(s_id == 2 * ns - 1)
    def _():
        for h in range(HEADS):
            o_ref[0, h] = acc_s[h, h * t:(h + 1) * t, :] / l_s[h * t:(h + 1) * t, 0:1]


def _sample_specs(page_table, n_pages, t, page_index):
    pps = PAGES_PER_STEP
    row_spec = pl.BlockSpec((t, ATT_WIDTH), lambda b, s, pt: (b, 0))
    out_spec = pl.BlockSpec((1, HEADS, t, HEAD_DIM), lambda b, s, pt: (b, 0, 0, 0))

    def page_spec(i, which):
        return pl.BlockSpec((None, PAGE * HEADS, HEAD_DIM),
                            lambda b, s, pt: (pt[b, page_index(s, i, which)], 0, 0))

    in_specs = ([row_spec] * 3 + [page_spec(i, 0) for i in range(pps)]
                + [page_spec(i, 1) for i in range(pps)])
    return in_specs, out_spec


def _heads_to_lanes(o, bsz, t):
    return o.transpose(0, 2, 1, 3).reshape(bsz * t, ATT_WIDTH)


def _moba_sample(q, k_new, v_new, k_pool, v_pool, page_table, *, bsz, t):
    n_pages = page_table.shape[1]
    pps = PAGES_PER_STEP
    assert n_pages % pps == 0 and (n_pages * PAGE) % MOBA_BLOCK == 0 and t <= SUBLANES
    assert n_pages * PAGE // MOBA_BLOCK <= LANES
    ns = n_pages // pps
    rows = HEADS * t

    def page_index(s, i, which):
        step = jnp.minimum(s, ns - 1) if which == 0 else jnp.maximum(s - ns, 0)
        return step * pps + i

    in_specs, out_spec = _sample_specs(page_table, n_pages, t, page_index)
    grid_spec = pltpu.PrefetchScalarGridSpec(
        num_scalar_prefetch=1, grid=(bsz, 2 * ns), in_specs=in_specs, out_specs=out_spec,
        scratch_shapes=[pltpu.VMEM((HEADS, rows, HEAD_DIM), BF16),
                        pltpu.VMEM((HEADS, PAGE, HEAD_DIM), F32),
                        pltpu.VMEM((HEADS, PAGE, HEAD_DIM), F32),
                        pltpu.VMEM((n_pages, rows, PAGE), F32)]
                       + [pltpu.VMEM((rows, LANES), F32)] * 5
                       + [pltpu.VMEM((HEADS, rows, HEAD_DIM), F32)])
    o = pl.pallas_call(
        functools.partial(_moba_sample_kernel, n_pages=n_pages, t=t),
        grid_spec=grid_spec,
        out_shape=jax.ShapeDtypeStruct((bsz, HEADS, t, HEAD_DIM), F32),
        name="moba_sample")(page_table, q, k_new, v_new, *([k_pool] * pps), *([v_pool] * pps))
    return _heads_to_lanes(o, bsz, t)


def _sb_weights(z, r, tri, strict):
    n = z.shape[0]
    lg = jnp.log(1.0 + jnp.exp(-jnp.abs(z)))
    log_beta = jnp.minimum(z, 0.0) - lg
    log_keep = jnp.minimum(-z, 0.0) - lg
    if strict is not None:
        log_beta = jnp.where(strict, log_beta, NEG)
        log_keep = jnp.where(strict, log_keep, 0.0)
    st = _dot(jnp.concatenate(_split3(log_keep), axis=0), tri)
    later = st[:n] + st[n:2 * n] + st[2 * n:]
    w = jnp.exp(log_beta + later + r)
    return w.astype(BF16), r + jnp.sum(log_keep, axis=-1, keepdims=True)


def _suffix_matrix(n):
    return (lax.broadcasted_iota(jnp.int32, (n, n), 0)
            > lax.broadcasted_iota(jnp.int32, (n, n), 1)).astype(BF16)


def _sb_prompt_kernel(q_ref, k_ref, v_ref, o_ref, kb_s, vb_s):
    bs = SB_BLOCK
    i = pl.program_id(2)

    @pl.when(i == 0)
    def _():
        for j in range(k_ref.shape[0] // bs):
            kb_s[j * bs:(j + 1) * bs, :] = k_ref[j * bs:(j + 1) * bs, :].astype(BF16)
            vb_s[j * bs:(j + 1) * bs, :] = v_ref[j * bs:(j + 1) * bs, :].astype(BF16)

    lane = lax.broadcasted_iota(jnp.int32, (1, LANES), 1)
    tri = _suffix_matrix(bs)
    strict = (lax.broadcasted_iota(jnp.int32, (bs, bs), 1)
              < lax.broadcasted_iota(jnp.int32, (bs, bs), 0))
    qf = q_ref[...]
    qss = []
    for hd in range(2):
        lm = (lane >= HEAD_DIM * hd) & (lane < HEAD_DIM * (hd + 1))
        qss.append((jnp.where(lm, qf, 0.0) * HEAD_DIM ** -0.5).astype(BF16))

    def block(j, rs, mask):
        c0 = pl.multiple_of(j * bs, bs)
        kj = kb_s[pl.ds(c0, bs), :]
        vj = vb_s[pl.ds(c0, bs), :]
        out = []
        for hd in range(2):
            w, r = _sb_weights(_dot_nt(qss[hd], kj), rs[hd], tri, mask)
            out += [_dot(w, vj), r]
        return out

    zero = jnp.zeros((bs, 1), F32)
    a0, r0, a1, r1 = block(i, (zero, zero), strict)

    def cond(c):
        j, r0, r1 = c[:3]
        return (j >= 0) & (jnp.maximum(jnp.max(r0), jnp.max(r1)) > EXP_UNDERFLOW)

    def body(c):
        j, r0, r1, a0, a1 = c
        d0, r0, d1, r1 = block(j, (r0, r1), None)
        return j - 1, r0, r1, a0 + d0, a1 + d1

    _, _, _, a0, a1 = lax.while_loop(cond, body, (i - 1, r0, r1, a0, a1))
    o_ref[...] = jnp.where(lane < HEAD_DIM, a0, a1)


def _sb_prompt(q, k, v, *, bsz, t):
    assert t % SB_BLOCK == 0
    n = bsz * t
    nq = t // SB_BLOCK
    q_spec = pl.BlockSpec((SB_BLOCK, LANES), lambda b, hp, i: (b * nq + i, hp))
    kv_spec = pl.BlockSpec((t, LANES), lambda b, hp, i: (b, hp))
    return pl.pallas_call(
        _sb_prompt_kernel,
        grid=(bsz, ATT_WIDTH // LANES, nq),
        in_specs=[q_spec, kv_spec, kv_spec], out_specs=q_spec,
        out_shape=jax.ShapeDtypeStruct((n, ATT_WIDTH), F32),
        scratch_shapes=[pltpu.VMEM((t, LANES), BF16), pltpu.VMEM((t, LANES), BF16)],
        name="sb_prompt")(q, k, v)


def _sb_sample_kernel(pt_ref, q_ref, kn_ref, vn_ref, *rest, n_pages, t):
    pps = PAGES_PER_STEP
    k_refs = rest[:pps]
    v_refs = rest[pps:2 * pps]
    o_ref = rest[2 * pps]
    qe_s, newk_s, newv_s, r_s, acc_s = rest[2 * pps + 1:]
    s_id = pl.program_id(1)
    rows = HEADS * t
    tri = _suffix_matrix(PAGE)

    @pl.when(s_id == 0)
    def _():
        _load_sample_rows(q_ref, kn_ref, vn_ref, qe_s, newk_s, newv_s, HEAD_DIM ** -0.5)
        z = _head_scores(qe_s, lambda h: newk_s[h].astype(BF16))
        col = lax.broadcasted_iota(jnp.int32, (rows, PAGE), 1)
        rowt = lax.broadcasted_iota(jnp.int32, (rows, PAGE), 0) % t
        w, r = _sb_weights(z, jnp.zeros((rows, 1), F32), tri, col < rowt)
        for h in range(HEADS):
            acc_s[h] = _dot(w, newv_s[h].astype(BF16))
        r_s[...] = jnp.broadcast_to(r, r_s.shape)

    for i in range(pps):
        @pl.when(jnp.max(r_s[...]) > EXP_UNDERFLOW)
        def _(i=i):
            z = _head_scores(qe_s, lambda h: _page_head(k_refs[i], h))
            w, r = _sb_weights(z, r_s[:, 0:1], tri, None)
            for h in range(HEADS):
                acc_s[h] = acc_s[h] + _dot(w, _page_head(v_refs[i], h))
            r_s[...] = jnp.broadcast_to(r, r_s.shape)

    @pl.when(s_id == pl.num_programs(1) - 1)
    def _():
        for h in range(HEADS):
            o_ref[0, h] = acc_s[h, h * t:(h + 1) * t, :]


def _sb_sample(q, k_new, v_new, k_pool, v_pool, page_table, *, bsz, t):
    n_pages = page_table.shape[1]
    pps = PAGES_PER_STEP
    assert n_pages % pps == 0 and t <= SUBLANES
    rows = HEADS * t
    in_specs, out_spec = _sample_specs(page_table, n_pages, t,
                                       lambda s, i, which: n_pages - 1 - (s * pps + i))
    grid_spec = pltpu.PrefetchScalarGridSpec(
        num_scalar_prefetch=1, grid=(bsz, n_pages // pps), in_specs=in_specs, out_specs=out_spec,
        scratch_shapes=[pltpu.VMEM((HEADS, rows, HEAD_DIM), BF16),
                        pltpu.VMEM((HEADS, PAGE, HEAD_DIM), F32),
                        pltpu.VMEM((HEADS, PAGE, HEAD_DIM), F32),
                        pltpu.VMEM((rows, LANES), F32),
                        pltpu.VMEM((HEADS, rows, HEAD_DIM), F32)])
    o = pl.pallas_call(
        functools.partial(_sb_sample_kernel, n_pages=n_pages, t=t),
        grid_spec=grid_spec,
        out_shape=jax.ShapeDtypeStruct((bsz, HEADS, t, HEAD_DIM), F32),
        name="sb_sample")(page_table, q, k_new, v_new, *([k_pool] * pps), *([v_pool] * pps))
    return _heads_to_lanes(o, bsz, t)


def _hgrn_kernel(q_ref, f_ref, i_ref, g_ref, lb_ref, s0_ref, gn_ref, o_ref, s_out_ref, st_s,
                 *, layer):
    tb = q_ref.shape[0]
    t = pl.program_id(1)

    @pl.when(t == 0)
    def _():
        for hh in range(HGRN_HEADS):
            st_s[hh] = s0_ref[0, hh].T

    lbp = lb_ref[...]
    e = jnp.exp(lbp - jnp.max(lbp, axis=0, keepdims=True))
    soft = e / jnp.sum(e, axis=0, keepdims=True)
    lbv = jnp.sum(soft[1:layer + 1], axis=0, keepdims=True)

    pad = max(HGRN_SUB - tb, 0)
    L = min(HGRN_CHUNK, tb + pad)
    c = min(HGRN_SUB, L)
    tri = (lax.broadcasted_iota(jnp.int32, (L, L), 0)
           >= lax.broadcasted_iota(jnp.int32, (L, L), 1)).astype(BF16)
    rowc = lax.broadcasted_iota(jnp.int32, (c, 1), 0)
    gn = gn_ref[...]

    def padrows(x):
        if pad == 0:
            return x
        return jnp.concatenate([x, jnp.zeros((pad, x.shape[1]), F32)], axis=0)

    def chunk(rows, n_valid):
        for hh in range(HGRN_HEADS):
            ls = slice(hh * HGRN_DIM, (hh + 1) * HGRN_DIM)
            lbh = lbv[:, ls]
            f = lbh + (1.0 - lbh) * _sigmoid(f_ref[rows, ls])
            lf = padrows(jnp.log(f))
            kk = padrows(1.0 - f)
            qv = padrows(q_ref[rows, ls])
            vv = padrows(i_ref[rows, ls])
            hi, mid, lo = _split3(lf)
            b = _dot(tri, hi) + _dot(tri, mid) + _dot(tri, lo)
            st = st_s[hh]
            o_inter = _dot_nt((qv * jnp.exp(b)).astype(BF16), st.astype(BF16))
            vb = vv.astype(BF16)
            parts = []
            for si in range(L // c):
                rs = slice(si * c, (si + 1) * c)
                b_i = b[rs]
                q_i = qv[rs]
                k_i = kk[rs]
                v_i = vv[rs]
                o_i = o_inter[rs]
                if si > 0:
                    b_prev = b[si * c - 1:si * c]
                    qt = (q_i * jnp.exp(b_i - b_prev)).astype(BF16)
                    kt = (kk[:si * c] * jnp.exp(b_prev - b[:si * c])).astype(BF16)
                    o_i = o_i + _dot(_dot_nt(qt, kt).astype(BF16), vb[:si * c])
                for s in range(min(c, max(n_valid - si * c, 0))):
                    dec = jnp.exp(jnp.minimum(b_i - b_i[s:s + 1], 0.0))
                    a = jnp.sum(q_i * k_i[s:s + 1] * dec, axis=-1, keepdims=True)
                    o_i = o_i + jnp.where(rowc >= s, a, 0.0) * v_i[s:s + 1]
                parts.append(o_i)
            o = jnp.concatenate(parts, axis=0) if len(parts) > 1 else parts[0]
            o = o[:n_valid]
            o = o * lax.rsqrt(jnp.mean(o * o, axis=-1, keepdims=True) + EPS) * gn
            gv = g_ref[rows, ls]
            o_ref[rows, ls] = o * (gv * _sigmoid(gv))
            b_last = b[L - 1:L]
            kdec = (kk * jnp.exp(b_last - b)).astype(BF16)
            st_s[hh] = st * jnp.exp(b_last) + _dot_tn(vb, kdec)

    if pad:
        chunk(slice(0, tb), tb)
    else:
        def body(ch, _):
            chunk(pl.ds(pl.multiple_of(ch * L, L), L), L)
            return 0

        lax.fori_loop(0, tb // L, body, 0)

    for hh in range(HGRN_HEADS):
        s_out_ref[0, hh] = st_s[hh].T


def _hgrn(qh, fh, ih, gh, lb, s0, gnorm, *, bsz, t, layer):
    n = bsz * t
    tb = min(ROW_TILE, t)
    nt = t // tb
    row_spec = pl.BlockSpec((tb, SLAB), lambda b, i: (b * nt + i, 0))
    st_spec = pl.BlockSpec((1, HGRN_HEADS, HGRN_DIM, HGRN_DIM), lambda b, i: (b, 0, 0, 0))
    return pl.pallas_call(
        functools.partial(_hgrn_kernel, layer=layer),
        grid=(bsz, nt),
        in_specs=[row_spec] * 4 + [_const(lb.shape), st_spec, _const((1, HGRN_DIM))],
        out_specs=[row_spec, st_spec],
        out_shape=[jax.ShapeDtypeStruct((n, SLAB), F32),
                   jax.ShapeDtypeStruct((bsz, HGRN_HEADS, HGRN_DIM, HGRN_DIM), F32)],
        scratch_shapes=[pltpu.VMEM((HGRN_HEADS, HGRN_DIM, HGRN_DIM), F32)],
        name="hgrn2")(qh, fh, ih, gh, lb, s0, gnorm.reshape(1, HGRN_DIM))


def _tail_kernel(*refs, chain):
    if chain:
        (h_ref, ya_ref, yb_ref, p_ref, wo_ref, gmp_ref, gfp_ref, wug_ref, wuv_ref, wcv_ref, wdn_ref,
         gfo_ref, wpp_ref, wpg_ref, gpl_ref, ho_ref, ffn_ref, acc_s, car_s) = refs
    else:
        (h_ref, ya_ref, yb_ref, p_ref, p1_ref, p2_ref, wo_ref, gmp_ref, gfp_ref, wug_ref, wuv_ref,
         wcv_ref, wdn_ref, gfo_ref, wpp_ref, wpg_ref, gpl_ref, ho_ref, ffn_ref, acc_s) = refs
    tm = h_ref.shape[0]
    nch = wug_ref.shape[0]
    cw = wug_ref.shape[2]
    mix = _dot(ya_ref[...].astype(BF16), wo_ref[0]) + _dot(yb_ref[...].astype(BF16), wo_ref[1])
    h1 = h_ref[...] + _rms(mix, gmp_ref[...])
    a = _rms(h1, gfp_ref[...]).astype(BF16)
    acc_s[...] = jnp.zeros(acc_s.shape, F32)
    rowi = lax.broadcasted_iota(jnp.int32, (tm, cw), 0)

    if chain:
        @pl.when(pl.program_id(1) == 0)
        def _():
            car_s[...] = jnp.zeros(car_s.shape, F32)

    def conv(u, prev, w):
        r1 = pltpu.roll(u, 1, 0)
        r2 = pltpu.roll(u, 2, 0)
        if chain:
            last = prev[SUBLANES - 1:SUBLANES]
            last2 = prev[SUBLANES - 2:SUBLANES - 1]
            p1 = jnp.where(rowi == 0, last, r1)
            p2 = jnp.where(rowi == 0, last2, jnp.where(rowi == 1, last, r2))
        else:
            rm = rowi % SUBLANES
            p1 = jnp.where(rm == 0, prev[0], r1)
            p2 = jnp.where(rm < 2, prev[1], r2)
        return w[3:4] + w[0:1] * p2 + w[1:2] * p1 + w[2:3] * u

    def chunk(c, _):
        ug = _dot(a, wug_ref[c])
        uv = _dot(a, wuv_ref[c])
        w = wcv_ref[c]
        if chain:
            cg = conv(ug, car_s[c], w[0:4])
            cv = conv(uv, car_s[nch + c], w[4:8])
            car_s[c] = ug[tm - SUBLANES:tm]
            car_s[nch + c] = uv[tm - SUBLANES:tm]
            ffn_ref[0, c] = ug[tm - SUBLANES:tm]
            ffn_ref[0, nch + c] = uv[tm - SUBLANES:tm]
        else:
            cg = conv(ug, (p1_ref[c], p2_ref[c]), w[0:4])
            cv = conv(uv, (p1_ref[nch + c], p2_ref[nch + c]), w[4:8])
            ffn_ref[c] = ug
            ffn_ref[nch + c] = uv
        act = (_gelu(cg) * cv).astype(BF16)
        acc_s[...] += _dot(act, wdn_ref[c])
        return 0

    lax.fori_loop(0, nch, chunk, 0)
    h2 = h1 + _rms(acc_s[...], gfo_ref[...])
    ple = _dot(p_ref[...].astype(BF16), wpp_ref[...]) * _sigmoid(_dot(h2.astype(BF16), wpg_ref[...]))
    ho_ref[...] = h2 + _rms(ple, gpl_ref[...])


def _tail(h, ya, yb, p, ffn0, w, *, bsz, t):
    n, d = h.shape
    nch = N_FF_CHUNKS
    cw = FF_CHUNK
    chain = t > SUBLANES
    consts = [w['wo'], w['gmp'], w['gfp'], w['wug'], w['wuv'], w['wcv'], w['wdn'], w['gfo'], w['wpp'],
              w['wpg'], w['gpl']]
    const_specs = [_const(x.shape) for x in consts]
    if chain:
        assert ffn0 is None, "a long sequence starts from an empty ConvFFN buffer"
        tm = min(MATMUL_TILE, t)
        nt = t // tm
        grid = (bsz, nt)
        rows = lambda width: pl.BlockSpec((tm, width), lambda b, i: (b * nt + i, 0))
        ffn_spec = pl.BlockSpec((1, 2 * nch, SUBLANES, cw), lambda b, i: (b, 0, 0, 0))
        ffn_sds = jax.ShapeDtypeStruct((bsz, 2 * nch, SUBLANES, cw), F32)
        extra, extra_specs = [], []
        scratch = [pltpu.VMEM((tm, d), F32), pltpu.VMEM((2 * nch, SUBLANES, cw), F32)]
    else:
        assert t == SUBLANES
        tm = n
        grid = (1, 1)
        rows = lambda width: pl.BlockSpec((tm, width), lambda b, i: (0, 0))
        ffn_spec = _full((2 * nch, tm, cw))
        ffn_sds = jax.ShapeDtypeStruct((2 * nch, tm, cw), F32)
        buf = ffn0.astype(F32).reshape(bsz, 2, 2 * nch, cw).transpose(2, 0, 1, 3)
        zero = jnp.zeros((2 * nch, bsz, SUBLANES - 2, cw), F32)
        p1 = jnp.concatenate([buf[:, :, 1:2], zero, zero[:, :, :1]], axis=2).reshape(2 * nch, tm, cw)
        p2 = jnp.concatenate([buf, zero], axis=2).reshape(2 * nch, tm, cw)
        extra = [p1, p2]
        extra_specs = [_full(p1.shape), _full(p2.shape)]
        scratch = [pltpu.VMEM((tm, d), F32)]
    ho, ffn = pl.pallas_call(
        functools.partial(_tail_kernel, chain=chain),
        grid=grid,
        in_specs=[rows(d), rows(SLAB), rows(SLAB), rows(p.shape[1])] + extra_specs + const_specs,
        out_specs=[rows(d), ffn_spec],
        out_shape=[jax.ShapeDtypeStruct((n, d), F32), ffn_sds],
        scratch_shapes=scratch,
        name="layer_tail")(h, ya, yb, p, *extra, *consts)
    if chain:
        st = ffn[:, :, SUBLANES - 2:, :]
    else:
        st = ffn.reshape(2 * nch, bsz, SUBLANES, cw)[:, :, SUBLANES - 2:, :].transpose(1, 0, 2, 3)
    return ho, st.transpose(0, 2, 1, 3).reshape(bsz, 2, 2 * nch * cw)


def _tail_weights(i, w_out, g_mix_post, g_ffn_pre, g_ffn_post, w_ffn_up, w_ffn_conv, b_ffn_conv,
                  w_ffn_down, w_ple_proj, w_ple_gate, g_ple):
    d = w_out.shape[1]
    nch, cw = N_FF_CHUNKS, FF_CHUNK
    up = w_ffn_up[i].astype(BF16).reshape(d, 2, nch, cw).transpose(1, 2, 0, 3)
    taps = jnp.concatenate([w_ffn_conv[i].astype(F32), b_ffn_conv[i].astype(F32)[None]], axis=0)
    wcv = taps.reshape(4, 2, nch, cw).transpose(2, 1, 0, 3).reshape(nch, 8, cw)
    row = lambda g: g.astype(F32).reshape(1, d)
    return dict(
        wo=w_out.astype(BF16).reshape(2, SLAB, d), gmp=row(g_mix_post[i]), gfp=row(g_ffn_pre[i]),
        wug=up[0], wuv=up[1], wcv=wcv, wdn=w_ffn_down[i].astype(BF16).reshape(nch, cw, d),
        gfo=row(g_ffn_post[i]), wpp=w_ple_proj[i].astype(BF16), wpg=w_ple_gate[i].astype(BF16),
        gpl=row(g_ple[i]))


def kernel(x_prompt, x_sample, p_prompt, p_sample, cache_moba_k, cache_moba_v, state_s5_re, state_s5_im,
           state_hgrn, cache_sb_k, cache_sb_v, state_ffn, page_table,
           g_mix_pre, g_mix_post, g_ffn_pre, g_ffn_post, w_ffn_up, w_ffn_conv, b_ffn_conv, w_ffn_down,
           w_ple_proj, w_ple_gate, g_ple,
           w_in_a, w_out_a, s5_a_re, s5_a_im, s5_log_dt, s5_b_re, s5_b_im, s5_c_re, s5_c_im, s5_d,
           s5_w_glu, s5_b_glu,
           w_in_c, w_out_c, hgrn_lb, g_hgrn_norm):
    depth = g_mix_pre.shape[0]
    d_model = x_prompt.shape[-1]
    n_pages = page_table.shape[1]
    past_len = n_pages * cache_moba_k.shape[2]
    page_table = page_table.astype(jnp.int32)

    layers = []
    for i in range(depth):
        j = i // 2
        lw = {}
        if i % 2 == 0:
            lw['w_in'] = w_in_a[j].astype(BF16)
            lw['disc'] = _s5_discretize(s5_a_re[j], s5_a_im[j], s5_log_dt[j])
            lw['wb'], lw['wc'] = _s5_weights(s5_b_re[j], s5_b_im[j], s5_c_re[j], s5_c_im[j])
            lw['d'] = s5_d[j].astype(F32).reshape(S5_WIDTH)
            lw['wglu'] = s5_w_glu[j].astype(BF16)
            lw['bglu'] = s5_b_glu[j].astype(F32)
            w_out = w_out_a[j]
        else:
            lw['w_in'] = w_in_c[j].astype(BF16)
            w_out = w_out_c[j]
        lw['tail'] = _tail_weights(i, w_out, g_mix_post, g_ffn_pre, g_ffn_post, w_ffn_up, w_ffn_conv,
                                   b_ffn_conv, w_ffn_down, w_ple_proj, w_ple_gate, g_ple)
        layers.append(lw)

    def pool2d(pool, j):
        return pool[j].reshape(pool.shape[1], pool.shape[2] * HEADS, HEAD_DIM)

    def run(x, p, q_start, s5_re0, s5_im0, hgrn0, ffn0, has_past):
        bsz, t, _ = x.shape
        n = bsz * t
        h = x.astype(F32).reshape(n, d_model)
        tables = _rope_tables(t, q_start)
        if t < MATMUL_TILE:
            tables = tuple(jnp.tile(tb, (n // t, 1)) for tb in tables)
        mk, mv, sr, si, hs, sk, sv, fb = [], [], [], [], [], [], [], []
        for i in range(depth):
            j = i // 2
            lw = layers[i]
            g_pre = g_mix_pre[i].astype(F32)
            if i % 2 == 0:
                u, q, k, v = _inproj(h, g_pre, lw['w_in'], tables, rope_slabs=(1, 2))
                y_a, hr, hi = _s5(u, s5_re0[j].astype(F32).reshape(bsz, 1, S5_LANES),
                                  s5_im0[j].astype(F32).reshape(bsz, 1, S5_LANES), lw['disc'], lw['wb'],
                                  lw['wc'], lw['d'], lw['wglu'], lw['bglu'], bsz=bsz, t=t)
                if has_past:
                    y_b = _moba_sample(q, k, v, pool2d(cache_moba_k, j), pool2d(cache_moba_v, j),
                                       page_table, bsz=bsz, t=t)
                else:
                    y_b = _moba_prompt(q, k, v, bsz=bsz, t=t)
                mk.append(k.reshape(bsz, t, HEADS, HEAD_DIM))
                mv.append(v.reshape(bsz, t, HEADS, HEAD_DIM))
                sr.append(hr.reshape(bsz, S5_GROUPS, S5_STATE))
                si.append(hi.reshape(bsz, S5_GROUPS, S5_STATE))
            else:
                qh, fh, ih, gh, q, k, v = _inproj(h, g_pre, lw['w_in'], tables, rope_slabs=())
                y_a, s_fin = _hgrn(qh, fh, ih, gh, hgrn_lb.astype(F32), hgrn0[j].astype(F32),
                                   g_hgrn_norm[j].astype(F32), bsz=bsz, t=t, layer=i)
                if has_past:
                    y_b = _sb_sample(q, k, v, pool2d(cache_sb_k, j), pool2d(cache_sb_v, j), page_table,
                                     bsz=bsz, t=t)
                else:
                    y_b = _sb_prompt(q, k, v, bsz=bsz, t=t)
                hs.append(s_fin)
                sk.append(k.reshape(bsz, t, HEADS, HEAD_DIM))
                sv.append(v.reshape(bsz, t, HEADS, HEAD_DIM))
            h, buf = _tail(h, y_a, y_b, p[i].astype(F32).reshape(n, p.shape[-1]),
                           None if ffn0 is None else ffn0[i], lw['tail'], bsz=bsz, t=t)
            fb.append(buf)
        return (h.reshape(bsz, t, d_model), jnp.stack(mk), jnp.stack(mv), jnp.stack(sr), jnp.stack(si),
                jnp.stack(hs), jnp.stack(sk), jnp.stack(sv), jnp.stack(fb))

    bp = x_prompt.shape[0]
    n_a = (depth + 1) // 2
    n_c = depth // 2
    outs_p = run(x_prompt, p_prompt, 0,
                 jnp.zeros((n_a, bp, S5_GROUPS, S5_STATE), F32), jnp.zeros((n_a, bp, S5_GROUPS, S5_STATE), F32),
                 jnp.zeros((n_c, bp, HGRN_HEADS, HGRN_DIM, HGRN_DIM), F32), None, False)
    outs_s = run(x_sample, p_sample, past_len, state_s5_re, state_s5_im, state_hgrn, state_ffn, True)
    return (outs_p[0], outs_s[0]) + tuple(outs_p[1:]) + tuple(outs_s[1:])
```

```python
import functools
import math

import jax
import jax.numpy as jnp
from jax import lax
from jax.experimental import pallas as pl
from jax.experimental.pallas import tpu as pltpu

F32 = jnp.float32
BF16 = jnp.bfloat16

S5_WIDTH = 512
S5_GROUPS = 32
S5_GROUP = 16
S5_STATE = 64
S5_LANES = S5_GROUPS * S5_STATE
HEADS = 8
HEAD_DIM = 64
ATT_WIDTH = HEADS * HEAD_DIM
ROPE_DIM = 16
ROPE_THETA = 500000.0
MOBA_BLOCK = 256
MOBA_TOPK = 3
HGRN_HEADS = 4
HGRN_DIM = 128
HGRN_CHUNK = 64
HGRN_SUB = 16
D_FF = 2816
FF_CHUNK = 256
N_FF_CHUNKS = D_FF // FF_CHUNK
PAGE = 128
EPS = 1e-6
NEG = -1e30
SLAB = 512
LANES = 128
SUBLANES = 8
ROW_TILE = 256
MATMUL_TILE = 512
SB_BLOCK = 256
PAGES_PER_STEP = 8
EXP_UNDERFLOW = -104.0


def _dot(a, b):
    return jnp.dot(a, b, preferred_element_type=F32)


def _dot_nt(a, b):
    return lax.dot_general(a, b, (((1,), (1,)), ((), ())), preferred_element_type=F32)


def _dot_tn(a, b):
    return lax.dot_general(a, b, (((0,), (0,)), ((), ())), preferred_element_type=F32)


def _split3(x):
    hi = x.astype(BF16)
    r1 = x - hi.astype(F32)
    mid = r1.astype(BF16)
    lo = (r1 - mid.astype(F32)).astype(BF16)
    return hi, mid, lo


def _rms(x, g):
    return x * lax.rsqrt(jnp.mean(x * x, axis=-1, keepdims=True) + EPS) * g


def _sigmoid(x):
    return 1.0 / (1.0 + jnp.exp(-x))


def _gelu(x):
    return 0.5 * x * (1.0 + jnp.tanh(0.7978845608028654 * (x + 0.044715 * (x * x * x))))


def _full(shape):
    nd = len(shape)
    return pl.BlockSpec(shape, lambda *_: (0,) * nd)


def _const(shape):
    nd = len(shape)
    return pl.BlockSpec(shape, lambda *_: (0,) * nd, pipeline_mode=pl.Buffered(1))


def _rope_table_kernel(c_ref, s1_ref, s2_ref, *, q_start):
    rows, lanes = c_ref.shape
    i = pl.program_id(0)
    lane = lax.broadcasted_iota(jnp.int32, (rows, lanes), 1)
    pos = lax.broadcasted_iota(jnp.int32, (rows, lanes), 0) + i * rows + q_start
    d = lane % HEAD_DIM
    half = ROPE_DIM // 2
    inv = jnp.exp((d % half).astype(F32) * (-math.log(ROPE_THETA) / half))
    ang = pos.astype(F32) * inv
    cs = jnp.cos(ang)
    sn = jnp.sin(ang)
    c_ref[...] = jnp.where(d < ROPE_DIM, cs, 1.0)
    s1_ref[...] = jnp.where(d < half, -sn, 0.0)
    s2_ref[...] = jnp.where((d >= half) & (d < ROPE_DIM), sn, 0.0)


def _rope_tables(t, q_start):
    rows = min(t, 512)
    sds = jax.ShapeDtypeStruct((t, LANES), F32)
    spec = pl.BlockSpec((rows, LANES), lambda i: (i, 0))
    return pl.pallas_call(
        functools.partial(_rope_table_kernel, q_start=q_start),
        grid=(t // rows,), out_shape=(sds, sds, sds), out_specs=(spec, spec, spec),
        name="rope_tables")()


def _inproj_kernel(h_ref, g_ref, w_ref, c_ref, s1_ref, s2_ref, *out_refs, n_slabs, rope_slabs,
                   t_slabs):
    a = _rms(h_ref[...], g_ref[...]).astype(BF16)
    half = ROPE_DIM // 2
    for s in range(n_slabs):
        o_ref = out_refs[s]
        z = _dot(a, w_ref[:, s * SLAB:(s + 1) * SLAB])
        if s in rope_slabs:
            c = c_ref[...]
            s1 = s1_ref[...]
            s2 = s2_ref[...]
            for j in range(SLAB // LANES):
                x = z[:, j * LANES:(j + 1) * LANES]
                o_ref[:, j * LANES:(j + 1) * LANES] = (
                    x * c + pltpu.roll(x, LANES - half, 1) * s1 + pltpu.roll(x, half, 1) * s2)
        else:
            o_ref[...] = z
        if s in t_slabs:
            out_refs[n_slabs + t_slabs.index(s)][...] = o_ref[...].T


def _inproj(h, g, w, tables, rope_slabs, t_slabs=(), seq=None):
    n, d = h.shape
    n_slabs = w.shape[1] // SLAB
    tm = min(MATMUL_TILE, n)
    t_tiles = tables[0].shape[0] // tm
    tab_spec = pl.BlockSpec((tm, LANES), lambda i: (i % t_tiles, 0))
    out_spec = pl.BlockSpec((tm, SLAB), lambda i: (i, 0))
    out_specs = [out_spec] * n_slabs
    out_shape = [jax.ShapeDtypeStruct((n, SLAB), F32)] * n_slabs
    if t_slabs:
        bsz, t = seq
        assert t % tm == 0
        nt = t // tm
        out_specs += [pl.BlockSpec((None, SLAB, tm), lambda i: (i // nt, 0, i % nt))] * len(t_slabs)
        out_shape += [jax.ShapeDtypeStruct((bsz, SLAB, t), F32)] * len(t_slabs)
    return pl.pallas_call(
        functools.partial(_inproj_kernel, n_slabs=n_slabs, rope_slabs=rope_slabs, t_slabs=t_slabs),
        grid=(n // tm,),
        in_specs=[pl.BlockSpec((tm, d), lambda i: (i, 0)), _const((1, d)), _const(w.shape),
                  tab_spec, tab_spec, tab_spec],
        out_specs=out_specs, out_shape=out_shape,
        name="inproj")(h, g.reshape(1, d), w, *tables)


def _s5_disc_kernel(ar_ref, ai_ref, ldt_ref, pre_ref, pim_ref, cc_ref):
    ar = ar_ref[...]
    ai = ai_ref[...]
    dt = jnp.exp(ldt_ref[...])
    row = lax.broadcasted_iota(jnp.int32, (2 * SUBLANES, 1), 0)
    n = jnp.where(row < SUBLANES, row + 1,
                  jnp.where(row == SUBLANES + 1, 2, jnp.where(row == SUBLANES + 2, 4, 1))).astype(F32)
    mag = jnp.exp(ar * dt * n)
    ang = ai * dt * n
    pre = mag * jnp.cos(ang)
    pim = mag * jnp.sin(ang)
    pre_ref[...] = pre
    pim_ref[...] = pim
    abr = pre[0:1]
    abi = pim[0:1]
    den = ar * ar + ai * ai
    nr = abr - 1.0
    cc_ref[0:1, :] = (nr * ar + abi * ai) / den
    cc_ref[1:2, :] = (abi * ar - nr * ai) / den


def _s5_discretize(a_re, a_im, log_dt):
    ar = a_re.reshape(1, S5_LANES)
    ai = a_im.reshape(1, S5_LANES)
    ldt = jnp.repeat(log_dt, S5_STATE).reshape(1, S5_LANES)
    return pl.pallas_call(
        _s5_disc_kernel,
        out_shape=(jax.ShapeDtypeStruct((2 * SUBLANES, S5_LANES), F32),
                   jax.ShapeDtypeStruct((2 * SUBLANES, S5_LANES), F32),
                   jax.ShapeDtypeStruct((2, S5_LANES), F32)),
        name="s5_discretize")(ar, ai, ldt)


SCAN_LANES = 512


def _s5_kernel(u_ref, h0r_ref, h0i_ref, pre_ref, pim_ref, cc_ref, wb_ref, wc_ref, d_ref,
               wglu_ref, bglu_ref, y_ref, hr_out_ref, hi_out_ref, xr_s, xi_s, car_s, *, chain):
    tm = u_ref.shape[0]
    nb = tm // SUBLANES
    u = u_ref[...]
    ub = u.astype(BF16)
    half_in = S5_WIDTH // 2
    half_st = S5_LANES // 2
    for hf in range(2):
        bu = _dot(ub[:, hf * half_in:(hf + 1) * half_in], wb_ref[hf])
        bur = bu[:, :half_st]
        bui = bu[:, half_st:]
        ls = slice(hf * half_st, (hf + 1) * half_st)
        cr = cc_ref[0:1, ls]
        ci = cc_ref[1:2, ls]
        xr_s[:, ls] = cr * bur - ci * bui
        xi_s[:, ls] = cr * bui + ci * bur

    if chain:
        t = pl.program_id(1)

        @pl.when(t == 0)
        def _():
            car_s[0:1, :] = h0r_ref[0]
            car_s[1:2, :] = h0i_ref[0]

    rowi = lax.broadcasted_iota(jnp.int32, (SUBLANES, SCAN_LANES), 0)
    for lc in range(S5_LANES // SCAN_LANES):
        ls = slice(lc * SCAN_LANES, (lc + 1) * SCAN_LANES)
        pre = pre_ref[:, ls]
        pim = pim_ref[:, ls]

        def body(i, carry, ls=ls, pre=pre, pim=pim):
            r0 = pl.multiple_of(i * SUBLANES, SUBLANES)
            hr = xr_s[pl.ds(r0, SUBLANES), ls]
            hi = xi_s[pl.ds(r0, SUBLANES), ls]
            for k, prow in ((1, SUBLANES), (2, SUBLANES + 1), (4, SUBLANES + 2)):
                ar = pre[prow:prow + 1]
                ai = pim[prow:prow + 1]
                sr = jnp.where(rowi >= k, pltpu.roll(hr, k, 0), 0.0)
                si = jnp.where(rowi >= k, pltpu.roll(hi, k, 0), 0.0)
                hr, hi = hr + ar * sr - ai * si, hi + ar * si + ai * sr
            if chain:
                cr_, ci_ = carry
            else:
                cr_ = h0r_ref[i, :, ls]
                ci_ = h0i_ref[i, :, ls]
            pr8 = pre[0:SUBLANES]
            pi8 = pim[0:SUBLANES]
            hr, hi = hr + pr8 * cr_ - pi8 * ci_, hi + pr8 * ci_ + pi8 * cr_
            xr_s[pl.ds(r0, SUBLANES), ls] = hr
            xi_s[pl.ds(r0, SUBLANES), ls] = hi
            last_r = hr[SUBLANES - 1:SUBLANES]
            last_i = hi[SUBLANES - 1:SUBLANES]
            if chain:
                return last_r, last_i
            hr_out_ref[i, :, ls] = last_r
            hi_out_ref[i, :, ls] = last_i
            return carry

        if chain:
            cr_, ci_ = lax.fori_loop(0, nb, body, (car_s[0:1, ls], car_s[1:2, ls]))
            car_s[0:1, ls] = cr_
            car_s[1:2, ls] = ci_
        else:
            lax.fori_loop(0, nb, body, 0)

    if chain:
        hr_out_ref[0] = car_s[0:1, :]
        hi_out_ref[0] = car_s[1:2, :]

    ys = []
    for hf in range(2):
        ls = slice(hf * half_st, (hf + 1) * half_st)
        hcat = jnp.concatenate([xr_s[:, ls], xi_s[:, ls]], axis=1).astype(BF16)
        ys.append(_dot(hcat, wc_ref[hf]))
    y = jnp.concatenate(ys, axis=1) + d_ref[...] * u
    y = _gelu(y)
    y_ref[...] = y * _sigmoid(_dot(y.astype(BF16), wglu_ref[...]) + bglu_ref[...])


def _s5_weights(b_re, b_im, c_re, c_im):
    gh = S5_GROUPS // 2
    eye = jnp.eye(gh, dtype=F32)

    def bmat(b):
        bt = b.astype(F32).transpose(0, 2, 1).reshape(2, gh, S5_GROUP, S5_STATE)
        return jnp.einsum('xghp,gk->xghkp', bt, eye).reshape(2, gh * S5_GROUP, gh * S5_STATE)

    def cmat(c):
        ct = c.astype(F32).transpose(0, 2, 1).reshape(2, gh, S5_STATE, S5_GROUP)
        return jnp.einsum('xgph,gk->xgpkh', ct, eye).reshape(2, gh * S5_STATE, gh * S5_GROUP)

    wb = jnp.concatenate([bmat(b_re), bmat(b_im)], axis=2).astype(BF16)
    wc = jnp.concatenate([cmat(c_re), -cmat(c_im)], axis=1).astype(BF16)
    return wb, wc


def _s5(u, h0r, h0i, disc, wb, wc, dvec, wglu, bglu, *, bsz, t):
    n = bsz * t
    pre, pim, cc = disc
    chain = t > SUBLANES
    if chain:
        tm = min(ROW_TILE, t)
        grid = (bsz, t // tm)
        row_spec = pl.BlockSpec((tm, S5_WIDTH), lambda b, i: (b * (t // tm) + i, 0))
        st_spec = pl.BlockSpec((1, 1, S5_LANES), lambda b, i: (b, 0, 0))
    else:
        assert t == SUBLANES
        tm = n
        grid = (1, 1)
        row_spec = pl.BlockSpec((tm, S5_WIDTH), lambda b, i: (0, 0))
        st_spec = pl.BlockSpec((bsz, 1, S5_LANES), lambda b, i: (0, 0, 0))
    st_sds = jax.ShapeDtypeStruct((bsz, 1, S5_LANES), F32)
    return pl.pallas_call(
        functools.partial(_s5_kernel, chain=chain),
        grid=grid,
        in_specs=[row_spec, st_spec, st_spec, _const(pre.shape), _const(pim.shape), _const(cc.shape),
                  _const(wb.shape), _const(wc.shape), _const((1, S5_WIDTH)), _const(wglu.shape),
                  _const((1, S5_WIDTH))],
        out_specs=[row_spec, st_spec, st_spec],
        out_shape=[jax.ShapeDtypeStruct((n, S5_WIDTH), F32), st_sds, st_sds],
        scratch_shapes=[pltpu.VMEM((tm, S5_LANES), F32), pltpu.VMEM((tm, S5_LANES), F32),
                        pltpu.VMEM((2, S5_LANES), F32)],
        name="s5_mixer")(u, h0r, h0i, pre, pim, cc, wb, wc, dvec.reshape(1, S5_WIDTH), wglu,
                         bglu.reshape(1, S5_WIDTH))


def _top_blocks(gate, n_past, col):
    g = jnp.where(col < n_past, gate, NEG)
    sel = jnp.zeros(gate.shape, F32)
    colf = col.astype(F32)
    for _ in range(MOBA_TOPK):
        m = jnp.max(g, axis=-1, keepdims=True)
        first = jnp.min(jnp.where(g == m, colf, float(LANES)), axis=-1, keepdims=True)
        pick = colf == first
        sel = jnp.where(pick & (m > 0.5 * NEG), 1.0, sel)
        g = jnp.where(pick, -3e38, g)
    return sel


def _top_blocks_t(gate, n_past):
    rowf = lax.broadcasted_iota(jnp.int32, gate.shape, 0).astype(F32)
    g = jnp.where(rowf < n_past, gate, NEG)
    sel = jnp.zeros(gate.shape, F32)
    for _ in range(MOBA_TOPK):
        m = jnp.max(g, axis=0, keepdims=True)
        first = jnp.min(jnp.where(g == m, rowf, float(LANES)), axis=0, keepdims=True)
        pick = rowf == first
        sel = jnp.where(pick & (m > 0.5 * NEG), 1.0, sel)
        g = jnp.where(pick, -3e38, g)
    return sel


def _moba_prompt_kernel(q_ref, k_ref, v_ref, o_ref, kb_s, vt_s, kmean_s, sel_s, *, nblk):
    bs = MOBA_BLOCK
    i = pl.program_id(2)

    @pl.when(i == 0)
    def _():
        kmean_s[...] = jnp.zeros(kmean_s.shape, F32)
        for j in range(nblk):
            kj = k_ref[j * bs:(j + 1) * bs, :]
            kb_s[j * bs:(j + 1) * bs, :] = kj.astype(BF16)
            kmean_s[j:j + 1, :] = jnp.sum(kj, axis=0, keepdims=True) * (1.0 / bs)
            vt_s[j] = v_ref[j * bs:(j + 1) * bs, :].T.astype(BF16)

    lane = lax.broadcasted_iota(jnp.int32, (1, LANES), 1)
    rowd = lax.broadcasted_iota(jnp.int32, (LANES, 1), 0)
    causal = (lax.broadcasted_iota(jnp.int32, (bs, bs), 0)
              <= lax.broadcasted_iota(jnp.int32, (bs, bs), 1))
    kmb = kmean_s[...].astype(BF16)
    qf = q_ref[...]
    r0 = pl.multiple_of(i * bs, bs)
    ki = kb_s[pl.ds(r0, bs), :]
    vti = vt_s[i]
    qss = []
    carry = []
    for hd in range(2):
        lm = (lane >= HEAD_DIM * hd) & (lane < HEAD_DIM * (hd + 1))
        qh = jnp.where(lm, qf, 0.0)
        qs = (qh * HEAD_DIM ** -0.5).astype(BF16)
        sel = _top_blocks_t(_dot_nt(kmb, qh.astype(BF16)), i)
        for b in range(nblk):
            sel_s[hd, b] = sel[b:b + 1, :]
        s = jnp.where(causal, _dot_nt(ki, qs), NEG)
        m = jnp.max(s, axis=0, keepdims=True)
        p = jnp.exp(s - m)
        carry += [m, jnp.sum(p, axis=0, keepdims=True), _dot(vti, p.astype(BF16))]
        qss.append(qs)

    def kvpair(jj, carry):
        c0 = pl.multiple_of(jj * 2 * bs, 2 * bs)
        kw = kb_s[pl.ds(c0, 2 * bs), :]
        out = []
        for hd in range(2):
            m, l, acc = carry[3 * hd:3 * hd + 3]
            s = _dot_nt(kw, qss[hd])
            s0 = jnp.where(sel_s[hd, 2 * jj] > 0.5, s[:bs], NEG)
            s1 = jnp.where(sel_s[hd, 2 * jj + 1] > 0.5, s[bs:], NEG)
            m_new = jnp.maximum(m, jnp.maximum(jnp.max(s0, axis=0, keepdims=True),
                                               jnp.max(s1, axis=0, keepdims=True)))
            alpha = jnp.exp(m - m_new)
            p0 = jnp.exp(s0 - m_new)
            p1 = jnp.exp(s1 - m_new)
            l = alpha * l + jnp.sum(p0, axis=0, keepdims=True) + jnp.sum(p1, axis=0, keepdims=True)
            acc = (alpha * acc + _dot(vt_s[2 * jj], p0.astype(BF16))
                   + _dot(vt_s[2 * jj + 1], p1.astype(BF16)))
            out += [m_new, l, acc]
        return tuple(out)

    carry = lax.fori_loop(0, (i + 1) // 2, kvpair, tuple(carry))
    out_t = jnp.where(rowd < HEAD_DIM, carry[2] / carry[1], carry[5] / carry[4])
    o_ref[...] = out_t.T


def _moba_prompt(q, k, v, *, bsz, t):
    nblk = t // MOBA_BLOCK
    assert t % MOBA_BLOCK == 0 and nblk % 2 == 0 and nblk <= LANES
    nblk_rows = -(-nblk // SUBLANES) * SUBLANES
    n = bsz * t
    q_spec = pl.BlockSpec((MOBA_BLOCK, LANES), lambda b, hp, i: (b * nblk + i, hp))
    kv_spec = pl.BlockSpec((t, LANES), lambda b, hp, i: (b, hp))
    return pl.pallas_call(
        functools.partial(_moba_prompt_kernel, nblk=nblk),
        grid=(bsz, ATT_WIDTH // LANES, nblk),
        in_specs=[q_spec, kv_spec, kv_spec], out_specs=q_spec,
        out_shape=jax.ShapeDtypeStruct((n, ATT_WIDTH), F32),
        scratch_shapes=[pltpu.VMEM((t, LANES), BF16),
                        pltpu.VMEM((nblk, LANES, MOBA_BLOCK), BF16),
                        pltpu.VMEM((nblk_rows, LANES), F32),
                        pltpu.VMEM((2, nblk, 1, MOBA_BLOCK), F32)],
        name="moba_prompt")(q, k, v)


def _expand_heads(q, qexp_s):
    t = q.shape[0]
    lane = lax.broadcasted_iota(jnp.int32, (1, ATT_WIDTH), 1)
    for h in range(HEADS):
        lm = (lane >= h * HEAD_DIM) & (lane < (h + 1) * HEAD_DIM)
        qexp_s[h * t:(h + 1) * t, :] = jnp.where(lm, q, 0.0)


def _collapse_heads(acc, t):
    lane = lax.broadcasted_iota(jnp.int32, (1, ATT_WIDTH), 1)
    out = jnp.zeros((t, ATT_WIDTH), F32)
    for h in range(HEADS):
        lm = (lane >= h * HEAD_DIM) & (lane < (h + 1) * HEAD_DIM)
        out = out + jnp.where(lm, acc[h * t:(h + 1) * t, :], 0.0)
    return out


def _pad_rows(x_ref, new_s):
    new_s[...] = jnp.zeros(new_s.shape, F32)
    new_s[0:x_ref.shape[0], :] = x_ref[...]
    return new_s[...].astype(BF16)


def _moba_sample_kernel(pt_ref, q_ref, kn_ref, vn_ref, *rest, n_pages, t):
    pps = PAGES_PER_STEP
    k_refs = rest[:pps]
    v_refs = rest[pps:2 * pps]
    o_ref = rest[2 * pps]
    qexp_s, new_s, s_all, gate_s, bmax_s, sel_s, m_s, l_s, acc_s = rest[2 * pps + 1:]
    ns = n_pages // pps
    s_id = pl.program_id(1)
    rows = HEADS * t
    col = lax.broadcasted_iota(jnp.int32, (rows, LANES), 1)
    scale = HEAD_DIM ** -0.5
    pages_per_block = MOBA_BLOCK // PAGE

    @pl.when(s_id == 0)
    def _():
        _expand_heads(q_ref[...], qexp_s)
        gate_s[...] = jnp.zeros(gate_s.shape, F32)
        bmax_s[...] = jnp.full(bmax_s.shape, NEG, F32)

    @pl.when(s_id < ns)
    def _():
        qe = qexp_s[...].astype(BF16)
        g = gate_s[...]
        bm = bmax_s[...]
        for i in range(pps):
            pg = s_id * pps + i
            s = _dot(qe, k_refs[i][...].astype(BF16))
            s_all[pg] = s
            mine = col == pg // pages_per_block
            g = g + jnp.where(mine, jnp.sum(s, axis=-1, keepdims=True) * (1.0 / MOBA_BLOCK), 0.0)
            bm = jnp.where(mine, jnp.maximum(bm, jnp.max(s, axis=-1, keepdims=True)), bm)
        gate_s[...] = g
        bmax_s[...] = bm

    @pl.when(s_id == ns)
    def _():
        sel = _top_blocks(gate_s[...], n_pages // pages_per_block, col)
        sel_s[...] = sel
        qe = qexp_s[...].astype(BF16)
        sn = _dot_nt(qe, _pad_rows(kn_ref, new_s)) * scale
        rowt = lax.broadcasted_iota(jnp.int32, (rows, LANES), 0) % t
        sn = jnp.where(col <= rowt, sn, NEG)
        m = jnp.maximum(jnp.max(sn, axis=-1, keepdims=True),
                        jnp.max(jnp.where(sel > 0.5, bmax_s[...] * scale, NEG), axis=-1, keepdims=True))
        pn = jnp.exp(sn - m)
        m_s[...] = jnp.broadcast_to(m, m_s.shape)
        l_s[...] = jnp.broadcast_to(jnp.sum(pn, axis=-1, keepdims=True), l_s.shape)
        acc_s[...] = _dot(pn.astype(BF16), _pad_rows(vn_ref, new_s))

    @pl.when(s_id >= ns)
    def _():
        sel = sel_s[...]
        m = m_s[:, 0:1]
        l = l_s[:, 0:1]
        acc = acc_s[...]
        for i in range(pps):
            pg = (s_id - ns) * pps + i
            selc = jnp.sum(jnp.where(col == pg // pages_per_block, sel, 0.0), axis=-1, keepdims=True)
            p = jnp.exp(jnp.where(selc > 0.5, s_all[pg] * scale, NEG) - m)
            l = l + jnp.sum(p, axis=-1, keepdims=True)
            acc = acc + _dot_nt(p.astype(BF16), v_refs[i][...].astype(BF16))
        l_s[...] = jnp.broadcast_to(l, l_s.shape)
        acc_s[...] = acc

    @pl.when(s_id == 2 * ns - 1)
    def _():
        o_ref[...] = _collapse_heads(acc_s[...] / l_s[:, 0:1], t)


def _sample_specs(page_table, n_pages, t, page_index):
    pps = PAGES_PER_STEP
    row_spec = pl.BlockSpec((t, ATT_WIDTH), lambda b, s, pt: (b, 0))

    def page_spec(i, which):
        return pl.BlockSpec((None, ATT_WIDTH, PAGE),
                            lambda b, s, pt: (pt[b, page_index(s, i, which)], 0, 0))

    in_specs = ([row_spec] * 3 + [page_spec(i, 0) for i in range(pps)]
                + [page_spec(i, 1) for i in range(pps)])
    return in_specs, row_spec


def _moba_sample(q, k_new, v_new, k_pool, v_pool, page_table, *, bsz, t):
    n_pages = page_table.shape[1]
    pps = PAGES_PER_STEP
    assert n_pages % pps == 0 and (n_pages * PAGE) % MOBA_BLOCK == 0 and t <= SUBLANES
    assert n_pages * PAGE // MOBA_BLOCK <= LANES
    ns = n_pages // pps
    rows = HEADS * t

    def page_index(s, i, which):
        step = jnp.minimum(s, ns - 1) if which == 0 else jnp.maximum(s - ns, 0)
        return step * pps + i

    in_specs, out_spec = _sample_specs(page_table, n_pages, t, page_index)
    grid_spec = pltpu.PrefetchScalarGridSpec(
        num_scalar_prefetch=1, grid=(bsz, 2 * ns), in_specs=in_specs, out_specs=out_spec,
        scratch_shapes=[pltpu.VMEM((rows, ATT_WIDTH), F32),
                        pltpu.VMEM((PAGE, ATT_WIDTH), F32),
                        pltpu.VMEM((n_pages, rows, PAGE), F32)]
                       + [pltpu.VMEM((rows, LANES), F32)] * 5
                       + [pltpu.VMEM((rows, ATT_WIDTH), F32)])
    return pl.pallas_call(
        functools.partial(_moba_sample_kernel, n_pages=n_pages, t=t),
        grid_spec=grid_spec,
        out_shape=jax.ShapeDtypeStruct((bsz * t, ATT_WIDTH), F32),
        name="moba_sample")(page_table, q, k_new, v_new, *([k_pool] * pps), *([v_pool] * pps))


def _sb_weights(z, r, tri, strict):
    n = z.shape[0]
    lg = jnp.log(1.0 + jnp.exp(-jnp.abs(z)))
    log_beta = jnp.minimum(z, 0.0) - lg
    log_keep = jnp.minimum(-z, 0.0) - lg
    if strict is not None:
        log_beta = jnp.where(strict, log_beta, NEG)
        log_keep = jnp.where(strict, log_keep, 0.0)
    st = _dot(jnp.concatenate(_split3(log_keep), axis=0), tri)
    later = st[:n] + st[n:2 * n] + st[2 * n:]
    w = jnp.exp(log_beta + later + r)
    return w.astype(BF16), r + jnp.sum(log_keep, axis=-1, keepdims=True)


def _suffix_matrix(n):
    return (lax.broadcasted_iota(jnp.int32, (n, n), 0)
            > lax.broadcasted_iota(jnp.int32, (n, n), 1)).astype(BF16)


def _sb_prompt_kernel(q_ref, k_ref, v_ref, o_ref, kb_s, vb_s):
    bs = SB_BLOCK
    i = pl.program_id(2)

    @pl.when(i == 0)
    def _():
        for j in range(k_ref.shape[0] // bs):
            kb_s[j * bs:(j + 1) * bs, :] = k_ref[j * bs:(j + 1) * bs, :].astype(BF16)
            vb_s[j * bs:(j + 1) * bs, :] = v_ref[j * bs:(j + 1) * bs, :].astype(BF16)

    lane = lax.broadcasted_iota(jnp.int32, (1, LANES), 1)
    tri = _suffix_matrix(bs)
    strict = (lax.broadcasted_iota(jnp.int32, (bs, bs), 1)
              < lax.broadcasted_iota(jnp.int32, (bs, bs), 0))
    qf = q_ref[...]
    qss = []
    for hd in range(2):
        lm = (lane >= HEAD_DIM * hd) & (lane < HEAD_DIM * (hd + 1))
        qss.append((jnp.where(lm, qf, 0.0) * HEAD_DIM ** -0.5).astype(BF16))

    def block(j, rs, mask):
        c0 = pl.multiple_of(j * bs, bs)
        kj = kb_s[pl.ds(c0, bs), :]
        vj = vb_s[pl.ds(c0, bs), :]
        out = []
        for hd in range(2):
            w, r = _sb_weights(_dot_nt(qss[hd], kj), rs[hd], tri, mask)
            out += [_dot(w, vj), r]
        return out

    zero = jnp.zeros((bs, 1), F32)
    a0, r0, a1, r1 = block(i, (zero, zero), strict)

    def cond(c):
        j, r0, r1 = c[:3]
        return (j >= 0) & (jnp.maximum(jnp.max(r0), jnp.max(r1)) > EXP_UNDERFLOW)

    def body(c):
        j, r0, r1, a0, a1 = c
        d0, r0, d1, r1 = block(j, (r0, r1), None)
        return j - 1, r0, r1, a0 + d0, a1 + d1

    _, _, _, a0, a1 = lax.while_loop(cond, body, (i - 1, r0, r1, a0, a1))
    o_ref[...] = jnp.where(lane < HEAD_DIM, a0, a1)


def _sb_prompt(q, k, v, *, bsz, t):
    assert t % SB_BLOCK == 0
    n = bsz * t
    nq = t // SB_BLOCK
    q_spec = pl.BlockSpec((SB_BLOCK, LANES), lambda b, hp, i: (b * nq + i, hp))
    kv_spec = pl.BlockSpec((t, LANES), lambda b, hp, i: (b, hp))
    return pl.pallas_call(
        _sb_prompt_kernel,
        grid=(bsz, ATT_WIDTH // LANES, nq),
        in_specs=[q_spec, kv_spec, kv_spec], out_specs=q_spec,
        out_shape=jax.ShapeDtypeStruct((n, ATT_WIDTH), F32),
        scratch_shapes=[pltpu.VMEM((t, LANES), BF16), pltpu.VMEM((t, LANES), BF16)],
        name="sb_prompt")(q, k, v)


def _sb_sample_kernel(pt_ref, q_ref, kn_ref, vn_ref, *rest, n_pages, t):
    pps = PAGES_PER_STEP
    k_refs = rest[:pps]
    v_refs = rest[pps:2 * pps]
    o_ref = rest[2 * pps]
    qexp_s, new_s, r_s, acc_s = rest[2 * pps + 1:]
    s_id = pl.program_id(1)
    rows = HEADS * t
    tri = _suffix_matrix(PAGE)

    @pl.when(s_id == 0)
    def _():
        _expand_heads(q_ref[...] * HEAD_DIM ** -0.5, qexp_s)
        z = _dot_nt(qexp_s[...].astype(BF16), _pad_rows(kn_ref, new_s))
        col = lax.broadcasted_iota(jnp.int32, (rows, PAGE), 1)
        rowt = lax.broadcasted_iota(jnp.int32, (rows, PAGE), 0) % t
        w, r = _sb_weights(z, jnp.zeros((rows, 1), F32), tri, col < rowt)
        acc_s[...] = _dot(w, _pad_rows(vn_ref, new_s))
        r_s[...] = jnp.broadcast_to(r, r_s.shape)

    for i in range(pps):
        @pl.when(jnp.max(r_s[...]) > EXP_UNDERFLOW)
        def _(i=i):
            z = _dot(qexp_s[...].astype(BF16), k_refs[i][...].astype(BF16))
            w, r = _sb_weights(z, r_s[:, 0:1], tri, None)
            acc_s[...] = acc_s[...] + _dot_nt(w, v_refs[i][...].astype(BF16))
            r_s[...] = jnp.broadcast_to(r, r_s.shape)

    @pl.when(s_id == pl.num_programs(1) - 1)
    def _():
        o_ref[...] = _collapse_heads(acc_s[...], t)


def _sb_sample(q, k_new, v_new, k_pool, v_pool, page_table, *, bsz, t):
    n_pages = page_table.shape[1]
    pps = PAGES_PER_STEP
    assert n_pages % pps == 0 and t <= SUBLANES
    rows = HEADS * t
    in_specs, out_spec = _sample_specs(page_table, n_pages, t,
                                       lambda s, i, which: n_pages - 1 - (s * pps + i))
    grid_spec = pltpu.PrefetchScalarGridSpec(
        num_scalar_prefetch=1, grid=(bsz, n_pages // pps), in_specs=in_specs, out_specs=out_spec,
        scratch_shapes=[pltpu.VMEM((rows, ATT_WIDTH), F32),
                        pltpu.VMEM((PAGE, ATT_WIDTH), F32),
                        pltpu.VMEM((rows, LANES), F32),
                        pltpu.VMEM((rows, ATT_WIDTH), F32)])
    return pl.pallas_call(
        functools.partial(_sb_sample_kernel, n_pages=n_pages, t=t),
        grid_spec=grid_spec,
        out_shape=jax.ShapeDtypeStruct((bsz * t, ATT_WIDTH), F32),
        name="sb_sample")(page_table, q, k_new, v_new, *([k_pool] * pps), *([v_pool] * pps))


def _hgrn_kernel(q_ref, f_ref, i_ref, g_ref, lb_ref, s0_ref, gn_ref, o_ref, s_out_ref, st_s,
                 *, layer):
    tb = q_ref.shape[0]
    t = pl.program_id(1)

    @pl.when(t == 0)
    def _():
        for hh in range(HGRN_HEADS):
            st_s[hh] = s0_ref[0, hh].T

    lbp = lb_ref[...]
    e = jnp.exp(lbp - jnp.max(lbp, axis=0, keepdims=True))
    soft = e / jnp.sum(e, axis=0, keepdims=True)
    lbv = jnp.sum(soft[1:layer + 1], axis=0, keepdims=True)

    pad = max(HGRN_SUB - tb, 0)
    L = min(HGRN_CHUNK, tb + pad)
    c = min(HGRN_SUB, L)
    tri = (lax.broadcasted_iota(jnp.int32, (L, L), 0)
           >= lax.broadcasted_iota(jnp.int32, (L, L), 1)).astype(BF16)
    rowc = lax.broadcasted_iota(jnp.int32, (c, 1), 0)
    gn = gn_ref[...]

    def padrows(x):
        if pad == 0:
            return x
        return jnp.concatenate([x, jnp.zeros((pad, x.shape[1]), F32)], axis=0)

    def chunk(rows, n_valid):
        for hh in range(HGRN_HEADS):
            ls = slice(hh * HGRN_DIM, (hh + 1) * HGRN_DIM)
            lbh = lbv[:, ls]
            f = lbh + (1.0 - lbh) * _sigmoid(f_ref[rows, ls])
            lf = padrows(jnp.log(f))
            kk = padrows(1.0 - f)
            qv = padrows(q_ref[rows, ls])
            vv = padrows(i_ref[rows, ls])
            hi, mid, lo = _split3(lf)
            b = _dot(tri, hi) + _dot(tri, mid) + _dot(tri, lo)
            st = st_s[hh]
            o_inter = _dot_nt((qv * jnp.exp(b)).astype(BF16), st.astype(BF16))
            vb = vv.astype(BF16)
            parts = []
            for si in range(L // c):
                rs = slice(si * c, (si + 1) * c)
                b_i = b[rs]
                q_i = qv[rs]
                k_i = kk[rs]
                v_i = vv[rs]
                o_i = o_inter[rs]
                if si > 0:
                    b_prev = b[si * c - 1:si * c]
                    qt = (q_i * jnp.exp(b_i - b_prev)).astype(BF16)
                    kt = (kk[:si * c] * jnp.exp(b_prev - b[:si * c])).astype(BF16)
                    o_i = o_i + _dot(_dot_nt(qt, kt).astype(BF16), vb[:si * c])
                for s in range(min(c, max(n_valid - si * c, 0))):
                    dec = jnp.exp(jnp.minimum(b_i - b_i[s:s + 1], 0.0))
                    a = jnp.sum(q_i * k_i[s:s + 1] * dec, axis=-1, keepdims=True)
                    o_i = o_i + jnp.where(rowc >= s, a, 0.0) * v_i[s:s + 1]
                parts.append(o_i)
            o = jnp.concatenate(parts, axis=0) if len(parts) > 1 else parts[0]
            o = o[:n_valid]
            o = o * lax.rsqrt(jnp.mean(o * o, axis=-1, keepdims=True) + EPS) * gn
            gv = g_ref[rows, ls]
            o_ref[rows, ls] = o * (gv * _sigmoid(gv))
            b_last = b[L - 1:L]
            kdec = (kk * jnp.exp(b_last - b)).astype(BF16)
            st_s[hh] = st * jnp.exp(b_last) + _dot_tn(vb, kdec)

    if pad:
        chunk(slice(0, tb), tb)
    else:
        def body(ch, _):
            chunk(pl.ds(pl.multiple_of(ch * L, L), L), L)
            return 0

        lax.fori_loop(0, tb // L, body, 0)

    for hh in range(HGRN_HEADS):
        s_out_ref[0, hh] = st_s[hh].T


def _hgrn(qh, fh, ih, gh, lb, s0, gnorm, *, bsz, t, layer):
    n = bsz * t
    tb = min(ROW_TILE, t)
    nt = t // tb
    row_spec = pl.BlockSpec((tb, SLAB), lambda b, i: (b * nt + i, 0))
    st_spec = pl.BlockSpec((1, HGRN_HEADS, HGRN_DIM, HGRN_DIM), lambda b, i: (b, 0, 0, 0))
    return pl.pallas_call(
        functools.partial(_hgrn_kernel, layer=layer),
        grid=(bsz, nt),
        in_specs=[row_spec] * 4 + [_const(lb.shape), st_spec, _const((1, HGRN_DIM))],
        out_specs=[row_spec, st_spec],
        out_shape=[jax.ShapeDtypeStruct((n, SLAB), F32),
                   jax.ShapeDtypeStruct((bsz, HGRN_HEADS, HGRN_DIM, HGRN_DIM), F32)],
        scratch_shapes=[pltpu.VMEM((HGRN_HEADS, HGRN_DIM, HGRN_DIM), F32)],
        name="hgrn2")(qh, fh, ih, gh, lb, s0, gnorm.reshape(1, HGRN_DIM))


def _tail_kernel(*refs, chain):
    if chain:
        (h_ref, ya_ref, yb_ref, p_ref, wo_ref, gmp_ref, gfp_ref, wug_ref, wuv_ref, wcv_ref, wdn_ref,
         gfo_ref, wpp_ref, wpg_ref, gpl_ref, ho_ref, ffn_ref, acc_s, car_s) = refs
    else:
        (h_ref, ya_ref, yb_ref, p_ref, p1_ref, p2_ref, wo_ref, gmp_ref, gfp_ref, wug_ref, wuv_ref,
         wcv_ref, wdn_ref, gfo_ref, wpp_ref, wpg_ref, gpl_ref, ho_ref, ffn_ref, acc_s) = refs
    tm = h_ref.shape[0]
    nch = wug_ref.shape[0]
    cw = wug_ref.shape[2]
    mix = _dot(ya_ref[...].astype(BF16), wo_ref[0]) + _dot(yb_ref[...].astype(BF16), wo_ref[1])
    h1 = h_ref[...] + _rms(mix, gmp_ref[...])
    a = _rms(h1, gfp_ref[...]).astype(BF16)
    acc_s[...] = jnp.zeros(acc_s.shape, F32)
    rowi = lax.broadcasted_iota(jnp.int32, (tm, cw), 0)

    if chain:
        @pl.when(pl.program_id(1) == 0)
        def _():
            car_s[...] = jnp.zeros(car_s.shape, F32)

    def conv(u, prev, w):
        r1 = pltpu.roll(u, 1, 0)
        r2 = pltpu.roll(u, 2, 0)
        if chain:
            last = prev[SUBLANES - 1:SUBLANES]
            last2 = prev[SUBLANES - 2:SUBLANES - 1]
            p1 = jnp.where(rowi == 0, last, r1)
            p2 = jnp.where(rowi == 0, last2, jnp.where(rowi == 1, last, r2))
        else:
            rm = rowi % SUBLANES
            p1 = jnp.where(rm == 0, prev[0], r1)
            p2 = jnp.where(rm < 2, prev[1], r2)
        return w[3:4] + w[0:1] * p2 + w[1:2] * p1 + w[2:3] * u

    def chunk(c, _):
        ug = _dot(a, wug_ref[c])
        uv = _dot(a, wuv_ref[c])
        w = wcv_ref[c]
        if chain:
            cg = conv(ug, car_s[c], w[0:4])
            cv = conv(uv, car_s[nch + c], w[4:8])
            car_s[c] = ug[tm - SUBLANES:tm]
            car_s[nch + c] = uv[tm - SUBLANES:tm]
            ffn_ref[0, c] = ug[tm - SUBLANES:tm]
            ffn_ref[0, nch + c] = uv[tm - SUBLANES:tm]
        else:
            cg = conv(ug, (p1_ref[c], p2_ref[c]), w[0:4])
            cv = conv(uv, (p1_ref[nch + c], p2_ref[nch + c]), w[4:8])
            ffn_ref[c] = ug
            ffn_ref[nch + c] = uv
        act = (_gelu(cg) * cv).astype(BF16)
        acc_s[...] += _dot(act, wdn_ref[c])
        return 0

    lax.fori_loop(0, nch, chunk, 0)
    h2 = h1 + _rms(acc_s[...], gfo_ref[...])
    ple = _dot(p_ref[...].astype(BF16), wpp_ref[...]) * _sigmoid(_dot(h2.astype(BF16), wpg_ref[...]))
    ho_ref[...] = h2 + _rms(ple, gpl_ref[...])


def _tail(h, ya, yb, p, ffn0, w, *, bsz, t):
    n, d = h.shape
    nch = N_FF_CHUNKS
    cw = FF_CHUNK
    chain = t > SUBLANES
    consts = [w['wo'], w['gmp'], w['gfp'], w['wug'], w['wuv'], w['wcv'], w['wdn'], w['gfo'], w['wpp'],
              w['wpg'], w['gpl']]
    const_specs = [_const(x.shape) for x in consts]
    if chain:
        assert ffn0 is None, "a long sequence starts from an empty ConvFFN buffer"
        tm = min(MATMUL_TILE, t)
        nt = t // tm
        grid = (bsz, nt)
        rows = lambda width: pl.BlockSpec((tm, width), lambda b, i: (b * nt + i, 0))
        ffn_spec = pl.BlockSpec((1, 2 * nch, SUBLANES, cw), lambda b, i: (b, 0, 0, 0))
        ffn_sds = jax.ShapeDtypeStruct((bsz, 2 * nch, SUBLANES, cw), F32)
        extra, extra_specs = [], []
        scratch = [pltpu.VMEM((tm, d), F32), pltpu.VMEM((2 * nch, SUBLANES, cw), F32)]
    else:
        assert t == SUBLANES
        tm = n
        grid = (1, 1)
        rows = lambda width: pl.BlockSpec((tm, width), lambda b, i: (0, 0))
        ffn_spec = _full((2 * nch, tm, cw))
        ffn_sds = jax.ShapeDtypeStruct((2 * nch, tm, cw), F32)
        buf = ffn0.astype(F32).reshape(bsz, 2, 2 * nch, cw).transpose(2, 0, 1, 3)
        zero = jnp.zeros((2 * nch, bsz, SUBLANES - 2, cw), F32)
        p1 = jnp.concatenate([buf[:, :, 1:2], zero, zero[:, :, :1]], axis=2).reshape(2 * nch, tm, cw)
        p2 = jnp.concatenate([buf, zero], axis=2).reshape(2 * nch, tm, cw)
        extra = [p1, p2]
        extra_specs = [_full(p1.shape), _full(p2.shape)]
        scratch = [pltpu.VMEM((tm, d), F32)]
    ho, ffn = pl.pallas_call(
        functools.partial(_tail_kernel, chain=chain),
        grid=grid,
        in_specs=[rows(d), rows(SLAB), rows(SLAB), rows(p.shape[1])] + extra_specs + const_specs,
        out_specs=[rows(d), ffn_spec],
        out_shape=[jax.ShapeDtypeStruct((n, d), F32), ffn_sds],
        scratch_shapes=scratch,
        name="layer_tail")(h, ya, yb, p, *extra, *consts)
    if chain:
        st = ffn[:, :, SUBLANES - 2:, :]
    else:
        st = ffn.reshape(2 * nch, bsz, SUBLANES, cw)[:, :, SUBLANES - 2:, :].transpose(1, 0, 2, 3)
    return ho, st.transpose(0, 2, 1, 3).reshape(bsz, 2, 2 * nch * cw)


def _tail_weights(i, w_out, g_mix_post, g_ffn_pre, g_ffn_post, w_ffn_up, w_ffn_conv, b_ffn_conv,
                  w_ffn_down, w_ple_proj, w_ple_gate, g_ple):
    d = w_out.shape[1]
    nch, cw = N_FF_CHUNKS, FF_CHUNK
    up = w_ffn_up[i].astype(BF16).reshape(d, 2, nch, cw).transpose(1, 2, 0, 3)
    taps = jnp.concatenate([w_ffn_conv[i].astype(F32), b_ffn_conv[i].astype(F32)[None]], axis=0)
    wcv = taps.reshape(4, 2, nch, cw).transpose(2, 1, 0, 3).reshape(nch, 8, cw)
    row = lambda g: g.astype(F32).reshape(1, d)
    return dict(
        wo=w_out.astype(BF16).reshape(2, SLAB, d), gmp=row(g_mix_post[i]), gfp=row(g_ffn_pre[i]),
        wug=up[0], wuv=up[1], wcv=wcv, wdn=w_ffn_down[i].astype(BF16).reshape(nch, cw, d),
        gfo=row(g_ffn_post[i]), wpp=w_ple_proj[i].astype(BF16), wpg=w_ple_gate[i].astype(BF16),
        gpl=row(g_ple[i]))


def kernel(x_prompt, x_sample, p_prompt, p_sample, cache_moba_k, cache_moba_v, state_s5_re, state_s5_im,
           state_hgrn, cache_sb_k, cache_sb_v, state_ffn, page_table,
           g_mix_pre, g_mix_post, g_ffn_pre, g_ffn_post, w_ffn_up, w_ffn_conv, b_ffn_conv, w_ffn_down,
           w_ple_proj, w_ple_gate, g_ple,
           w_in_a, w_out_a, s5_a_re, s5_a_im, s5_log_dt, s5_b_re, s5_b_im, s5_c_re, s5_c_im, s5_d,
           s5_w_glu, s5_b_glu,
           w_in_c, w_out_c, hgrn_lb, g_hgrn_norm):
    depth = g_mix_pre.shape[0]
    d_model = x_prompt.shape[-1]
    n_pages = page_table.shape[1]
    past_len = n_pages * cache_moba_k.shape[2]
    page_table = page_table.astype(jnp.int32)

    layers = []
    for i in range(depth):
        j = i // 2
        lw = {}
        if i % 2 == 0:
            lw['w_in'] = w_in_a[j].astype(BF16)
            lw['disc'] = _s5_discretize(s5_a_re[j], s5_a_im[j], s5_log_dt[j])
            lw['wb'], lw['wc'] = _s5_weights(s5_b_re[j], s5_b_im[j], s5_c_re[j], s5_c_im[j])
            lw['d'] = s5_d[j].astype(F32).reshape(S5_WIDTH)
            lw['wglu'] = s5_w_glu[j].astype(BF16)
            lw['bglu'] = s5_b_glu[j].astype(F32)
            w_out = w_out_a[j]
        else:
            lw['w_in'] = w_in_c[j].astype(BF16)
            w_out = w_out_c[j]
        lw['tail'] = _tail_weights(i, w_out, g_mix_post, g_ffn_pre, g_ffn_post, w_ffn_up, w_ffn_conv,
                                   b_ffn_conv, w_ffn_down, w_ple_proj, w_ple_gate, g_ple)
        layers.append(lw)

    def pool2d(pool, j):
        return pool[j].transpose(0, 2, 3, 1).reshape(pool.shape[1], ATT_WIDTH, pool.shape[2])

    def run(x, p, q_start, s5_re0, s5_im0, hgrn0, ffn0, has_past):
        bsz, t, _ = x.shape
        n = bsz * t
        h = x.astype(F32).reshape(n, d_model)
        tables = _rope_tables(t, q_start)
        if t < MATMUL_TILE:
            tables = tuple(jnp.tile(tb, (n // t, 1)) for tb in tables)
        mk, mv, sr, si, hs, sk, sv, fb = [], [], [], [], [], [], [], []
        long_seq = t % MATMUL_TILE == 0

        def kv_out(x, x_t, which):
            if x_t:
                return x_t[which].reshape(bsz, HEADS, HEAD_DIM, t).transpose(0, 3, 1, 2)
            return x.reshape(bsz, t, HEADS, HEAD_DIM)

        for i in range(depth):
            j = i // 2
            lw = layers[i]
            g_pre = g_mix_pre[i].astype(F32)
            if i % 2 == 0:
                u, q, k, v, *kv_t = _inproj(h, g_pre, lw['w_in'], tables, rope_slabs=(1, 2),
                                            t_slabs=(2, 3) if long_seq else (), seq=(bsz, t))
                y_a, hr, hi = _s5(u, s5_re0[j].astype(F32).reshape(bsz, 1, S5_LANES),
                                  s5_im0[j].astype(F32).reshape(bsz, 1, S5_LANES), lw['disc'], lw['wb'],
                                  lw['wc'], lw['d'], lw['wglu'], lw['bglu'], bsz=bsz, t=t)
                if has_past:
                    y_b = _moba_sample(q, k, v, pool2d(cache_moba_k, j), pool2d(cache_moba_v, j),
                                       page_table, bsz=bsz, t=t)
                else:
                    y_b = _moba_prompt(q, k, v, bsz=bsz, t=t)
                mk.append(kv_out(k, kv_t, 0))
                mv.append(kv_out(v, kv_t, 1))
                sr.append(hr.reshape(bsz, S5_GROUPS, S5_STATE))
                si.append(hi.reshape(bsz, S5_GROUPS, S5_STATE))
            else:
                qh, fh, ih, gh, q, k, v, *kv_t = _inproj(h, g_pre, lw['w_in'], tables, rope_slabs=(),
                                                         t_slabs=(5, 6) if long_seq else (), seq=(bsz, t))
                y_a, s_fin = _hgrn(qh, fh, ih, gh, hgrn_lb.astype(F32), hgrn0[j].astype(F32),
                                   g_hgrn_norm[j].astype(F32), bsz=bsz, t=t, layer=i)
                if has_past:
                    y_b = _sb_sample(q, k, v, pool2d(cache_sb_k, j), pool2d(cache_sb_v, j), page_table,
                                     bsz=bsz, t=t)
                else:
                    y_b = _sb_prompt(q, k, v, bsz=bsz, t=t)
                hs.append(s_fin)
                sk.append(kv_out(k, kv_t, 0))
                sv.append(kv_out(v, kv_t, 1))
            h, buf = _tail(h, y_a, y_b, p[i].astype(F32).reshape(n, p.shape[-1]),
                           None if ffn0 is None else ffn0[i], lw['tail'], bsz=bsz, t=t)
            fb.append(buf)
        return (h.reshape(bsz, t, d_model), jnp.stack(mk), jnp.stack(mv), jnp.stack(sr), jnp.stack(si),
                jnp.stack(hs), jnp.stack(sk), jnp.stack(sv), jnp.stack(fb))

    bp = x_prompt.shape[0]
    n_a = (depth + 1) // 2
    n_c = depth // 2
    outs_p = run(x_prompt, p_prompt, 0,
                 jnp.zeros((n_a, bp, S5_GROUPS, S5_STATE), F32), jnp.zeros((n_a, bp, S5_GROUPS, S5_STATE), F32),
                 jnp.zeros((n_c, bp, HGRN_HEADS, HGRN_DIM, HGRN_DIM), F32), None, False)
    outs_s = run(x_sample, p_sample, past_len, state_s5_re, state_s5_im, state_hgrn, state_ffn, True)
    return (outs_p[0], outs_s[0]) + tuple(outs_p[1:]) + tuple(outs_s[1:])
```

```python
import functools
import math

import jax
import jax.numpy as jnp
from jax import lax
from jax.experimental import pallas as pl
from jax.experimental.pallas import tpu as pltpu

F32 = jnp.float32
BF16 = jnp.bfloat16

S5_WIDTH = 512
S5_GROUPS = 32
S5_GROUP = 16
S5_STATE = 64
S5_LANES = S5_GROUPS * S5_STATE
HEADS = 8
HEAD_DIM = 64
ATT_WIDTH = HEADS * HEAD_DIM
ROPE_DIM = 16
ROPE_THETA = 500000.0
MOBA_BLOCK = 256
MOBA_TOPK = 3
HGRN_HEADS = 4
HGRN_DIM = 128
HGRN_CHUNK = 64
HGRN_SUB = 16
D_FF = 2816
FF_CHUNK = 256
N_FF_CHUNKS = D_FF // FF_CHUNK
PAGE = 128
EPS = 1e-6
NEG = -1e30
SLAB = 512
LANES = 128
SUBLANES = 8
ROW_TILE = 256
MATMUL_TILE = 512
SB_BLOCK = 256
PAGES_PER_STEP = 8
EXP_UNDERFLOW = -104.0


def _dot(a, b):
    return jnp.dot(a, b, preferred_element_type=F32)


def _dot_nt(a, b):
    return lax.dot_general(a, b, (((1,), (1,)), ((), ())), preferred_element_type=F32)


def _dot_tn(a, b):
    return lax.dot_general(a, b, (((0,), (0,)), ((), ())), preferred_element_type=F32)


def _split3(x):
    hi = x.astype(BF16)
    r1 = x - hi.astype(F32)
    mid = r1.astype(BF16)
    lo = (r1 - mid.astype(F32)).astype(BF16)
    return hi, mid, lo


def _rms(x, g):
    return x * lax.rsqrt(jnp.mean(x * x, axis=-1, keepdims=True) + EPS) * g


def _sigmoid(x):
    return 1.0 / (1.0 + jnp.exp(-x))


def _gelu(x):
    return 0.5 * x * (1.0 + jnp.tanh(0.7978845608028654 * (x + 0.044715 * (x * x * x))))


def _full(shape):
    nd = len(shape)
    return pl.BlockSpec(shape, lambda *_: (0,) * nd)


def _const(shape):
    nd = len(shape)
    return pl.BlockSpec(shape, lambda *_: (0,) * nd, pipeline_mode=pl.Buffered(1))


def _rope_table_kernel(c_ref, s1_ref, s2_ref, *, q_start):
    rows, lanes = c_ref.shape
    i = pl.program_id(0)
    lane = lax.broadcasted_iota(jnp.int32, (rows, lanes), 1)
    pos = lax.broadcasted_iota(jnp.int32, (rows, lanes), 0) + i * rows + q_start
    d = lane % HEAD_DIM
    half = ROPE_DIM // 2
    inv = jnp.exp((d % half).astype(F32) * (-math.log(ROPE_THETA) / half))
    ang = pos.astype(F32) * inv
    cs = jnp.cos(ang)
    sn = jnp.sin(ang)
    c_ref[...] = jnp.where(d < ROPE_DIM, cs, 1.0)
    s1_ref[...] = jnp.where(d < half, -sn, 0.0)
    s2_ref[...] = jnp.where((d >= half) & (d < ROPE_DIM), sn, 0.0)


def _rope_tables(t, q_start):
    rows = min(t, 512)
    sds = jax.ShapeDtypeStruct((t, LANES), F32)
    spec = pl.BlockSpec((rows, LANES), lambda i: (i, 0))
    return pl.pallas_call(
        functools.partial(_rope_table_kernel, q_start=q_start),
        grid=(t // rows,), out_shape=(sds, sds, sds), out_specs=(spec, spec, spec),
        name="rope_tables")()


def _inproj_kernel(h_ref, g_ref, w_ref, c_ref, s1_ref, s2_ref, *out_refs, n_slabs, rope_slabs,
                   t_slabs):
    a = _rms(h_ref[...], g_ref[...]).astype(BF16)
    half = ROPE_DIM // 2
    for s in range(n_slabs):
        o_ref = out_refs[s]
        z = _dot(a, w_ref[:, s * SLAB:(s + 1) * SLAB])
        if s in rope_slabs:
            c = c_ref[...]
            s1 = s1_ref[...]
            s2 = s2_ref[...]
            for j in range(SLAB // LANES):
                x = z[:, j * LANES:(j + 1) * LANES]
                o_ref[:, j * LANES:(j + 1) * LANES] = (
                    x * c + pltpu.roll(x, LANES - half, 1) * s1 + pltpu.roll(x, half, 1) * s2)
        else:
            o_ref[...] = z
        if s in t_slabs:
            out_refs[n_slabs + t_slabs.index(s)][...] = o_ref[...].T


def _inproj(h, g, w, tables, rope_slabs, t_slabs=(), seq=None):
    n, d = h.shape
    n_slabs = w.shape[1] // SLAB
    tm = min(MATMUL_TILE, n)
    t_tiles = tables[0].shape[0] // tm
    tab_spec = pl.BlockSpec((tm, LANES), lambda i: (i % t_tiles, 0))
    out_spec = pl.BlockSpec((tm, SLAB), lambda i: (i, 0))
    out_specs = [out_spec] * n_slabs
    out_shape = [jax.ShapeDtypeStruct((n, SLAB), F32)] * n_slabs
    if t_slabs:
        bsz, t = seq
        assert t % tm == 0
        nt = t // tm
        out_specs += [pl.BlockSpec((None, SLAB, tm), lambda i: (i // nt, 0, i % nt))] * len(t_slabs)
        out_shape += [jax.ShapeDtypeStruct((bsz, SLAB, t), F32)] * len(t_slabs)
    return pl.pallas_call(
        functools.partial(_inproj_kernel, n_slabs=n_slabs, rope_slabs=rope_slabs, t_slabs=t_slabs),
        grid=(n // tm,),
        in_specs=[pl.BlockSpec((tm, d), lambda i: (i, 0)), _const((1, d)), _const(w.shape),
                  tab_spec, tab_spec, tab_spec],
        out_specs=out_specs, out_shape=out_shape,
        name="inproj")(h, g.reshape(1, d), w, *tables)


def _s5_disc_kernel(ar_ref, ai_ref, ldt_ref, pre_ref, pim_ref, cc_ref):
    ar = ar_ref[...]
    ai = ai_ref[...]
    dt = jnp.exp(ldt_ref[...])
    row = lax.broadcasted_iota(jnp.int32, (2 * SUBLANES, 1), 0)
    n = jnp.where(row < SUBLANES, row + 1,
                  jnp.where(row == SUBLANES + 1, 2, jnp.where(row == SUBLANES + 2, 4, 1))).astype(F32)
    mag = jnp.exp(ar * dt * n)
    ang = ai * dt * n
    pre = mag * jnp.cos(ang)
    pim = mag * jnp.sin(ang)
    pre_ref[...] = pre
    pim_ref[...] = pim
    abr = pre[0:1]
    abi = pim[0:1]
    den = ar * ar + ai * ai
    nr = abr - 1.0
    cc_ref[0:1, :] = (nr * ar + abi * ai) / den
    cc_ref[1:2, :] = (abi * ar - nr * ai) / den


def _s5_discretize(a_re, a_im, log_dt):
    ar = a_re.reshape(1, S5_LANES)
    ai = a_im.reshape(1, S5_LANES)
    ldt = jnp.repeat(log_dt, S5_STATE).reshape(1, S5_LANES)
    return pl.pallas_call(
        _s5_disc_kernel,
        out_shape=(jax.ShapeDtypeStruct((2 * SUBLANES, S5_LANES), F32),
                   jax.ShapeDtypeStruct((2 * SUBLANES, S5_LANES), F32),
                   jax.ShapeDtypeStruct((2, S5_LANES), F32)),
        name="s5_discretize")(ar, ai, ldt)


SCAN_LANES = 512


def _s5_kernel(u_ref, h0r_ref, h0i_ref, pre_ref, pim_ref, cc_ref, wb_ref, wc_ref, d_ref,
               wglu_ref, bglu_ref, y_ref, hr_out_ref, hi_out_ref, xr_s, xi_s, car_s, *, chain):
    tm = u_ref.shape[0]
    nb = tm // SUBLANES
    u = u_ref[...]
    ub = u.astype(BF16)
    half_in = S5_WIDTH // 2
    half_st = S5_LANES // 2
    for hf in range(2):
        bu = _dot(ub[:, hf * half_in:(hf + 1) * half_in], wb_ref[hf])
        bur = bu[:, :half_st]
        bui = bu[:, half_st:]
        ls = slice(hf * half_st, (hf + 1) * half_st)
        cr = cc_ref[0:1, ls]
        ci = cc_ref[1:2, ls]
        xr_s[:, ls] = cr * bur - ci * bui
        xi_s[:, ls] = cr * bui + ci * bur

    if chain:
        t = pl.program_id(1)

        @pl.when(t == 0)
        def _():
            car_s[0:1, :] = h0r_ref[0]
            car_s[1:2, :] = h0i_ref[0]

    rowi = lax.broadcasted_iota(jnp.int32, (SUBLANES, SCAN_LANES), 0)
    for lc in range(S5_LANES // SCAN_LANES):
        ls = slice(lc * SCAN_LANES, (lc + 1) * SCAN_LANES)
        pre = pre_ref[:, ls]
        pim = pim_ref[:, ls]

        def body(i, carry, ls=ls, pre=pre, pim=pim):
            r0 = pl.multiple_of(i * SUBLANES, SUBLANES)
            hr = xr_s[pl.ds(r0, SUBLANES), ls]
            hi = xi_s[pl.ds(r0, SUBLANES), ls]
            for k, prow in ((1, SUBLANES), (2, SUBLANES + 1), (4, SUBLANES + 2)):
                ar = pre[prow:prow + 1]
                ai = pim[prow:prow + 1]
                sr = jnp.where(rowi >= k, pltpu.roll(hr, k, 0), 0.0)
                si = jnp.where(rowi >= k, pltpu.roll(hi, k, 0), 0.0)
                hr, hi = hr + ar * sr - ai * si, hi + ar * si + ai * sr
            if chain:
                cr_, ci_ = carry
            else:
                cr_ = h0r_ref[i, :, ls]
                ci_ = h0i_ref[i, :, ls]
            pr8 = pre[0:SUBLANES]
            pi8 = pim[0:SUBLANES]
            hr, hi = hr + pr8 * cr_ - pi8 * ci_, hi + pr8 * ci_ + pi8 * cr_
            xr_s[pl.ds(r0, SUBLANES), ls] = hr
            xi_s[pl.ds(r0, SUBLANES), ls] = hi
            last_r = hr[SUBLANES - 1:SUBLANES]
            last_i = hi[SUBLANES - 1:SUBLANES]
            if chain:
                return last_r, last_i
            hr_out_ref[i, :, ls] = last_r
            hi_out_ref[i, :, ls] = last_i
            return carry

        if chain:
            cr_, ci_ = lax.fori_loop(0, nb, body, (car_s[0:1, ls], car_s[1:2, ls]))
            car_s[0:1, ls] = cr_
            car_s[1:2, ls] = ci_
        else:
            lax.fori_loop(0, nb, body, 0)

    if chain:
        hr_out_ref[0] = car_s[0:1, :]
        hi_out_ref[0] = car_s[1:2, :]

    ys = []
    for hf in range(2):
        ls = slice(hf * half_st, (hf + 1) * half_st)
        hcat = jnp.concatenate([xr_s[:, ls], xi_s[:, ls]], axis=1).astype(BF16)
        ys.append(_dot(hcat, wc_ref[hf]))
    y = jnp.concatenate(ys, axis=1) + d_ref[...] * u
    y = _gelu(y)
    y_ref[...] = y * _sigmoid(_dot(y.astype(BF16), wglu_ref[...]) + bglu_ref[...])


def _s5_weights(b_re, b_im, c_re, c_im):
    gh = S5_GROUPS // 2
    eye = jnp.eye(gh, dtype=F32)

    def bmat(b):
        bt = b.astype(F32).transpose(0, 2, 1).reshape(2, gh, S5_GROUP, S5_STATE)
        return jnp.einsum('xghp,gk->xghkp', bt, eye).reshape(2, gh * S5_GROUP, gh * S5_STATE)

    def cmat(c):
        ct = c.astype(F32).transpose(0, 2, 1).reshape(2, gh, S5_STATE, S5_GROUP)
        return jnp.einsum('xgph,gk->xgpkh', ct, eye).reshape(2, gh * S5_STATE, gh * S5_GROUP)

    wb = jnp.concatenate([bmat(b_re), bmat(b_im)], axis=2).astype(BF16)
    wc = jnp.concatenate([cmat(c_re), -cmat(c_im)], axis=1).astype(BF16)
    return wb, wc


def _s5(u, h0r, h0i, disc, wb, wc, dvec, wglu, bglu, *, bsz, t):
    n = bsz * t
    pre, pim, cc = disc
    chain = t > SUBLANES
    if chain:
        tm = min(ROW_TILE, t)
        grid = (bsz, t // tm)
        row_spec = pl.BlockSpec((tm, S5_WIDTH), lambda b, i: (b * (t // tm) + i, 0))
        st_spec = pl.BlockSpec((1, 1, S5_LANES), lambda b, i: (b, 0, 0))
    else:
        assert t == SUBLANES
        tm = n
        grid = (1, 1)
        row_spec = pl.BlockSpec((tm, S5_WIDTH), lambda b, i: (0, 0))
        st_spec = pl.BlockSpec((bsz, 1, S5_LANES), lambda b, i: (0, 0, 0))
    st_sds = jax.ShapeDtypeStruct((bsz, 1, S5_LANES), F32)
    return pl.pallas_call(
        functools.partial(_s5_kernel, chain=chain),
        grid=grid,
        in_specs=[row_spec, st_spec, st_spec, _const(pre.shape), _const(pim.shape), _const(cc.shape),
                  _const(wb.shape), _const(wc.shape), _const((1, S5_WIDTH)), _const(wglu.shape),
                  _const((1, S5_WIDTH))],
        out_specs=[row_spec, st_spec, st_spec],
        out_shape=[jax.ShapeDtypeStruct((n, S5_WIDTH), F32), st_sds, st_sds],
        scratch_shapes=[pltpu.VMEM((tm, S5_LANES), F32), pltpu.VMEM((tm, S5_LANES), F32),
                        pltpu.VMEM((2, S5_LANES), F32)],
        name="s5_mixer")(u, h0r, h0i, pre, pim, cc, wb, wc, dvec.reshape(1, S5_WIDTH), wglu,
                         bglu.reshape(1, S5_WIDTH))


def _top_blocks(gate, n_past, col):
    g = jnp.where(col < n_past, gate, NEG)
    sel = jnp.zeros(gate.shape, F32)
    colf = col.astype(F32)
    for _ in range(MOBA_TOPK):
        m = jnp.max(g, axis=-1, keepdims=True)
        first = jnp.min(jnp.where(g == m, colf, float(LANES)), axis=-1, keepdims=True)
        pick = colf == first
        sel = jnp.where(pick & (m > 0.5 * NEG), 1.0, sel)
        g = jnp.where(pick, -3e38, g)
    return sel


def _top_blocks_t(gate, n_past):
    rowf = lax.broadcasted_iota(jnp.int32, gate.shape, 0).astype(F32)
    g = jnp.where(rowf < n_past, gate, NEG)
    sel = jnp.zeros(gate.shape, F32)
    for _ in range(MOBA_TOPK):
        m = jnp.max(g, axis=0, keepdims=True)
        first = jnp.min(jnp.where(g == m, rowf, float(LANES)), axis=0, keepdims=True)
        pick = rowf == first
        sel = jnp.where(pick & (m > 0.5 * NEG), 1.0, sel)
        g = jnp.where(pick, -3e38, g)
    return sel


def _moba_prompt_kernel(q_ref, k_ref, v_ref, o_ref, kb_s, vt_s, kmean_s, sel_s, *, nblk):
    bs = MOBA_BLOCK
    i = pl.program_id(2)

    @pl.when(i == 0)
    def _():
        kmean_s[...] = jnp.zeros(kmean_s.shape, F32)
        for j in range(nblk):
            kj = k_ref[j * bs:(j + 1) * bs, :]
            kb_s[j * bs:(j + 1) * bs, :] = kj.astype(BF16)
            kmean_s[j:j + 1, :] = jnp.sum(kj, axis=0, keepdims=True) * (1.0 / bs)
            vt_s[j] = v_ref[j * bs:(j + 1) * bs, :].T.astype(BF16)

    lane = lax.broadcasted_iota(jnp.int32, (1, LANES), 1)
    rowd = lax.broadcasted_iota(jnp.int32, (LANES, 1), 0)
    causal = (lax.broadcasted_iota(jnp.int32, (bs, bs), 0)
              <= lax.broadcasted_iota(jnp.int32, (bs, bs), 1))
    kmb = kmean_s[...].astype(BF16)
    qf = q_ref[...]
    r0 = pl.multiple_of(i * bs, bs)
    ki = kb_s[pl.ds(r0, bs), :]
    vti = vt_s[i]
    qss = []
    carry = []
    for hd in range(2):
        lm = (lane >= HEAD_DIM * hd) & (lane < HEAD_DIM * (hd + 1))
        qh = jnp.where(lm, qf, 0.0)
        qs = (qh * HEAD_DIM ** -0.5).astype(BF16)
        sel = _top_blocks_t(_dot_nt(kmb, qh.astype(BF16)), i)
        for b in range(nblk):
            sel_s[hd, b] = sel[b:b + 1, :]
        s = jnp.where(causal, _dot_nt(ki, qs), NEG)
        m = jnp.max(s, axis=0, keepdims=True)
        p = jnp.exp(s - m)
        carry += [m, jnp.sum(p, axis=0, keepdims=True), _dot(vti, p.astype(BF16))]
        qss.append(qs)

    def kvpair(jj, carry):
        c0 = pl.multiple_of(jj * 2 * bs, 2 * bs)
        kw = kb_s[pl.ds(c0, 2 * bs), :]
        out = []
        for hd in range(2):
            m, l, acc = carry[3 * hd:3 * hd + 3]
            s = _dot_nt(kw, qss[hd])
            s0 = jnp.where(sel_s[hd, 2 * jj] > 0.5, s[:bs], NEG)
            s1 = jnp.where(sel_s[hd, 2 * jj + 1] > 0.5, s[bs:], NEG)
            m_new = jnp.maximum(m, jnp.maximum(jnp.max(s0, axis=0, keepdims=True),
                                               jnp.max(s1, axis=0, keepdims=True)))
            alpha = jnp.exp(m - m_new)
            p0 = jnp.exp(s0 - m_new)
            p1 = jnp.exp(s1 - m_new)
            l = alpha * l + jnp.sum(p0, axis=0, keepdims=True) + jnp.sum(p1, axis=0, keepdims=True)
            acc = (alpha * acc + _dot(vt_s[2 * jj], p0.astype(BF16))
                   + _dot(vt_s[2 * jj + 1], p1.astype(BF16)))
            out += [m_new, l, acc]
        return tuple(out)

    carry = lax.fori_loop(0, (i + 1) // 2, kvpair, tuple(carry))
    out_t = jnp.where(rowd < HEAD_DIM, carry[2] / carry[1], carry[5] / carry[4])
    o_ref[...] = out_t.T


def _moba_prompt(q, k, v, *, bsz, t):
    nblk = t // MOBA_BLOCK
    assert t % MOBA_BLOCK == 0 and nblk % 2 == 0 and nblk <= LANES
    nblk_rows = -(-nblk // SUBLANES) * SUBLANES
    n = bsz * t
    q_spec = pl.BlockSpec((MOBA_BLOCK, LANES), lambda b, hp, i: (b * nblk + i, hp))
    kv_spec = pl.BlockSpec((t, LANES), lambda b, hp, i: (b, hp))
    return pl.pallas_call(
        functools.partial(_moba_prompt_kernel, nblk=nblk),
        grid=(bsz, ATT_WIDTH // LANES, nblk),
        in_specs=[q_spec, kv_spec, kv_spec], out_specs=q_spec,
        out_shape=jax.ShapeDtypeStruct((n, ATT_WIDTH), F32),
        scratch_shapes=[pltpu.VMEM((t, LANES), BF16),
                        pltpu.VMEM((nblk, LANES, MOBA_BLOCK), BF16),
                        pltpu.VMEM((nblk_rows, LANES), F32),
                        pltpu.VMEM((2, nblk, 1, MOBA_BLOCK), F32)],
        name="moba_prompt")(q, k, v)


def _expand_heads(q, qexp_s):
    t = q.shape[0]
    lane = lax.broadcasted_iota(jnp.int32, (1, ATT_WIDTH), 1)
    for h in range(HEADS):
        lm = (lane >= h * HEAD_DIM) & (lane < (h + 1) * HEAD_DIM)
        qexp_s[h * t:(h + 1) * t, :] = jnp.where(lm, q, 0.0)


def _collapse_heads(acc, t):
    lane = lax.broadcasted_iota(jnp.int32, (1, ATT_WIDTH), 1)
    out = jnp.zeros((t, ATT_WIDTH), F32)
    for h in range(HEADS):
        lm = (lane >= h * HEAD_DIM) & (lane < (h + 1) * HEAD_DIM)
        out = out + jnp.where(lm, acc[h * t:(h + 1) * t, :], 0.0)
    return out


def _pad_rows(x_ref, new_s):
    new_s[...] = jnp.zeros(new_s.shape, F32)
    new_s[0:x_ref.shape[0], :] = x_ref[...]
    return new_s[...].astype(BF16)


def _moba_sample_kernel(pt_ref, q_ref, kn_ref, vn_ref, *rest, n_pages, t):
    pps = PAGES_PER_STEP
    k_refs = rest[:pps]
    v_refs = rest[pps:2 * pps]
    o_ref = rest[2 * pps]
    qexp_s, new_s, s_all, gate_s, bmax_s, sel_s, m_s, l_s, acc_s = rest[2 * pps + 1:]
    ns = n_pages // pps
    s_id = pl.program_id(1)
    rows = HEADS * t
    col = lax.broadcasted_iota(jnp.int32, (rows, LANES), 1)
    scale = HEAD_DIM ** -0.5
    pages_per_block = MOBA_BLOCK // PAGE

    @pl.when(s_id == 0)
    def _():
        _expand_heads(q_ref[...], qexp_s)
        gate_s[...] = jnp.zeros(gate_s.shape, F32)
        bmax_s[...] = jnp.full(bmax_s.shape, NEG, F32)

    @pl.when(s_id < ns)
    def _():
        qe = qexp_s[...].astype(BF16)
        g = gate_s[...]
        bm = bmax_s[...]
        for i in range(pps):
            pg = s_id * pps + i
            s = _dot(qe, k_refs[i][...].astype(BF16))
            s_all[pg] = s
            mine = col == pg // pages_per_block
            g = g + jnp.where(mine, jnp.sum(s, axis=-1, keepdims=True) * (1.0 / MOBA_BLOCK), 0.0)
            bm = jnp.where(mine, jnp.maximum(bm, jnp.max(s, axis=-1, keepdims=True)), bm)
        gate_s[...] = g
        bmax_s[...] = bm

    @pl.when(s_id == ns)
    def _():
        sel = _top_blocks(gate_s[...], n_pages // pages_per_block, col)
        sel_s[...] = sel
        qe = qexp_s[...].astype(BF16)
        sn = _dot_nt(qe, _pad_rows(kn_ref, new_s)) * scale
        rowt = lax.broadcasted_iota(jnp.int32, (rows, LANES), 0) % t
        sn = jnp.where(col <= rowt, sn, NEG)
        m = jnp.maximum(jnp.max(sn, axis=-1, keepdims=True),
                        jnp.max(jnp.where(sel > 0.5, bmax_s[...] * scale, NEG), axis=-1, keepdims=True))
        pn = jnp.exp(sn - m)
        m_s[...] = jnp.broadcast_to(m, m_s.shape)
        l_s[...] = jnp.broadcast_to(jnp.sum(pn, axis=-1, keepdims=True), l_s.shape)
        acc_s[...] = _dot(pn.astype(BF16), _pad_rows(vn_ref, new_s))

    @pl.when(s_id >= ns)
    def _():
        sel = sel_s[...]
        m = m_s[:, 0:1]
        l = l_s[:, 0:1]
        acc = acc_s[...]
        for i in range(pps):
            pg = (s_id - ns) * pps + i
            selc = jnp.sum(jnp.where(col == pg // pages_per_block, sel, 0.0), axis=-1, keepdims=True)
            p = jnp.exp(jnp.where(selc > 0.5, s_all[pg] * scale, NEG) - m)
            l = l + jnp.sum(p, axis=-1, keepdims=True)
            acc = acc + _dot_nt(p.astype(BF16), v_refs[i][...].astype(BF16))
        l_s[...] = jnp.broadcast_to(l, l_s.shape)
        acc_s[...] = acc

    @pl.when(s_id == 2 * ns - 1)
    def _():
        o_ref[...] = _collapse_heads(acc_s[...] / l_s[:, 0:1], t)


def _sample_specs(page_table, n_pages, t, page_index):
    pps = PAGES_PER_STEP
    row_spec = pl.BlockSpec((t, ATT_WIDTH), lambda b, s, pt: (b, 0))

    def page_spec(i, which):
        return pl.BlockSpec((None, ATT_WIDTH, PAGE),
                            lambda b, s, pt: (pt[b, page_index(s, i, which)], 0, 0))

    in_specs = ([row_spec] * 3 + [page_spec(i, 0) for i in range(pps)]
                + [page_spec(i, 1) for i in range(pps)])
    return in_specs, row_spec


def _moba_sample(q, k_new, v_new, k_pool, v_pool, page_table, *, bsz, t):
    n_pages = page_table.shape[1]
    pps = PAGES_PER_STEP
    assert n_pages % pps == 0 and (n_pages * PAGE) % MOBA_BLOCK == 0 and t <= SUBLANES
    assert n_pages * PAGE // MOBA_BLOCK <= LANES
    ns = n_pages // pps
    rows = HEADS * t

    def page_index(s, i, which):
        step = jnp.minimum(s, ns - 1) if which == 0 else jnp.maximum(s - ns, 0)
        return step * pps + i

    in_specs, out_spec = _sample_specs(page_table, n_pages, t, page_index)
    grid_spec = pltpu.PrefetchScalarGridSpec(
        num_scalar_prefetch=1, grid=(bsz, 2 * ns), in_specs=in_specs, out_specs=out_spec,
        scratch_shapes=[pltpu.VMEM((rows, ATT_WIDTH), F32),
                        pltpu.VMEM((PAGE, ATT_WIDTH), F32),
                        pltpu.VMEM((n_pages, rows, PAGE), F32)]
                       + [pltpu.VMEM((rows, LANES), F32)] * 5
                       + [pltpu.VMEM((rows, ATT_WIDTH), F32)])
    return pl.pallas_call(
        functools.partial(_moba_sample_kernel, n_pages=n_pages, t=t),
        grid_spec=grid_spec,
        out_shape=jax.ShapeDtypeStruct((bsz * t, ATT_WIDTH), F32),
        name="moba_sample")(page_table, q, k_new, v_new, *([k_pool] * pps), *([v_pool] * pps))


def _sb_weights(z, r, tri, strict):
    n = z.shape[0]
    lg = jnp.log(1.0 + jnp.exp(-jnp.abs(z)))
    log_beta = jnp.minimum(z, 0.0) - lg
    log_keep = jnp.minimum(-z, 0.0) - lg
    if strict is not None:
        log_beta = jnp.where(strict, log_beta, NEG)
        log_keep = jnp.where(strict, log_keep, 0.0)
    st = _dot(jnp.concatenate(_split3(log_keep), axis=0), tri)
    later = st[:n] + st[n:2 * n] + st[2 * n:]
    w = jnp.exp(log_beta + later + r)
    return w.astype(BF16), r + jnp.sum(log_keep, axis=-1, keepdims=True)


def _suffix_matrix(n):
    return (lax.broadcasted_iota(jnp.int32, (n, n), 0)
            > lax.broadcasted_iota(jnp.int32, (n, n), 1)).astype(BF16)


def _sb_prompt_kernel(q_ref, k_ref, v_ref, o_ref, kb_s, vb_s):
    bs = SB_BLOCK
    i = pl.program_id(2)

    @pl.when(i == 0)
    def _():
        for j in range(k_ref.shape[0] // bs):
            kb_s[j * bs:(j + 1) * bs, :] = k_ref[j * bs:(j + 1) * bs, :].astype(BF16)
            vb_s[j * bs:(j + 1) * bs, :] = v_ref[j * bs:(j + 1) * bs, :].astype(BF16)

    lane = lax.broadcasted_iota(jnp.int32, (1, LANES), 1)
    tri = _suffix_matrix(bs)
    strict = (lax.broadcasted_iota(jnp.int32, (bs, bs), 1)
              < lax.broadcasted_iota(jnp.int32, (bs, bs), 0))
    qf = q_ref[...]
    qss = []
    for hd in range(2):
        lm = (lane >= HEAD_DIM * hd) & (lane < HEAD_DIM * (hd + 1))
        qss.append((jnp.where(lm, qf, 0.0) * HEAD_DIM ** -0.5).astype(BF16))

    def block(j, rs, mask):
        c0 = pl.multiple_of(j * bs, bs)
        kj = kb_s[pl.ds(c0, bs), :]
        vj = vb_s[pl.ds(c0, bs), :]
        out = []
        for hd in range(2):
            w, r = _sb_weights(_dot_nt(qss[hd], kj), rs[hd], tri, mask)
            out += [_dot(w, vj), r]
        return out

    zero = jnp.zeros((bs, 1), F32)
    a0, r0, a1, r1 = block(i, (zero, zero), strict)
    has_prev = lax.broadcasted_iota(jnp.int32, (bs, bs), 0) < jnp.where(i > 0, bs, 0)
    d0, r0, d1, r1 = block(jnp.maximum(i - 1, 0), (r0, r1), has_prev)
    a0 = a0 + d0
    a1 = a1 + d1

    def cond(c):
        j, r0, r1 = c[:3]
        return (j >= 0) & (jnp.maximum(jnp.max(r0), jnp.max(r1)) > EXP_UNDERFLOW)

    def body(c):
        j, r0, r1, a0, a1 = c
        d0, r0, d1, r1 = block(j, (r0, r1), None)
        return j - 1, r0, r1, a0 + d0, a1 + d1

    _, _, _, a0, a1 = lax.while_loop(cond, body, (i - 2, r0, r1, a0, a1))
    o_ref[...] = jnp.where(lane < HEAD_DIM, a0, a1)


def _sb_prompt(q, k, v, *, bsz, t):
    assert t % SB_BLOCK == 0
    n = bsz * t
    nq = t // SB_BLOCK
    q_spec = pl.BlockSpec((SB_BLOCK, LANES), lambda b, hp, i: (b * nq + i, hp))
    kv_spec = pl.BlockSpec((t, LANES), lambda b, hp, i: (b, hp))
    return pl.pallas_call(
        _sb_prompt_kernel,
        grid=(bsz, ATT_WIDTH // LANES, nq),
        in_specs=[q_spec, kv_spec, kv_spec], out_specs=q_spec,
        out_shape=jax.ShapeDtypeStruct((n, ATT_WIDTH), F32),
        scratch_shapes=[pltpu.VMEM((t, LANES), BF16), pltpu.VMEM((t, LANES), BF16)],
        name="sb_prompt")(q, k, v)


def _sb_sample_kernel(pt_ref, q_ref, a_ref, b_ref, *rest, t, first):
    pps = PAGES_PER_STEP
    k_refs = rest[:pps]
    v_refs = rest[pps:2 * pps]
    n_out = 3 if first else 1
    o_ref = rest[2 * pps]
    qexp_s, new_s, r_s, acc_s = rest[2 * pps + n_out:]
    s_id = pl.program_id(1)
    rows = HEADS * t
    tri = _suffix_matrix(PAGE)

    @pl.when(s_id == 0)
    def _():
        _expand_heads(q_ref[...] * HEAD_DIM ** -0.5, qexp_s)
        if first:
            z = _dot_nt(qexp_s[...].astype(BF16), _pad_rows(a_ref, new_s))
            col = lax.broadcasted_iota(jnp.int32, (rows, PAGE), 1)
            rowt = lax.broadcasted_iota(jnp.int32, (rows, PAGE), 0) % t
            w, r = _sb_weights(z, jnp.zeros((rows, 1), F32), tri, col < rowt)
            acc_s[...] = _dot(w, _pad_rows(b_ref, new_s))
            r_s[...] = jnp.broadcast_to(r, r_s.shape)
        else:
            acc_s[...] = a_ref[...]
            r_s[...] = b_ref[...]

    for i in range(pps):
        @pl.when(jnp.max(r_s[...]) > EXP_UNDERFLOW)
        def _(i=i):
            z = _dot(qexp_s[...].astype(BF16), k_refs[i][...].astype(BF16))
            w, r = _sb_weights(z, r_s[:, 0:1], tri, None)
            acc_s[...] = acc_s[...] + _dot_nt(w, v_refs[i][...].astype(BF16))
            r_s[...] = jnp.broadcast_to(r, r_s.shape)

    @pl.when(s_id == pl.num_programs(1) - 1)
    def _():
        o_ref[...] = _collapse_heads(acc_s[...], t)
        if first:
            rest[2 * pps + 1][...] = acc_s[...]
            rest[2 * pps + 2][...] = r_s[...]


def _sb_sample(q, k_new, v_new, k_pool, v_pool, page_table, *, bsz, t):
    n_pages = page_table.shape[1]
    pps = PAGES_PER_STEP
    assert n_pages % pps == 0 and t <= SUBLANES
    rows = HEADS * t
    n = bsz * t
    scratch = [pltpu.VMEM((rows, ATT_WIDTH), F32), pltpu.VMEM((PAGE, ATT_WIDTH), F32),
               pltpu.VMEM((rows, LANES), F32), pltpu.VMEM((rows, ATT_WIDTH), F32)]
    acc_spec = pl.BlockSpec((rows, ATT_WIDTH), lambda b, s, pt: (b, 0))
    r_spec = pl.BlockSpec((rows, LANES), lambda b, s, pt: (b, 0))
    y_sds = jax.ShapeDtypeStruct((n, ATT_WIDTH), F32)
    pools = [k_pool] * pps + [v_pool] * pps

    in_specs, y_spec = _sample_specs(page_table, n_pages, t, lambda s, i, which: n_pages - 1 - i)
    y, acc, r = pl.pallas_call(
        functools.partial(_sb_sample_kernel, t=t, first=True),
        grid_spec=pltpu.PrefetchScalarGridSpec(
            num_scalar_prefetch=1, grid=(bsz, 1), in_specs=in_specs,
            out_specs=[y_spec, acc_spec, r_spec], scratch_shapes=scratch),
        out_shape=[y_sds, jax.ShapeDtypeStruct((bsz * rows, ATT_WIDTH), F32),
                   jax.ShapeDtypeStruct((bsz * rows, LANES), F32)],
        name="sb_sample")(page_table, q, k_new, v_new, *pools)
    if n_pages == pps:
        return y

    def older_pages():
        specs, _ = _sample_specs(page_table, n_pages, t,
                                 lambda s, i, which: n_pages - 1 - pps - (s * pps + i))
        specs = [specs[0], acc_spec, r_spec] + specs[3:]
        return pl.pallas_call(
            functools.partial(_sb_sample_kernel, t=t, first=False),
            grid_spec=pltpu.PrefetchScalarGridSpec(
                num_scalar_prefetch=1, grid=(bsz, n_pages // pps - 1), in_specs=specs,
                out_specs=y_spec, scratch_shapes=scratch),
            out_shape=y_sds,
            name="sb_sample_older")(page_table, q, acc, r, *pools)

    return lax.cond(jnp.max(r) > EXP_UNDERFLOW, older_pages, lambda: y)


def _hgrn_kernel(q_ref, f_ref, i_ref, g_ref, lb_ref, s0_ref, gn_ref, o_ref, s_out_ref, st_s,
                 *, layer):
    tb = q_ref.shape[0]
    t = pl.program_id(1)

    @pl.when(t == 0)
    def _():
        for hh in range(HGRN_HEADS):
            st_s[hh] = s0_ref[0, hh].T

    lbp = lb_ref[...]
    e = jnp.exp(lbp - jnp.max(lbp, axis=0, keepdims=True))
    soft = e / jnp.sum(e, axis=0, keepdims=True)
    lbv = jnp.sum(soft[1:layer + 1], axis=0, keepdims=True)

    pad = max(HGRN_SUB - tb, 0)
    L = min(HGRN_CHUNK, tb + pad)
    c = min(HGRN_SUB, L)
    tri = (lax.broadcasted_iota(jnp.int32, (L, L), 0)
           >= lax.broadcasted_iota(jnp.int32, (L, L), 1)).astype(BF16)
    rowc = lax.broadcasted_iota(jnp.int32, (c, 1), 0)
    gn = gn_ref[...]

    def padrows(x):
        if pad == 0:
            return x
        return jnp.concatenate([x, jnp.zeros((pad, x.shape[1]), F32)], axis=0)

    def chunk(rows, n_valid):
        for hh in range(HGRN_HEADS):
            ls = slice(hh * HGRN_DIM, (hh + 1) * HGRN_DIM)
            lbh = lbv[:, ls]
            f = lbh + (1.0 - lbh) * _sigmoid(f_ref[rows, ls])
            lf = padrows(jnp.log(f))
            kk = padrows(1.0 - f)
            qv = padrows(q_ref[rows, ls])
            vv = padrows(i_ref[rows, ls])
            hi, mid, lo = _split3(lf)
            b = _dot(tri, hi) + _dot(tri, mid) + _dot(tri, lo)
            st = st_s[hh]
            o_inter = _dot_nt((qv * jnp.exp(b)).astype(BF16), st.astype(BF16))
            vb = vv.astype(BF16)
            parts = []
            for si in range(L // c):
                rs = slice(si * c, (si + 1) * c)
                b_i = b[rs]
                q_i = qv[rs]
                k_i = kk[rs]
                v_i = vv[rs]
                o_i = o_inter[rs]
                if si > 0:
                    b_prev = b[si * c - 1:si * c]
                    qt = (q_i * jnp.exp(b_i - b_prev)).astype(BF16)
                    kt = (kk[:si * c] * jnp.exp(b_prev - b[:si * c])).astype(BF16)
                    o_i = o_i + _dot(_dot_nt(qt, kt).astype(BF16), vb[:si * c])
                for s in range(min(c, max(n_valid - si * c, 0))):
                    dec = jnp.exp(jnp.minimum(b_i - b_i[s:s + 1], 0.0))
                    a = jnp.sum(q_i * k_i[s:s + 1] * dec, axis=-1, keepdims=True)
                    o_i = o_i + jnp.where(rowc >= s, a, 0.0) * v_i[s:s + 1]
                parts.append(o_i)
            o = jnp.concatenate(parts, axis=0) if len(parts) > 1 else parts[0]
            o = o[:n_valid]
            o = o * lax.rsqrt(jnp.mean(o * o, axis=-1, keepdims=True) + EPS) * gn
            gv = g_ref[rows, ls]
            o_ref[rows, ls] = o * (gv * _sigmoid(gv))
            b_last = b[L - 1:L]
            kdec = (kk * jnp.exp(b_last - b)).astype(BF16)
            st_s[hh] = st * jnp.exp(b_last) + _dot_tn(vb, kdec)

    if pad:
        chunk(slice(0, tb), tb)
    else:
        def body(ch, _):
            chunk(pl.ds(pl.multiple_of(ch * L, L), L), L)
            return 0

        lax.fori_loop(0, tb // L, body, 0)

    for hh in range(HGRN_HEADS):
        s_out_ref[0, hh] = st_s[hh].T


def _hgrn(qh, fh, ih, gh, lb, s0, gnorm, *, bsz, t, layer):
    n = bsz * t
    tb = min(ROW_TILE, t)
    nt = t // tb
    row_spec = pl.BlockSpec((tb, SLAB), lambda b, i: (b * nt + i, 0))
    st_spec = pl.BlockSpec((1, HGRN_HEADS, HGRN_DIM, HGRN_DIM), lambda b, i: (b, 0, 0, 0))
    return pl.pallas_call(
        functools.partial(_hgrn_kernel, layer=layer),
        grid=(bsz, nt),
        in_specs=[row_spec] * 4 + [_const(lb.shape), st_spec, _const((1, HGRN_DIM))],
        out_specs=[row_spec, st_spec],
        out_shape=[jax.ShapeDtypeStruct((n, SLAB), F32),
                   jax.ShapeDtypeStruct((bsz, HGRN_HEADS, HGRN_DIM, HGRN_DIM), F32)],
        scratch_shapes=[pltpu.VMEM((HGRN_HEADS, HGRN_DIM, HGRN_DIM), F32)],
        name="hgrn2")(qh, fh, ih, gh, lb, s0, gnorm.reshape(1, HGRN_DIM))


def _tail_kernel(*refs, chain):
    if chain:
        (h_ref, ya_ref, yb_ref, p_ref, wo_ref, gmp_ref, gfp_ref, wug_ref, wuv_ref, wcv_ref, wdn_ref,
         gfo_ref, wpp_ref, wpg_ref, gpl_ref, ho_ref, ffn_ref, acc_s, car_s) = refs
    else:
        (h_ref, ya_ref, yb_ref, p_ref, p1_ref, p2_ref, wo_ref, gmp_ref, gfp_ref, wug_ref, wuv_ref,
         wcv_ref, wdn_ref, gfo_ref, wpp_ref, wpg_ref, gpl_ref, ho_ref, ffn_ref, acc_s) = refs
    tm = h_ref.shape[0]
    nch = wug_ref.shape[0]
    cw = wug_ref.shape[2]
    mix = _dot(ya_ref[...].astype(BF16), wo_ref[0]) + _dot(yb_ref[...].astype(BF16), wo_ref[1])
    h1 = h_ref[...] + _rms(mix, gmp_ref[...])
    a = _rms(h1, gfp_ref[...]).astype(BF16)
    acc_s[...] = jnp.zeros(acc_s.shape, F32)
    rowi = lax.broadcasted_iota(jnp.int32, (tm, cw), 0)

    if chain:
        @pl.when(pl.program_id(1) == 0)
        def _():
            car_s[...] = jnp.zeros(car_s.shape, F32)

    def conv(u, prev, w):
        r1 = pltpu.roll(u, 1, 0)
        r2 = pltpu.roll(u, 2, 0)
        if chain:
            last = prev[SUBLANES - 1:SUBLANES]
            last2 = prev[SUBLANES - 2:SUBLANES - 1]
            p1 = jnp.where(rowi == 0, last, r1)
            p2 = jnp.where(rowi == 0, last2, jnp.where(rowi == 1, last, r2))
        else:
            rm = rowi % SUBLANES
            p1 = jnp.where(rm == 0, prev[0], r1)
            p2 = jnp.where(rm < 2, prev[1], r2)
        return w[3:4] + w[0:1] * p2 + w[1:2] * p1 + w[2:3] * u

    def chunk(c, _):
        ug = _dot(a, wug_ref[c])
        uv = _dot(a, wuv_ref[c])
        w = wcv_ref[c]
        if chain:
            cg = conv(ug, car_s[c], w[0:4])
            cv = conv(uv, car_s[nch + c], w[4:8])
            car_s[c] = ug[tm - SUBLANES:tm]
            car_s[nch + c] = uv[tm - SUBLANES:tm]
            ffn_ref[0, c] = ug[tm - SUBLANES:tm]
            ffn_ref[0, nch + c] = uv[tm - SUBLANES:tm]
        else:
            cg = conv(ug, (p1_ref[c], p2_ref[c]), w[0:4])
            cv = conv(uv, (p1_ref[nch + c], p2_ref[nch + c]), w[4:8])
            ffn_ref[c] = ug
            ffn_ref[nch + c] = uv
        act = (_gelu(cg) * cv).astype(BF16)
        acc_s[...] += _dot(act, wdn_ref[c])
        return 0

    lax.fori_loop(0, nch, chunk, 0, unroll=True)
    h2 = h1 + _rms(acc_s[...], gfo_ref[...])
    ple = _dot(p_ref[...].astype(BF16), wpp_ref[...]) * _sigmoid(_dot(h2.astype(BF16), wpg_ref[...]))
    ho_ref[...] = h2 + _rms(ple, gpl_ref[...])


def _tail(h, ya, yb, p, ffn0, w, *, bsz, t):
    n, d = h.shape
    nch = N_FF_CHUNKS
    cw = FF_CHUNK
    chain = t > SUBLANES
    consts = [w['wo'], w['gmp'], w['gfp'], w['wug'], w['wuv'], w['wcv'], w['wdn'], w['gfo'], w['wpp'],
              w['wpg'], w['gpl']]
    const_specs = [_const(x.shape) for x in consts]
    if chain:
        assert ffn0 is None, "a long sequence starts from an empty ConvFFN buffer"
        tm = min(MATMUL_TILE, t)
        nt = t // tm
        grid = (bsz, nt)
        rows = lambda width: pl.BlockSpec((tm, width), lambda b, i: (b * nt + i, 0))
        ffn_spec = pl.BlockSpec((1, 2 * nch, SUBLANES, cw), lambda b, i: (b, 0, 0, 0))
        ffn_sds = jax.ShapeDtypeStruct((bsz, 2 * nch, SUBLANES, cw), F32)
        extra, extra_specs = [], []
        scratch = [pltpu.VMEM((tm, d), F32), pltpu.VMEM((2 * nch, SUBLANES, cw), F32)]
    else:
        assert t == SUBLANES
        tm = n
        grid = (1, 1)
        rows = lambda width: pl.BlockSpec((tm, width), lambda b, i: (0, 0))
        ffn_spec = _full((2 * nch, tm, cw))
        ffn_sds = jax.ShapeDtypeStruct((2 * nch, tm, cw), F32)
        buf = ffn0.astype(F32).reshape(bsz, 2, 2 * nch, cw).transpose(2, 0, 1, 3)
        zero = jnp.zeros((2 * nch, bsz, SUBLANES - 2, cw), F32)
        p1 = jnp.concatenate([buf[:, :, 1:2], zero, zero[:, :, :1]], axis=2).reshape(2 * nch, tm, cw)
        p2 = jnp.concatenate([buf, zero], axis=2).reshape(2 * nch, tm, cw)
        extra = [p1, p2]
        extra_specs = [_full(p1.shape), _full(p2.shape)]
        scratch = [pltpu.VMEM((tm, d), F32)]
    ho, ffn = pl.pallas_call(
        functools.partial(_tail_kernel, chain=chain),
        grid=grid,
        in_specs=[rows(d), rows(SLAB), rows(SLAB), rows(p.shape[1])] + extra_specs + const_specs,
        out_specs=[rows(d), ffn_spec],
        out_shape=[jax.ShapeDtypeStruct((n, d), F32), ffn_sds],
        scratch_shapes=scratch,
        name="layer_tail")(h, ya, yb, p, *extra, *consts)
    if chain:
        st = ffn[:, :, SUBLANES - 2:, :]
    else:
        st = ffn.reshape(2 * nch, bsz, SUBLANES, cw)[:, :, SUBLANES - 2:, :].transpose(1, 0, 2, 3)
    return ho, st.transpose(0, 2, 1, 3).reshape(bsz, 2, 2 * nch * cw)


def _tail_weights(i, w_out, g_mix_post, g_ffn_pre, g_ffn_post, w_ffn_up, w_ffn_conv, b_ffn_conv,
                  w_ffn_down, w_ple_proj, w_ple_gate, g_ple):
    d = w_out.shape[1]
    nch, cw = N_FF_CHUNKS, FF_CHUNK
    up = w_ffn_up[i].astype(BF16).reshape(d, 2, nch, cw).transpose(1, 2, 0, 3)
    taps = jnp.concatenate([w_ffn_conv[i].astype(F32), b_ffn_conv[i].astype(F32)[None]], axis=0)
    wcv = taps.reshape(4, 2, nch, cw).transpose(2, 1, 0, 3).reshape(nch, 8, cw)
    row = lambda g: g.astype(F32).reshape(1, d)
    return dict(
        wo=w_out.astype(BF16).reshape(2, SLAB, d), gmp=row(g_mix_post[i]), gfp=row(g_ffn_pre[i]),
        wug=up[0], wuv=up[1], wcv=wcv, wdn=w_ffn_down[i].astype(BF16).reshape(nch, cw, d),
        gfo=row(g_ffn_post[i]), wpp=w_ple_proj[i].astype(BF16), wpg=w_ple_gate[i].astype(BF16),
        gpl=row(g_ple[i]))


def kernel(x_prompt, x_sample, p_prompt, p_sample, cache_moba_k, cache_moba_v, state_s5_re, state_s5_im,
           state_hgrn, cache_sb_k, cache_sb_v, state_ffn, page_table,
           g_mix_pre, g_mix_post, g_ffn_pre, g_ffn_post, w_ffn_up, w_ffn_conv, b_ffn_conv, w_ffn_down,
           w_ple_proj, w_ple_gate, g_ple,
           w_in_a, w_out_a, s5_a_re, s5_a_im, s5_log_dt, s5_b_re, s5_b_im, s5_c_re, s5_c_im, s5_d,
           s5_w_glu, s5_b_glu,
           w_in_c, w_out_c, hgrn_lb, g_hgrn_norm):
    depth = g_mix_pre.shape[0]
    d_model = x_prompt.shape[-1]
    n_pages = page_table.shape[1]
    past_len = n_pages * cache_moba_k.shape[2]
    page_table = page_table.astype(jnp.int32)

    layers = []
    for i in range(depth):
        j = i // 2
        lw = {}
        if i % 2 == 0:
            lw['w_in'] = w_in_a[j].astype(BF16)
            lw['disc'] = _s5_discretize(s5_a_re[j], s5_a_im[j], s5_log_dt[j])
            lw['wb'], lw['wc'] = _s5_weights(s5_b_re[j], s5_b_im[j], s5_c_re[j], s5_c_im[j])
            lw['d'] = s5_d[j].astype(F32).reshape(S5_WIDTH)
            lw['wglu'] = s5_w_glu[j].astype(BF16)
            lw['bglu'] = s5_b_glu[j].astype(F32)
            w_out = w_out_a[j]
        else:
            lw['w_in'] = w_in_c[j].astype(BF16)
            w_out = w_out_c[j]
        lw['tail'] = _tail_weights(i, w_out, g_mix_post, g_ffn_pre, g_ffn_post, w_ffn_up, w_ffn_conv,
                                   b_ffn_conv, w_ffn_down, w_ple_proj, w_ple_gate, g_ple)
        layers.append(lw)

    def pool2d(pool, j):
        return pool[j].transpose(0, 2, 3, 1).reshape(pool.shape[1], ATT_WIDTH, pool.shape[2])

    def run(x, p, q_start, s5_re0, s5_im0, hgrn0, ffn0, has_past):
        bsz, t, _ = x.shape
        n = bsz * t
        h = x.astype(F32).reshape(n, d_model)
        tables = _rope_tables(t, q_start)
        if t < MATMUL_TILE:
            tables = tuple(jnp.tile(tb, (n // t, 1)) for tb in tables)
        mk, mv, sr, si, hs, sk, sv, fb = [], [], [], [], [], [], [], []
        long_seq = t % MATMUL_TILE == 0

        def kv_out(x, x_t, which):
            if x_t:
                return x_t[which].reshape(bsz, HEADS, HEAD_DIM, t).transpose(0, 3, 1, 2)
            return x.reshape(bsz, t, HEADS, HEAD_DIM)

        for i in range(depth):
            j = i // 2
            lw = layers[i]
            g_pre = g_mix_pre[i].astype(F32)
            if i % 2 == 0:
                u, q, k, v, *kv_t = _inproj(h, g_pre, lw['w_in'], tables, rope_slabs=(1, 2),
                                            t_slabs=(2, 3) if long_seq else (), seq=(bsz, t))
                y_a, hr, hi = _s5(u, s5_re0[j].astype(F32).reshape(bsz, 1, S5_LANES),
                                  s5_im0[j].astype(F32).reshape(bsz, 1, S5_LANES), lw['disc'], lw['wb'],
                                  lw['wc'], lw['d'], lw['wglu'], lw['bglu'], bsz=bsz, t=t)
                if has_past:
                    y_b = _moba_sample(q, k, v, pool2d(cache_moba_k, j), pool2d(cache_moba_v, j),
                                       page_table, bsz=bsz, t=t)
                else:
                    y_b = _moba_prompt(q, k, v, bsz=bsz, t=t)
                mk.append(kv_out(k, kv_t, 0))
                mv.append(kv_out(v, kv_t, 1))
                sr.append(hr.reshape(bsz, S5_GROUPS, S5_STATE))
                si.append(hi.reshape(bsz, S5_GROUPS, S5_STATE))
            else:
                qh, fh, ih, gh, q, k, v, *kv_t = _inproj(h, g_pre, lw['w_in'], tables, rope_slabs=(),
                                                         t_slabs=(5, 6) if long_seq else (), seq=(bsz, t))
                y_a, s_fin = _hgrn(qh, fh, ih, gh, hgrn_lb.astype(F32), hgrn0[j].astype(F32),
                                   g_hgrn_norm[j].astype(F32), bsz=bsz, t=t, layer=i)
                if has_past:
                    y_b = _sb_sample(q, k, v, pool2d(cache_sb_k, j), pool2d(cache_sb_v, j), page_table,
                                     bsz=bsz, t=t)
                else:
                    y_b = _sb_prompt(q, k, v, bsz=bsz, t=t)
                hs.append(s_fin)
                sk.append(kv_out(k, kv_t, 0))
                sv.append(kv_out(v, kv_t, 1))
            h, buf = _tail(h, y_a, y_b, p[i].astype(F32).reshape(n, p.shape[-1]),
                           None if ffn0 is None else ffn0[i], lw['tail'], bsz=bsz, t=t)
            fb.append(buf)
        return (h.reshape(bsz, t, d_model), jnp.stack(mk), jnp.stack(mv), jnp.stack(sr), jnp.stack(si),
                jnp.stack(hs), jnp.stack(sk), jnp.stack(sv), jnp.stack(fb))

    bp = x_prompt.shape[0]
    n_a = (depth + 1) // 2
    n_c = depth // 2
    outs_p = run(x_prompt, p_prompt, 0,
                 jnp.zeros((n_a, bp, S5_GROUPS, S5_STATE), F32), jnp.zeros((n_a, bp, S5_GROUPS, S5_STATE), F32),
                 jnp.zeros((n_c, bp, HGRN_HEADS, HGRN_DIM, HGRN_DIM), F32), None, False)
    outs_s = run(x_sample, p_sample, past_len, state_s5_re, state_s5_im, state_hgrn, state_ffn, True)
    return (outs_p[0], outs_s[0]) + tuple(outs_p[1:]) + tuple(outs_s[1:])
```

```python
import functools
import math

import jax
import jax.numpy as jnp
from jax import lax
from jax.experimental import pallas as pl
from jax.experimental.pallas import tpu as pltpu

F32 = jnp.float32
BF16 = jnp.bfloat16

S5_WIDTH = 512
S5_GROUPS = 32
S5_GROUP = 16
S5_STATE = 64
S5_LANES = S5_GROUPS * S5_STATE
HEADS = 8
HEAD_DIM = 64
ATT_WIDTH = HEADS * HEAD_DIM
ROPE_DIM = 16
ROPE_THETA = 500000.0
MOBA_BLOCK = 256
MOBA_TOPK = 3
HGRN_HEADS = 4
HGRN_DIM = 128
HGRN_CHUNK = 64
HGRN_SUB = 16
D_FF = 2816
FF_CHUNK = 256
N_FF_CHUNKS = D_FF // FF_CHUNK
PAGE = 128
EPS = 1e-6
NEG = -1e30
SLAB = 512
LANES = 128
SUBLANES = 8
ROW_TILE = 256
MATMUL_TILE = 512
SB_BLOCK = 256
PAGES_PER_STEP = 16
LOG2E = 1.4426950408889634
EXP_UNDERFLOW = -104.0


def _dot(a, b):
    return jnp.dot(a, b, preferred_element_type=F32)


def _dot_nt(a, b):
    return lax.dot_general(a, b, (((1,), (1,)), ((), ())), preferred_element_type=F32)


def _dot_tn(a, b):
    return lax.dot_general(a, b, (((0,), (0,)), ((), ())), preferred_element_type=F32)


def _split3(x):
    hi = x.astype(BF16)
    r1 = x - hi.astype(F32)
    mid = r1.astype(BF16)
    lo = (r1 - mid.astype(F32)).astype(BF16)
    return hi, mid, lo


def _rms(x, g):
    return x * lax.rsqrt(jnp.mean(x * x, axis=-1, keepdims=True) + EPS) * g


def _sigmoid(x):
    return 1.0 / (1.0 + jnp.exp(-x))


def _gelu(x):
    return 0.5 * x * (1.0 + jnp.tanh(0.7978845608028654 * (x + 0.044715 * (x * x * x))))


def _full(shape):
    nd = len(shape)
    return pl.BlockSpec(shape, lambda *_: (0,) * nd)


def _const(shape):
    nd = len(shape)
    return pl.BlockSpec(shape, lambda *_: (0,) * nd, pipeline_mode=pl.Buffered(1))


def _rope_table_kernel(c_ref, s1_ref, s2_ref, *, q_start):
    rows, lanes = c_ref.shape
    i = pl.program_id(0)
    lane = lax.broadcasted_iota(jnp.int32, (rows, lanes), 1)
    pos = lax.broadcasted_iota(jnp.int32, (rows, lanes), 0) + i * rows + q_start
    d = lane % HEAD_DIM
    half = ROPE_DIM // 2
    inv = jnp.exp((d % half).astype(F32) * (-math.log(ROPE_THETA) / half))
    ang = pos.astype(F32) * inv
    cs = jnp.cos(ang)
    sn = jnp.sin(ang)
    c_ref[...] = jnp.where(d < ROPE_DIM, cs, 1.0)
    s1_ref[...] = jnp.where(d < half, -sn, 0.0)
    s2_ref[...] = jnp.where((d >= half) & (d < ROPE_DIM), sn, 0.0)


def _rope_tables(t, q_start):
    rows = min(t, 512)
    sds = jax.ShapeDtypeStruct((t, LANES), F32)
    spec = pl.BlockSpec((rows, LANES), lambda i: (i, 0))
    return pl.pallas_call(
        functools.partial(_rope_table_kernel, q_start=q_start),
        grid=(t // rows,), out_shape=(sds, sds, sds), out_specs=(spec, spec, spec),
        name="rope_tables")()


def _inproj_kernel(h_ref, g_ref, w_ref, c_ref, s1_ref, s2_ref, *out_refs, n_slabs, rope_slabs,
                   t_slabs):
    a = _rms(h_ref[...], g_ref[...]).astype(BF16)
    half = ROPE_DIM // 2
    for s in range(n_slabs):
        o_ref = out_refs[s]
        z = _dot(a, w_ref[:, s * SLAB:(s + 1) * SLAB])
        if s in rope_slabs:
            c = c_ref[...]
            s1 = s1_ref[...]
            s2 = s2_ref[...]
            for j in range(SLAB // LANES):
                x = z[:, j * LANES:(j + 1) * LANES]
                o_ref[:, j * LANES:(j + 1) * LANES] = (
                    x * c + pltpu.roll(x, LANES - half, 1) * s1 + pltpu.roll(x, half, 1) * s2)
        else:
            o_ref[...] = z
        if s in t_slabs:
            out_refs[n_slabs + t_slabs.index(s)][...] = o_ref[...].T


def _inproj(h, g, w, tables, rope_slabs, t_slabs=(), seq=None):
    n, d = h.shape
    n_slabs = w.shape[1] // SLAB
    tm = min(MATMUL_TILE, n)
    t_tiles = tables[0].shape[0] // tm
    tab_spec = pl.BlockSpec((tm, LANES), lambda i: (i % t_tiles, 0))
    out_spec = pl.BlockSpec((tm, SLAB), lambda i: (i, 0))
    out_specs = [out_spec] * n_slabs
    out_shape = [jax.ShapeDtypeStruct((n, SLAB), F32)] * n_slabs
    if t_slabs:
        bsz, t = seq
        assert t % tm == 0
        nt = t // tm
        out_specs += [pl.BlockSpec((None, SLAB, tm), lambda i: (i // nt, 0, i % nt))] * len(t_slabs)
        out_shape += [jax.ShapeDtypeStruct((bsz, SLAB, t), F32)] * len(t_slabs)
    return pl.pallas_call(
        functools.partial(_inproj_kernel, n_slabs=n_slabs, rope_slabs=rope_slabs, t_slabs=t_slabs),
        grid=(n // tm,),
        in_specs=[pl.BlockSpec((tm, d), lambda i: (i, 0)), _const((1, d)), _const(w.shape),
                  tab_spec, tab_spec, tab_spec],
        out_specs=out_specs, out_shape=out_shape,
        name="inproj")(h, g.reshape(1, d), w, *tables)


def _s5_disc_kernel(ar_ref, ai_ref, ldt_ref, pre_ref, pim_ref, cc_ref):
    ar = ar_ref[...]
    ai = ai_ref[...]
    dt = jnp.exp(ldt_ref[...])
    row = lax.broadcasted_iota(jnp.int32, (2 * SUBLANES, 1), 0)
    n = jnp.where(row < SUBLANES, row + 1,
                  jnp.where(row == SUBLANES + 1, 2, jnp.where(row == SUBLANES + 2, 4, 1))).astype(F32)
    mag = jnp.exp(ar * dt * n)
    ang = ai * dt * n
    pre = mag * jnp.cos(ang)
    pim = mag * jnp.sin(ang)
    pre_ref[...] = pre
    pim_ref[...] = pim
    abr = pre[0:1]
    abi = pim[0:1]
    den = ar * ar + ai * ai
    nr = abr - 1.0
    cc_ref[0:1, :] = (nr * ar + abi * ai) / den
    cc_ref[1:2, :] = (abi * ar - nr * ai) / den


def _s5_discretize(a_re, a_im, log_dt):
    ar = a_re.reshape(1, S5_LANES)
    ai = a_im.reshape(1, S5_LANES)
    ldt = jnp.repeat(log_dt, S5_STATE).reshape(1, S5_LANES)
    return pl.pallas_call(
        _s5_disc_kernel,
        out_shape=(jax.ShapeDtypeStruct((2 * SUBLANES, S5_LANES), F32),
                   jax.ShapeDtypeStruct((2 * SUBLANES, S5_LANES), F32),
                   jax.ShapeDtypeStruct((2, S5_LANES), F32)),
        name="s5_discretize")(ar, ai, ldt)


SCAN_LANES = 512


def _s5_kernel(u_ref, h0r_ref, h0i_ref, pre_ref, pim_ref, cc_ref, wb_ref, wc_ref, d_ref,
               wglu_ref, bglu_ref, y_ref, hr_out_ref, hi_out_ref, xr_s, xi_s, car_s, *, chain):
    tm = u_ref.shape[0]
    nb = tm // SUBLANES
    u = u_ref[...]
    ub = u.astype(BF16)
    half_in = S5_WIDTH // 2
    half_st = S5_LANES // 2
    for hf in range(2):
        bu = _dot(ub[:, hf * half_in:(hf + 1) * half_in], wb_ref[hf])
        bur = bu[:, :half_st]
        bui = bu[:, half_st:]
        ls = slice(hf * half_st, (hf + 1) * half_st)
        cr = cc_ref[0:1, ls]
        ci = cc_ref[1:2, ls]
        xr_s[:, ls] = cr * bur - ci * bui
        xi_s[:, ls] = cr * bui + ci * bur

    if chain:
        t = pl.program_id(1)

        @pl.when(t == 0)
        def _():
            car_s[0:1, :] = h0r_ref[0]
            car_s[1:2, :] = h0i_ref[0]

    rowi = lax.broadcasted_iota(jnp.int32, (SUBLANES, SCAN_LANES), 0)
    for lc in range(S5_LANES // SCAN_LANES):
        ls = slice(lc * SCAN_LANES, (lc + 1) * SCAN_LANES)
        pre = pre_ref[:, ls]
        pim = pim_ref[:, ls]

        def body(i, carry, ls=ls, pre=pre, pim=pim):
            r0 = pl.multiple_of(i * SUBLANES, SUBLANES)
            hr = xr_s[pl.ds(r0, SUBLANES), ls]
            hi = xi_s[pl.ds(r0, SUBLANES), ls]
            for k, prow in ((1, SUBLANES), (2, SUBLANES + 1), (4, SUBLANES + 2)):
                ar = pre[prow:prow + 1]
                ai = pim[prow:prow + 1]
                sr = jnp.where(rowi >= k, pltpu.roll(hr, k, 0), 0.0)
                si = jnp.where(rowi >= k, pltpu.roll(hi, k, 0), 0.0)
                hr, hi = hr + ar * sr - ai * si, hi + ar * si + ai * sr
            if chain:
                cr_, ci_ = carry
            else:
                cr_ = h0r_ref[i, :, ls]
                ci_ = h0i_ref[i, :, ls]
            pr8 = pre[0:SUBLANES]
            pi8 = pim[0:SUBLANES]
            hr, hi = hr + pr8 * cr_ - pi8 * ci_, hi + pr8 * ci_ + pi8 * cr_
            xr_s[pl.ds(r0, SUBLANES), ls] = hr
            xi_s[pl.ds(r0, SUBLANES), ls] = hi
            last_r = hr[SUBLANES - 1:SUBLANES]
            last_i = hi[SUBLANES - 1:SUBLANES]
            if chain:
                return last_r, last_i
            hr_out_ref[i, :, ls] = last_r
            hi_out_ref[i, :, ls] = last_i
            return carry

        if chain:
            cr_, ci_ = lax.fori_loop(0, nb, body, (car_s[0:1, ls], car_s[1:2, ls]))
            car_s[0:1, ls] = cr_
            car_s[1:2, ls] = ci_
        else:
            lax.fori_loop(0, nb, body, 0)

    if chain:
        hr_out_ref[0] = car_s[0:1, :]
        hi_out_ref[0] = car_s[1:2, :]

    ys = []
    for hf in range(2):
        ls = slice(hf * half_st, (hf + 1) * half_st)
        hcat = jnp.concatenate([xr_s[:, ls], xi_s[:, ls]], axis=1).astype(BF16)
        ys.append(_dot(hcat, wc_ref[hf]))
    y = jnp.concatenate(ys, axis=1) + d_ref[...] * u
    y = _gelu(y)
    y_ref[...] = y * _sigmoid(_dot(y.astype(BF16), wglu_ref[...]) + bglu_ref[...])


def _s5_weights(b_re, b_im, c_re, c_im):
    gh = S5_GROUPS // 2
    eye = jnp.eye(gh, dtype=F32)

    def bmat(b):
        bt = b.astype(F32).transpose(0, 2, 1).reshape(2, gh, S5_GROUP, S5_STATE)
        return jnp.einsum('xghp,gk->xghkp', bt, eye).reshape(2, gh * S5_GROUP, gh * S5_STATE)

    def cmat(c):
        ct = c.astype(F32).transpose(0, 2, 1).reshape(2, gh, S5_STATE, S5_GROUP)
        return jnp.einsum('xgph,gk->xgpkh', ct, eye).reshape(2, gh * S5_STATE, gh * S5_GROUP)

    wb = jnp.concatenate([bmat(b_re), bmat(b_im)], axis=2).astype(BF16)
    wc = jnp.concatenate([cmat(c_re), -cmat(c_im)], axis=1).astype(BF16)
    return wb, wc


def _s5(u, h0r, h0i, disc, wb, wc, dvec, wglu, bglu, *, bsz, t):
    n = bsz * t
    pre, pim, cc = disc
    chain = t > SUBLANES
    if chain:
        tm = min(ROW_TILE, t)
        grid = (bsz, t // tm)
        row_spec = pl.BlockSpec((tm, S5_WIDTH), lambda b, i: (b * (t // tm) + i, 0))
        st_spec = pl.BlockSpec((1, 1, S5_LANES), lambda b, i: (b, 0, 0))
    else:
        assert t == SUBLANES
        tm = n
        grid = (1, 1)
        row_spec = pl.BlockSpec((tm, S5_WIDTH), lambda b, i: (0, 0))
        st_spec = pl.BlockSpec((bsz, 1, S5_LANES), lambda b, i: (0, 0, 0))
    st_sds = jax.ShapeDtypeStruct((bsz, 1, S5_LANES), F32)
    return pl.pallas_call(
        functools.partial(_s5_kernel, chain=chain),
        grid=grid,
        in_specs=[row_spec, st_spec, st_spec, _const(pre.shape), _const(pim.shape), _const(cc.shape),
                  _const(wb.shape), _const(wc.shape), _const((1, S5_WIDTH)), _const(wglu.shape),
                  _const((1, S5_WIDTH))],
        out_specs=[row_spec, st_spec, st_spec],
        out_shape=[jax.ShapeDtypeStruct((n, S5_WIDTH), F32), st_sds, st_sds],
        scratch_shapes=[pltpu.VMEM((tm, S5_LANES), F32), pltpu.VMEM((tm, S5_LANES), F32),
                        pltpu.VMEM((2, S5_LANES), F32)],
        name="s5_mixer")(u, h0r, h0i, pre, pim, cc, wb, wc, dvec.reshape(1, S5_WIDTH), wglu,
                         bglu.reshape(1, S5_WIDTH))


def _top_blocks(gate, n_past, col):
    g = jnp.where(col < n_past, gate, NEG)
    sel = jnp.zeros(gate.shape, F32)
    colf = col.astype(F32)
    for _ in range(MOBA_TOPK):
        m = jnp.max(g, axis=-1, keepdims=True)
        first = jnp.min(jnp.where(g == m, colf, float(LANES)), axis=-1, keepdims=True)
        pick = colf == first
        sel = jnp.where(pick & (m > 0.5 * NEG), 1.0, sel)
        g = jnp.where(pick, -3e38, g)
    return sel


def _top_blocks_t(gate, n_past):
    rowf = lax.broadcasted_iota(jnp.int32, gate.shape, 0).astype(F32)
    g = jnp.where(rowf < n_past, gate, NEG)
    sel = jnp.zeros(gate.shape, F32)
    for _ in range(MOBA_TOPK):
        m = jnp.max(g, axis=0, keepdims=True)
        first = jnp.min(jnp.where(g == m, rowf, float(LANES)), axis=0, keepdims=True)
        pick = rowf == first
        sel = jnp.where(pick & (m > 0.5 * NEG), 1.0, sel)
        g = jnp.where(pick, -3e38, g)
    return sel


def _moba_prompt_kernel(q_ref, k_ref, v_ref, o_ref, kb_s, vt_s, kmean_s, sel_s, *, nblk):
    bs = MOBA_BLOCK
    i = pl.program_id(2)

    @pl.when(i == 0)
    def _():
        kmean_s[...] = jnp.zeros(kmean_s.shape, F32)
        for j in range(nblk):
            kj = k_ref[j * bs:(j + 1) * bs, :]
            kb_s[j * bs:(j + 1) * bs, :] = kj.astype(BF16)
            kmean_s[j:j + 1, :] = jnp.sum(kj, axis=0, keepdims=True) * (1.0 / bs)
            vt_s[j] = v_ref[j * bs:(j + 1) * bs, :].T.astype(BF16)

    lane = lax.broadcasted_iota(jnp.int32, (1, LANES), 1)
    rowd = lax.broadcasted_iota(jnp.int32, (LANES, 1), 0)
    causal = (lax.broadcasted_iota(jnp.int32, (bs, 2 * bs), 0)
              <= lax.broadcasted_iota(jnp.int32, (bs, 2 * bs), 1) % bs)
    kmb = kmean_s[...].astype(BF16)
    qf = q_ref[...]
    r0 = pl.multiple_of(i * bs, bs)
    qhs = []
    for hd in range(2):
        lm = (lane >= HEAD_DIM * hd) & (lane < HEAD_DIM * (hd + 1))
        qhs.append(jnp.where(lm, qf, 0.0))
    qh2 = jnp.concatenate(qhs, axis=0)
    qs2 = (qh2 * (HEAD_DIM ** -0.5 * LOG2E)).astype(BF16)
    sel = _top_blocks_t(_dot_nt(kmb, qh2.astype(BF16)), i)
    for b in range(nblk):
        sel_s[b] = sel[b:b + 1, :]
    s = jnp.where(causal, _dot_nt(kb_s[pl.ds(r0, bs), :], qs2), NEG)
    m = jnp.max(s, axis=0, keepdims=True)
    p = jnp.exp2(s - m)
    carry = (m, jnp.sum(p, axis=0, keepdims=True), _dot(vt_s[i], p.astype(BF16)))

    def kvpair(jj, carry):
        m, l, acc = carry
        c0 = pl.multiple_of(jj * 2 * bs, 2 * bs)
        s = _dot_nt(kb_s[pl.ds(c0, 2 * bs), :], qs2)
        s0 = jnp.where(sel_s[2 * jj] > 0.5, s[:bs], NEG)
        s1 = jnp.where(sel_s[2 * jj + 1] > 0.5, s[bs:], NEG)
        m_new = jnp.maximum(m, jnp.maximum(jnp.max(s0, axis=0, keepdims=True),
                                           jnp.max(s1, axis=0, keepdims=True)))
        alpha = jnp.exp2(m - m_new)
        p0 = jnp.exp2(s0 - m_new)
        p1 = jnp.exp2(s1 - m_new)
        l = alpha * l + jnp.sum(p0, axis=0, keepdims=True) + jnp.sum(p1, axis=0, keepdims=True)
        acc = (alpha * acc + _dot(vt_s[2 * jj], p0.astype(BF16))
               + _dot(vt_s[2 * jj + 1], p1.astype(BF16)))
        return m_new, l, acc

    m, l, acc = lax.fori_loop(0, (i + 1) // 2, kvpair, carry)
    out2 = acc / l
    o_ref[...] = jnp.where(rowd < HEAD_DIM, out2[:, :bs], out2[:, bs:]).T


def _moba_prompt(q, k, v, *, bsz, t):
    nblk = t // MOBA_BLOCK
    assert t % MOBA_BLOCK == 0 and nblk % 2 == 0 and nblk <= LANES
    nblk_rows = -(-nblk // SUBLANES) * SUBLANES
    n = bsz * t
    q_spec = pl.BlockSpec((MOBA_BLOCK, LANES), lambda b, hp, i: (b * nblk + i, hp))
    kv_spec = pl.BlockSpec((t, LANES), lambda b, hp, i: (b, hp))
    return pl.pallas_call(
        functools.partial(_moba_prompt_kernel, nblk=nblk),
        grid=(bsz, ATT_WIDTH // LANES, nblk),
        in_specs=[q_spec, kv_spec, kv_spec], out_specs=q_spec,
        out_shape=jax.ShapeDtypeStruct((n, ATT_WIDTH), F32),
        scratch_shapes=[pltpu.VMEM((t, LANES), BF16),
                        pltpu.VMEM((nblk, LANES, MOBA_BLOCK), BF16),
                        pltpu.VMEM((nblk_rows, LANES), F32),
                        pltpu.VMEM((nblk, 1, 2 * MOBA_BLOCK), F32)],
        name="moba_prompt")(q, k, v)


def _expand_heads(q, qexp_s):
    t = q.shape[0]
    lane = lax.broadcasted_iota(jnp.int32, (1, ATT_WIDTH), 1)
    for h in range(HEADS):
        lm = (lane >= h * HEAD_DIM) & (lane < (h + 1) * HEAD_DIM)
        qexp_s[h * t:(h + 1) * t, :] = jnp.where(lm, q, 0.0)


def _collapse_heads(acc, t):
    lane = lax.broadcasted_iota(jnp.int32, (1, ATT_WIDTH), 1)
    out = jnp.zeros((t, ATT_WIDTH), F32)
    for h in range(HEADS):
        lm = (lane >= h * HEAD_DIM) & (lane < (h + 1) * HEAD_DIM)
        out = out + jnp.where(lm, acc[h * t:(h + 1) * t, :], 0.0)
    return out


def _pad_rows(x_ref, new_s):
    new_s[...] = jnp.zeros(new_s.shape, F32)
    new_s[0:x_ref.shape[0], :] = x_ref[...]
    return new_s[...].astype(BF16)


def _moba_sample_kernel(pt_ref, q_ref, kn_ref, vn_ref, *rest, n_pages, t):
    pps = PAGES_PER_STEP
    k_refs = rest[:pps]
    v_refs = rest[pps:2 * pps]
    o_ref = rest[2 * pps]
    qexp_s, new_s, s_all, gate_s, bmax_s, sel_s, m_s, l_s, acc_s = rest[2 * pps + 1:]
    ns = n_pages // pps
    s_id = pl.program_id(1)
    rows = HEADS * t
    col = lax.broadcasted_iota(jnp.int32, (rows, LANES), 1)
    scale = HEAD_DIM ** -0.5
    pages_per_block = MOBA_BLOCK // PAGE

    @pl.when(s_id == 0)
    def _():
        _expand_heads(q_ref[...], qexp_s)
        gate_s[...] = jnp.zeros(gate_s.shape, F32)
        bmax_s[...] = jnp.full(bmax_s.shape, NEG, F32)

    @pl.when(s_id < ns)
    def _():
        qe = qexp_s[...].astype(BF16)
        g = gate_s[...]
        bm = bmax_s[...]
        for i in range(pps):
            pg = s_id * pps + i
            s = _dot(qe, k_refs[i][...].astype(BF16))
            s_all[pg] = s
            mine = col == pg // pages_per_block
            g = g + jnp.where(mine, jnp.sum(s, axis=-1, keepdims=True) * (1.0 / MOBA_BLOCK), 0.0)
            bm = jnp.where(mine, jnp.maximum(bm, jnp.max(s, axis=-1, keepdims=True)), bm)
        gate_s[...] = g
        bmax_s[...] = bm

    @pl.when(s_id == ns)
    def _():
        sel = _top_blocks(gate_s[...], n_pages // pages_per_block, col)
        sel_s[...] = sel
        qe = qexp_s[...].astype(BF16)
        sn = _dot_nt(qe, _pad_rows(kn_ref, new_s)) * scale
        rowt = lax.broadcasted_iota(jnp.int32, (rows, LANES), 0) % t
        sn = jnp.where(col <= rowt, sn, NEG)
        m = jnp.maximum(jnp.max(sn, axis=-1, keepdims=True),
                        jnp.max(jnp.where(sel > 0.5, bmax_s[...] * scale, NEG), axis=-1, keepdims=True))
        pn = jnp.exp(sn - m)
        m_s[...] = jnp.broadcast_to(m, m_s.shape)
        l_s[...] = jnp.broadcast_to(jnp.sum(pn, axis=-1, keepdims=True), l_s.shape)
        acc_s[...] = _dot(pn.astype(BF16), _pad_rows(vn_ref, new_s))

    @pl.when(s_id >= ns)
    def _():
        sel = sel_s[...]
        m = m_s[:, 0:1]
        l = l_s[:, 0:1]
        acc = acc_s[...]
        for i in range(pps):
            pg = (s_id - ns) * pps + i
            selc = jnp.sum(jnp.where(col == pg // pages_per_block, sel, 0.0), axis=-1, keepdims=True)
            p = jnp.exp(jnp.where(selc > 0.5, s_all[pg] * scale, NEG) - m)
            l = l + jnp.sum(p, axis=-1, keepdims=True)
            acc = acc + _dot_nt(p.astype(BF16), v_refs[i][...].astype(BF16))
        l_s[...] = jnp.broadcast_to(l, l_s.shape)
        acc_s[...] = acc

    @pl.when(s_id == 2 * ns - 1)
    def _():
        o_ref[...] = _collapse_heads(acc_s[...] / l_s[:, 0:1], t)


def _sample_specs(page_table, n_pages, t, page_index):
    pps = PAGES_PER_STEP
    row_spec = pl.BlockSpec((t, ATT_WIDTH), lambda b, s, pt: (b, 0))

    def page_spec(i, which):
        return pl.BlockSpec((None, ATT_WIDTH, PAGE),
                            lambda b, s, pt: (pt[b, page_index(s, i, which)], 0, 0))

    in_specs = ([row_spec] * 3 + [page_spec(i, 0) for i in range(pps)]
                + [page_spec(i, 1) for i in range(pps)])
    return in_specs, row_spec


def _moba_sample(q, k_new, v_new, k_pool, v_pool, page_table, *, bsz, t):
    n_pages = page_table.shape[1]
    pps = PAGES_PER_STEP
    assert n_pages % pps == 0 and (n_pages * PAGE) % MOBA_BLOCK == 0 and t <= SUBLANES
    assert n_pages * PAGE // MOBA_BLOCK <= LANES
    ns = n_pages // pps
    rows = HEADS * t

    def page_index(s, i, which):
        step = jnp.minimum(s, ns - 1) if which == 0 else jnp.maximum(s - ns, 0)
        return step * pps + i

    in_specs, out_spec = _sample_specs(page_table, n_pages, t, page_index)
    grid_spec = pltpu.PrefetchScalarGridSpec(
        num_scalar_prefetch=1, grid=(bsz, 2 * ns), in_specs=in_specs, out_specs=out_spec,
        scratch_shapes=[pltpu.VMEM((rows, ATT_WIDTH), F32),
                        pltpu.VMEM((PAGE, ATT_WIDTH), F32),
                        pltpu.VMEM((n_pages, rows, PAGE), F32)]
                       + [pltpu.VMEM((rows, LANES), F32)] * 5
                       + [pltpu.VMEM((rows, ATT_WIDTH), F32)])
    return pl.pallas_call(
        functools.partial(_moba_sample_kernel, n_pages=n_pages, t=t),
        grid_spec=grid_spec,
        out_shape=jax.ShapeDtypeStruct((bsz * t, ATT_WIDTH), F32),
        name="moba_sample")(page_table, q, k_new, v_new, *([k_pool] * pps), *([v_pool] * pps))


def _sb_weights(z, r, tri, strict):
    n = z.shape[0]
    lg = jnp.log(1.0 + jnp.exp(-jnp.abs(z)))
    log_beta = jnp.minimum(z, 0.0) - lg
    log_keep = jnp.minimum(-z, 0.0) - lg
    if strict is not None:
        log_beta = jnp.where(strict, log_beta, NEG)
        log_keep = jnp.where(strict, log_keep, 0.0)
    st = _dot(jnp.concatenate(_split3(log_keep), axis=0), tri)
    later = st[:n] + st[n:2 * n] + st[2 * n:]
    w = jnp.exp(log_beta + later + r)
    return w.astype(BF16), r + jnp.sum(log_keep, axis=-1, keepdims=True)


def _suffix_matrix(n):
    return (lax.broadcasted_iota(jnp.int32, (n, n), 0)
            > lax.broadcasted_iota(jnp.int32, (n, n), 1)).astype(BF16)


def _sb_prompt_kernel(q_ref, k_ref, v_ref, o_ref, kb_s, vb_s):
    bs = SB_BLOCK
    i = pl.program_id(2)

    @pl.when(i == 0)
    def _():
        for j in range(k_ref.shape[0] // bs):
            kb_s[j * bs:(j + 1) * bs, :] = k_ref[j * bs:(j + 1) * bs, :].astype(BF16)
            vb_s[j * bs:(j + 1) * bs, :] = v_ref[j * bs:(j + 1) * bs, :].astype(BF16)

    lane = lax.broadcasted_iota(jnp.int32, (1, LANES), 1)
    tri = _suffix_matrix(bs)
    strict = (lax.broadcasted_iota(jnp.int32, (2 * bs, bs), 1)
              < lax.broadcasted_iota(jnp.int32, (2 * bs, bs), 0) % bs)
    qf = q_ref[...]
    qhs = []
    for hd in range(2):
        lm = (lane >= HEAD_DIM * hd) & (lane < HEAD_DIM * (hd + 1))
        qhs.append(jnp.where(lm, qf, 0.0))
    qs2 = (jnp.concatenate(qhs, axis=0) * HEAD_DIM ** -0.5).astype(BF16)

    def block(j, r, mask):
        c0 = pl.multiple_of(j * bs, bs)
        w, r = _sb_weights(_dot_nt(qs2, kb_s[pl.ds(c0, bs), :]), r, tri, mask)
        return _dot(w, vb_s[pl.ds(c0, bs), :]), r

    acc, r = block(i, jnp.zeros((2 * bs, 1), F32), strict)
    has_prev = lax.broadcasted_iota(jnp.int32, (2 * bs, bs), 0) < jnp.where(i > 0, 2 * bs, 0)
    d, r = block(jnp.maximum(i - 1, 0), r, has_prev)
    acc = acc + d

    def cond(c):
        j, r, _ = c
        return (j >= 0) & (jnp.max(r) > EXP_UNDERFLOW)

    def body(c):
        j, r, acc = c
        d, r = block(j, r, None)
        return j - 1, r, acc + d

    _, _, acc = lax.while_loop(cond, body, (i - 2, r, acc))
    o_ref[...] = jnp.where(lane < HEAD_DIM, acc[:bs], acc[bs:])


def _sb_prompt(q, k, v, *, bsz, t):
    assert t % SB_BLOCK == 0
    n = bsz * t
    nq = t // SB_BLOCK
    q_spec = pl.BlockSpec((SB_BLOCK, LANES), lambda b, hp, i: (b * nq + i, hp))
    kv_spec = pl.BlockSpec((t, LANES), lambda b, hp, i: (b, hp))
    return pl.pallas_call(
        _sb_prompt_kernel,
        grid=(bsz, ATT_WIDTH // LANES, nq),
        in_specs=[q_spec, kv_spec, kv_spec], out_specs=q_spec,
        out_shape=jax.ShapeDtypeStruct((n, ATT_WIDTH), F32),
        scratch_shapes=[pltpu.VMEM((t, LANES), BF16), pltpu.VMEM((t, LANES), BF16)],
        name="sb_prompt")(q, k, v)


def _sb_sample_kernel(pt_ref, q_ref, a_ref, b_ref, *rest, t, first):
    pps = PAGES_PER_STEP
    k_refs = rest[:pps]
    v_refs = rest[pps:2 * pps]
    n_out = 3 if first else 1
    o_ref = rest[2 * pps]
    qexp_s, new_s, r_s, acc_s = rest[2 * pps + n_out:]
    s_id = pl.program_id(1)
    rows = HEADS * t
    tri = _suffix_matrix(PAGE)

    @pl.when(s_id == 0)
    def _():
        _expand_heads(q_ref[...] * HEAD_DIM ** -0.5, qexp_s)
        if first:
            z = _dot_nt(qexp_s[...].astype(BF16), _pad_rows(a_ref, new_s))
            col = lax.broadcasted_iota(jnp.int32, (rows, PAGE), 1)
            rowt = lax.broadcasted_iota(jnp.int32, (rows, PAGE), 0) % t
            w, r = _sb_weights(z, jnp.zeros((rows, 1), F32), tri, col < rowt)
            acc_s[...] = _dot(w, _pad_rows(b_ref, new_s))
            r_s[...] = jnp.broadcast_to(r, r_s.shape)
        else:
            acc_s[...] = a_ref[...]
            r_s[...] = b_ref[...]

    for i in range(pps):
        @pl.when(jnp.max(r_s[...]) > EXP_UNDERFLOW)
        def _(i=i):
            z = _dot(qexp_s[...].astype(BF16), k_refs[i][...].astype(BF16))
            w, r = _sb_weights(z, r_s[:, 0:1], tri, None)
            acc_s[...] = acc_s[...] + _dot_nt(w, v_refs[i][...].astype(BF16))
            r_s[...] = jnp.broadcast_to(r, r_s.shape)

    @pl.when(s_id == pl.num_programs(1) - 1)
    def _():
        o_ref[...] = _collapse_heads(acc_s[...], t)
        if first:
            rest[2 * pps + 1][...] = acc_s[...]
            rest[2 * pps + 2][...] = r_s[...]


def _sb_sample(q, k_new, v_new, k_pool, v_pool, page_table, *, bsz, t):
    n_pages = page_table.shape[1]
    pps = PAGES_PER_STEP
    assert n_pages % pps == 0 and t <= SUBLANES
    rows = HEADS * t
    n = bsz * t
    scratch = [pltpu.VMEM((rows, ATT_WIDTH), F32), pltpu.VMEM((PAGE, ATT_WIDTH), F32),
               pltpu.VMEM((rows, LANES), F32), pltpu.VMEM((rows, ATT_WIDTH), F32)]
    acc_spec = pl.BlockSpec((rows, ATT_WIDTH), lambda b, s, pt: (b, 0))
    r_spec = pl.BlockSpec((rows, LANES), lambda b, s, pt: (b, 0))
    y_sds = jax.ShapeDtypeStruct((n, ATT_WIDTH), F32)
    pools = [k_pool] * pps + [v_pool] * pps

    in_specs, y_spec = _sample_specs(page_table, n_pages, t, lambda s, i, which: n_pages - 1 - i)
    y, acc, r = pl.pallas_call(
        functools.partial(_sb_sample_kernel, t=t, first=True),
        grid_spec=pltpu.PrefetchScalarGridSpec(
            num_scalar_prefetch=1, grid=(bsz, 1), in_specs=in_specs,
            out_specs=[y_spec, acc_spec, r_spec], scratch_shapes=scratch),
        out_shape=[y_sds, jax.ShapeDtypeStruct((bsz * rows, ATT_WIDTH), F32),
                   jax.ShapeDtypeStruct((bsz * rows, LANES), F32)],
        name="sb_sample")(page_table, q, k_new, v_new, *pools)
    if n_pages == pps:
        return y

    def older_pages():
        specs, _ = _sample_specs(page_table, n_pages, t,
                                 lambda s, i, which: n_pages - 1 - pps - (s * pps + i))
        specs = [specs[0], acc_spec, r_spec] + specs[3:]
        return pl.pallas_call(
            functools.partial(_sb_sample_kernel, t=t, first=False),
            grid_spec=pltpu.PrefetchScalarGridSpec(
                num_scalar_prefetch=1, grid=(bsz, n_pages // pps - 1), in_specs=specs,
                out_specs=y_spec, scratch_shapes=scratch),
            out_shape=y_sds,
            name="sb_sample_older")(page_table, q, acc, r, *pools)

    return lax.cond(jnp.max(r) > EXP_UNDERFLOW, older_pages, lambda: y)


def _hgrn_kernel(q_ref, f_ref, i_ref, g_ref, lb_ref, s0_ref, gn_ref, o_ref, s_out_ref, st_s,
                 *, layer):
    tb = q_ref.shape[0]
    t = pl.program_id(1)

    @pl.when(t == 0)
    def _():
        for hh in range(HGRN_HEADS):
            st_s[hh] = s0_ref[0, hh].T

    lbp = lb_ref[...]
    e = jnp.exp(lbp - jnp.max(lbp, axis=0, keepdims=True))
    soft = e / jnp.sum(e, axis=0, keepdims=True)
    lbv = jnp.sum(soft[1:layer + 1], axis=0, keepdims=True)

    pad = max(HGRN_SUB - tb, 0)
    L = min(HGRN_CHUNK, tb + pad)
    c = min(HGRN_SUB, L)
    tri = (lax.broadcasted_iota(jnp.int32, (L, L), 0)
           >= lax.broadcasted_iota(jnp.int32, (L, L), 1)).astype(BF16)
    rowc = lax.broadcasted_iota(jnp.int32, (c, 1), 0)
    gn = gn_ref[...]

    def padrows(x):
        if pad == 0:
            return x
        return jnp.concatenate([x, jnp.zeros((pad, x.shape[1]), F32)], axis=0)

    def chunk(rows, n_valid):
        for hh in range(HGRN_HEADS):
            ls = slice(hh * HGRN_DIM, (hh + 1) * HGRN_DIM)
            lbh = lbv[:, ls]
            f = lbh + (1.0 - lbh) * _sigmoid(f_ref[rows, ls])
            lf = padrows(jnp.log(f))
            kk = padrows(1.0 - f)
            qv = padrows(q_ref[rows, ls])
            vv = padrows(i_ref[rows, ls])
            hi, mid, lo = _split3(lf)
            b = _dot(tri, hi) + _dot(tri, mid) + _dot(tri, lo)
            st = st_s[hh]
            o_inter = _dot_nt((qv * jnp.exp(b)).astype(BF16), st.astype(BF16))
            vb = vv.astype(BF16)
            parts = []
            for si in range(L // c):
                rs = slice(si * c, (si + 1) * c)
                b_i = b[rs]
                q_i = qv[rs]
                k_i = kk[rs]
                v_i = vv[rs]
                o_i = o_inter[rs]
                if si > 0:
                    b_prev = b[si * c - 1:si * c]
                    qt = (q_i * jnp.exp(b_i - b_prev)).astype(BF16)
                    kt = (kk[:si * c] * jnp.exp(b_prev - b[:si * c])).astype(BF16)
                    o_i = o_i + _dot(_dot_nt(qt, kt).astype(BF16), vb[:si * c])
                for s in range(min(c, max(n_valid - si * c, 0))):
                    dec = jnp.exp(jnp.minimum(b_i - b_i[s:s + 1], 0.0))
                    a = jnp.sum(q_i * k_i[s:s + 1] * dec, axis=-1, keepdims=True)
                    o_i = o_i + jnp.where(rowc >= s, a, 0.0) * v_i[s:s + 1]
                parts.append(o_i)
            o = jnp.concatenate(parts, axis=0) if len(parts) > 1 else parts[0]
            o = o[:n_valid]
            o = o * lax.rsqrt(jnp.mean(o * o, axis=-1, keepdims=True) + EPS) * gn
            gv = g_ref[rows, ls]
            o_ref[rows, ls] = o * (gv * _sigmoid(gv))
            b_last = b[L - 1:L]
            kdec = (kk * jnp.exp(b_last - b)).astype(BF16)
            st_s[hh] = st * jnp.exp(b_last) + _dot_tn(vb, kdec)

    if pad:
        chunk(slice(0, tb), tb)
    else:
        def body(ch, _):
            chunk(pl.ds(pl.multiple_of(ch * L, L), L), L)
            return 0

        lax.fori_loop(0, tb // L, body, 0)

    for hh in range(HGRN_HEADS):
        s_out_ref[0, hh] = st_s[hh].T


def _hgrn(qh, fh, ih, gh, lb, s0, gnorm, *, bsz, t, layer):
    n = bsz * t
    tb = min(ROW_TILE, t)
    nt = t // tb
    row_spec = pl.BlockSpec((tb, SLAB), lambda b, i: (b * nt + i, 0))
    st_spec = pl.BlockSpec((1, HGRN_HEADS, HGRN_DIM, HGRN_DIM), lambda b, i: (b, 0, 0, 0))
    return pl.pallas_call(
        functools.partial(_hgrn_kernel, layer=layer),
        grid=(bsz, nt),
        in_specs=[row_spec] * 4 + [_const(lb.shape), st_spec, _const((1, HGRN_DIM))],
        out_specs=[row_spec, st_spec],
        out_shape=[jax.ShapeDtypeStruct((n, SLAB), F32),
                   jax.ShapeDtypeStruct((bsz, HGRN_HEADS, HGRN_DIM, HGRN_DIM), F32)],
        scratch_shapes=[pltpu.VMEM((HGRN_HEADS, HGRN_DIM, HGRN_DIM), F32)],
        name="hgrn2")(qh, fh, ih, gh, lb, s0, gnorm.reshape(1, HGRN_DIM))


def _tail_kernel(*refs, chain):
    if chain:
        (h_ref, ya_ref, yb_ref, p_ref, wo_ref, gmp_ref, gfp_ref, wug_ref, wuv_ref, wcv_ref, wdn_ref,
         gfo_ref, wpp_ref, wpg_ref, gpl_ref, ho_ref, ffn_ref, acc_s, car_s) = refs
    else:
        (h_ref, ya_ref, yb_ref, p_ref, p1_ref, p2_ref, wo_ref, gmp_ref, gfp_ref, wug_ref, wuv_ref,
         wcv_ref, wdn_ref, gfo_ref, wpp_ref, wpg_ref, gpl_ref, ho_ref, ffn_ref, acc_s) = refs
    tm = h_ref.shape[0]
    nch = wug_ref.shape[0]
    cw = wug_ref.shape[2]
    mix = _dot(ya_ref[...].astype(BF16), wo_ref[0]) + _dot(yb_ref[...].astype(BF16), wo_ref[1])
    h1 = h_ref[...] + _rms(mix, gmp_ref[...])
    a = _rms(h1, gfp_ref[...]).astype(BF16)
    acc_s[...] = jnp.zeros(acc_s.shape, F32)
    rowi = lax.broadcasted_iota(jnp.int32, (tm, cw), 0)

    if chain:
        @pl.when(pl.program_id(1) == 0)
        def _():
            car_s[...] = jnp.zeros(car_s.shape, F32)

    def conv(u, prev, w):
        r1 = pltpu.roll(u, 1, 0)
        r2 = pltpu.roll(u, 2, 0)
        if chain:
            last = prev[SUBLANES - 1:SUBLANES]
            last2 = prev[SUBLANES - 2:SUBLANES - 1]
            p1 = jnp.where(rowi == 0, last, r1)
            p2 = jnp.where(rowi == 0, last2, jnp.where(rowi == 1, last, r2))
        else:
            rm = rowi % SUBLANES
            p1 = jnp.where(rm == 0, prev[0], r1)
            p2 = jnp.where(rm < 2, prev[1], r2)
        return w[3:4] + w[0:1] * p2 + w[1:2] * p1 + w[2:3] * u

    def chunk(c, _):
        ug = _dot(a, wug_ref[c])
        uv = _dot(a, wuv_ref[c])
        w = wcv_ref[c]
        if chain:
            cg = conv(ug, car_s[c], w[0:4])
            cv = conv(uv, car_s[nch + c], w[4:8])
            car_s[c] = ug[tm - SUBLANES:tm]
            car_s[nch + c] = uv[tm - SUBLANES:tm]
            ffn_ref[0, c] = ug[tm - SUBLANES:tm]
            ffn_ref[0, nch + c] = uv[tm - SUBLANES:tm]
        else:
            cg = conv(ug, (p1_ref[c], p2_ref[c]), w[0:4])
            cv = conv(uv, (p1_ref[nch + c], p2_ref[nch + c]), w[4:8])
            ffn_ref[c] = ug
            ffn_ref[nch + c] = uv
        act = (_gelu(cg) * cv).astype(BF16)
        acc_s[...] += _dot(act, wdn_ref[c])
        return 0

    lax.fori_loop(0, nch, chunk, 0, unroll=True)
    h2 = h1 + _rms(acc_s[...], gfo_ref[...])
    ple = _dot(p_ref[...].astype(BF16), wpp_ref[...]) * _sigmoid(_dot(h2.astype(BF16), wpg_ref[...]))
    ho_ref[...] = h2 + _rms(ple, gpl_ref[...])


def _tail(h, ya, yb, p, ffn0, w, *, bsz, t):
    n, d = h.shape
    nch = N_FF_CHUNKS
    cw = FF_CHUNK
    chain = t > SUBLANES
    consts = [w['wo'], w['gmp'], w['gfp'], w['wug'], w['wuv'], w['wcv'], w['wdn'], w['gfo'], w['wpp'],
              w['wpg'], w['gpl']]
    const_specs = [_const(x.shape) for x in consts]
    if chain:
        assert ffn0 is None, "a long sequence starts from an empty ConvFFN buffer"
        tm = min(MATMUL_TILE, t)
        nt = t // tm
        grid = (bsz, nt)
        rows = lambda width: pl.BlockSpec((tm, width), lambda b, i: (b * nt + i, 0))
        ffn_spec = pl.BlockSpec((1, 2 * nch, SUBLANES, cw), lambda b, i: (b, 0, 0, 0))
        ffn_sds = jax.ShapeDtypeStruct((bsz, 2 * nch, SUBLANES, cw), F32)
        extra, extra_specs = [], []
        scratch = [pltpu.VMEM((tm, d), F32), pltpu.VMEM((2 * nch, SUBLANES, cw), F32)]
    else:
        assert t == SUBLANES
        tm = n
        grid = (1, 1)
        rows = lambda width: pl.BlockSpec((tm, width), lambda b, i: (0, 0))
        ffn_spec = _full((2 * nch, tm, cw))
        ffn_sds = jax.ShapeDtypeStruct((2 * nch, tm, cw), F32)
        buf = ffn0.astype(F32).reshape(bsz, 2, 2 * nch, cw).transpose(2, 0, 1, 3)
        zero = jnp.zeros((2 * nch, bsz, SUBLANES - 2, cw), F32)
        p1 = jnp.concatenate([buf[:, :, 1:2], zero, zero[:, :, :1]], axis=2).reshape(2 * nch, tm, cw)
        p2 = jnp.concatenate([buf, zero], axis=2).reshape(2 * nch, tm, cw)
        extra = [p1, p2]
        extra_specs = [_full(p1.shape), _full(p2.shape)]
        scratch = [pltpu.VMEM((tm, d), F32)]
    ho, ffn = pl.pallas_call(
        functools.partial(_tail_kernel, chain=chain),
        grid=grid,
        in_specs=[rows(d), rows(SLAB), rows(SLAB), rows(p.shape[1])] + extra_specs + const_specs,
        out_specs=[rows(d), ffn_spec],
        out_shape=[jax.ShapeDtypeStruct((n, d), F32), ffn_sds],
        scratch_shapes=scratch,
        name="layer_tail")(h, ya, yb, p, *extra, *consts)
    if chain:
        st = ffn[:, :, SUBLANES - 2:, :]
    else:
        st = ffn.reshape(2 * nch, bsz, SUBLANES, cw)[:, :, SUBLANES - 2:, :].transpose(1, 0, 2, 3)
    return ho, st.transpose(0, 2, 1, 3).reshape(bsz, 2, 2 * nch * cw)


def _tail_weights(i, w_out, g_mix_post, g_ffn_pre, g_ffn_post, w_ffn_up, w_ffn_conv, b_ffn_conv,
                  w_ffn_down, w_ple_proj, w_ple_gate, g_ple):
    d = w_out.shape[1]
    nch, cw = N_FF_CHUNKS, FF_CHUNK
    up = w_ffn_up[i].astype(BF16).reshape(d, 2, nch, cw).transpose(1, 2, 0, 3)
    taps = jnp.concatenate([w_ffn_conv[i].astype(F32), b_ffn_conv[i].astype(F32)[None]], axis=0)
    wcv = taps.reshape(4, 2, nch, cw).transpose(2, 1, 0, 3).reshape(nch, 8, cw)
    row = lambda g: g.astype(F32).reshape(1, d)
    return dict(
        wo=w_out.astype(BF16).reshape(2, SLAB, d), gmp=row(g_mix_post[i]), gfp=row(g_ffn_pre[i]),
        wug=up[0], wuv=up[1], wcv=wcv, wdn=w_ffn_down[i].astype(BF16).reshape(nch, cw, d),
        gfo=row(g_ffn_post[i]), wpp=w_ple_proj[i].astype(BF16), wpg=w_ple_gate[i].astype(BF16),
        gpl=row(g_ple[i]))


def kernel(x_prompt, x_sample, p_prompt, p_sample, cache_moba_k, cache_moba_v, state_s5_re, state_s5_im,
           state_hgrn, cache_sb_k, cache_sb_v, state_ffn, page_table,
           g_mix_pre, g_mix_post, g_ffn_pre, g_ffn_post, w_ffn_up, w_ffn_conv, b_ffn_conv, w_ffn_down,
           w_ple_proj, w_ple_gate, g_ple,
           w_in_a, w_out_a, s5_a_re, s5_a_im, s5_log_dt, s5_b_re, s5_b_im, s5_c_re, s5_c_im, s5_d,
           s5_w_glu, s5_b_glu,
           w_in_c, w_out_c, hgrn_lb, g_hgrn_norm):
    depth = g_mix_pre.shape[0]
    d_model = x_prompt.shape[-1]
    n_pages = page_table.shape[1]
    past_len = n_pages * cache_moba_k.shape[2]
    page_table = page_table.astype(jnp.int32)

    layers = []
    for i in range(depth):
        j = i // 2
        lw = {}
        if i % 2 == 0:
            lw['w_in'] = w_in_a[j].astype(BF16)
            lw['disc'] = _s5_discretize(s5_a_re[j], s5_a_im[j], s5_log_dt[j])
            lw['wb'], lw['wc'] = _s5_weights(s5_b_re[j], s5_b_im[j], s5_c_re[j], s5_c_im[j])
            lw['d'] = s5_d[j].astype(F32).reshape(S5_WIDTH)
            lw['wglu'] = s5_w_glu[j].astype(BF16)
            lw['bglu'] = s5_b_glu[j].astype(F32)
            w_out = w_out_a[j]
        else:
            lw['w_in'] = w_in_c[j].astype(BF16)
            w_out = w_out_c[j]
        lw['tail'] = _tail_weights(i, w_out, g_mix_post, g_ffn_pre, g_ffn_post, w_ffn_up, w_ffn_conv,
                                   b_ffn_conv, w_ffn_down, w_ple_proj, w_ple_gate, g_ple)
        layers.append(lw)

    def pool2d(pool, j):
        return pool[j].transpose(0, 2, 3, 1).reshape(pool.shape[1], ATT_WIDTH, pool.shape[2])

    def run(x, p, q_start, s5_re0, s5_im0, hgrn0, ffn0, has_past):
        bsz, t, _ = x.shape
        n = bsz * t
        h = x.astype(F32).reshape(n, d_model)
        tables = _rope_tables(t, q_start)
        if t < MATMUL_TILE:
            tables = tuple(jnp.tile(tb, (n // t, 1)) for tb in tables)
        mk, mv, sr, si, hs, sk, sv, fb = [], [], [], [], [], [], [], []
        long_seq = t % MATMUL_TILE == 0

        def kv_out(x, x_t, which):
            if x_t:
                return x_t[which].reshape(bsz, HEADS, HEAD_DIM, t).transpose(0, 3, 1, 2)
            return x.reshape(bsz, t, HEADS, HEAD_DIM)

        for i in range(depth):
            j = i // 2
            lw = layers[i]
            g_pre = g_mix_pre[i].astype(F32)
            if i % 2 == 0:
                u, q, k, v, *kv_t = _inproj(h, g_pre, lw['w_in'], tables, rope_slabs=(1, 2),
                                            t_slabs=(2, 3) if long_seq else (), seq=(bsz, t))
                y_a, hr, hi = _s5(u, s5_re0[j].astype(F32).reshape(bsz, 1, S5_LANES),
                                  s5_im0[j].astype(F32).reshape(bsz, 1, S5_LANES), lw['disc'], lw['wb'],
                                  lw['wc'], lw['d'], lw['wglu'], lw['bglu'], bsz=bsz, t=t)
                if has_past:
                    y_b = _moba_sample(q, k, v, pool2d(cache_moba_k, j), pool2d(cache_moba_v, j),
                                       page_table, bsz=bsz, t=t)
                else:
                    y_b = _moba_prompt(q, k, v, bsz=bsz, t=t)
                mk.append(kv_out(k, kv_t, 0))
                mv.append(kv_out(v, kv_t, 1))
                sr.append(hr.reshape(bsz, S5_GROUPS, S5_STATE))
                si.append(hi.reshape(bsz, S5_GROUPS, S5_STATE))
            else:
                qh, fh, ih, gh, q, k, v, *kv_t = _inproj(h, g_pre, lw['w_in'], tables, rope_slabs=(),
                                                         t_slabs=(5, 6) if long_seq else (), seq=(bsz, t))
                y_a, s_fin = _hgrn(qh, fh, ih, gh, hgrn_lb.astype(F32), hgrn0[j].astype(F32),
                                   g_hgrn_norm[j].astype(F32), bsz=bsz, t=t, layer=i)
                if has_past:
                    y_b = _sb_sample(q, k, v, pool2d(cache_sb_k, j), pool2d(cache_sb_v, j), page_table,
                                     bsz=bsz, t=t)
                else:
                    y_b = _sb_prompt(q, k, v, bsz=bsz, t=t)
                hs.append(s_fin)
                sk.append(kv_out(k, kv_t, 0))
                sv.append(kv_out(v, kv_t, 1))
            h, buf = _tail(h, y_a, y_b, p[i].astype(F32).reshape(n, p.shape[-1]),
                           None if ffn0 is None else ffn0[i], lw['tail'], bsz=bsz, t=t)
            fb.append(buf)
        return (h.reshape(bsz, t, d_model), jnp.stack(mk), jnp.stack(mv), jnp.stack(sr), jnp.stack(si),
                jnp.stack(hs), jnp.stack(sk), jnp.stack(sv), jnp.stack(fb))

    bp = x_prompt.shape[0]
    n_a = (depth + 1) // 2
    n_c = depth // 2
    outs_p = run(x_prompt, p_prompt, 0,
                 jnp.zeros((n_a, bp, S5_GROUPS, S5_STATE), F32), jnp.zeros((n_a, bp, S5_GROUPS, S5_STATE), F32),
                 jnp.zeros((n_c, bp, HGRN_HEADS, HGRN_DIM, HGRN_DIM), F32), None, False)
    outs_s = run(x_sample, p_sample, past_len, state_s5_re, state_s5_im, state_hgrn, state_ffn, True)
    return (outs_p[0], outs_s[0]) + tuple(outs_p[1:]) + tuple(outs_s[1:])
```

```python
import functools
import math

import jax
import jax.numpy as jnp
from jax import lax
from jax.experimental import pallas as pl
from jax.experimental.pallas import tpu as pltpu

F32 = jnp.float32
BF16 = jnp.bfloat16

S5_WIDTH = 512
S5_GROUPS = 32
S5_GROUP = 16
S5_STATE = 64
S5_LANES = S5_GROUPS * S5_STATE
HEADS = 8
HEAD_DIM = 64
ATT_WIDTH = HEADS * HEAD_DIM
ROPE_DIM = 16
ROPE_THETA = 500000.0
MOBA_BLOCK = 256
MOBA_TOPK = 3
HGRN_HEADS = 4
HGRN_DIM = 128
HGRN_CHUNK = 64
HGRN_SUB = 16
D_FF = 2816
FF_CHUNK = 256
N_FF_CHUNKS = D_FF // FF_CHUNK
PAGE = 128
EPS = 1e-6
NEG = -1e30
SLAB = 512
LANES = 128
SUBLANES = 8
ROW_TILE = 256
MATMUL_TILE = 512
CONV_ROWS = 128
SB_BLOCK = 256
PAGES_PER_STEP = 16
SB_FIRST_PAGES = 4
LOG2E = 1.4426950408889634
EXP_UNDERFLOW = -104.0


def _dot(a, b):
    return jnp.dot(a, b, preferred_element_type=F32)


def _dot_nt(a, b):
    return lax.dot_general(a, b, (((1,), (1,)), ((), ())), preferred_element_type=F32)


def _dot_tn(a, b):
    return lax.dot_general(a, b, (((0,), (0,)), ((), ())), preferred_element_type=F32)


def _split3(x):
    hi = x.astype(BF16)
    r1 = x - hi.astype(F32)
    mid = r1.astype(BF16)
    lo = (r1 - mid.astype(F32)).astype(BF16)
    return hi, mid, lo


def _rms(x, g):
    return x * lax.rsqrt(jnp.mean(x * x, axis=-1, keepdims=True) + EPS) * g


def _sigmoid(x):
    return 1.0 / (1.0 + jnp.exp(-x))


def _gelu(x):
    return 0.5 * x * (1.0 + jnp.tanh(0.7978845608028654 * (x + 0.044715 * (x * x * x))))


def _full(shape):
    nd = len(shape)
    return pl.BlockSpec(shape, lambda *_: (0,) * nd)


def _const(shape):
    nd = len(shape)
    return pl.BlockSpec(shape, lambda *_: (0,) * nd, pipeline_mode=pl.Buffered(1))


def _rope_table_kernel(c_ref, s1_ref, s2_ref, *, q_start):
    rows, lanes = c_ref.shape
    i = pl.program_id(0)
    lane = lax.broadcasted_iota(jnp.int32, (rows, lanes), 1)
    pos = lax.broadcasted_iota(jnp.int32, (rows, lanes), 0) + i * rows + q_start
    d = lane % HEAD_DIM
    half = ROPE_DIM // 2
    inv = jnp.exp((d % half).astype(F32) * (-math.log(ROPE_THETA) / half))
    ang = pos.astype(F32) * inv
    cs = jnp.cos(ang)
    sn = jnp.sin(ang)
    c_ref[...] = jnp.where(d < ROPE_DIM, cs, 1.0)
    s1_ref[...] = jnp.where(d < half, -sn, 0.0)
    s2_ref[...] = jnp.where((d >= half) & (d < ROPE_DIM), sn, 0.0)


def _rope_tables(t, q_start):
    rows = min(t, 512)
    sds = jax.ShapeDtypeStruct((t, LANES), F32)
    spec = pl.BlockSpec((rows, LANES), lambda i: (i, 0))
    return pl.pallas_call(
        functools.partial(_rope_table_kernel, q_start=q_start),
        grid=(t // rows,), out_shape=(sds, sds, sds), out_specs=(spec, spec, spec),
        name="rope_tables")()


def _inproj_kernel(h_ref, g_ref, w_ref, c_ref, s1_ref, s2_ref, *out_refs, n_slabs, rope_slabs,
                   t_slabs):
    a = _rms(h_ref[...], g_ref[...]).astype(BF16)
    half = ROPE_DIM // 2
    for s in range(n_slabs):
        o_ref = out_refs[s]
        z = _dot(a, w_ref[:, s * SLAB:(s + 1) * SLAB])
        if s in rope_slabs:
            c = c_ref[...]
            s1 = s1_ref[...]
            s2 = s2_ref[...]
            for j in range(SLAB // LANES):
                x = z[:, j * LANES:(j + 1) * LANES]
                o_ref[:, j * LANES:(j + 1) * LANES] = (
                    x * c + pltpu.roll(x, LANES - half, 1) * s1 + pltpu.roll(x, half, 1) * s2)
        else:
            o_ref[...] = z
        if s in t_slabs:
            out_refs[n_slabs + t_slabs.index(s)][...] = o_ref[...].T


def _inproj(h, g, w, tables, rope_slabs, t_slabs=(), seq=None):
    n, d = h.shape
    n_slabs = w.shape[1] // SLAB
    tm = min(MATMUL_TILE, n)
    t_tiles = tables[0].shape[0] // tm
    tab_spec = pl.BlockSpec((tm, LANES), lambda i: (i % t_tiles, 0))
    out_spec = pl.BlockSpec((tm, SLAB), lambda i: (i, 0))
    out_specs = [out_spec] * n_slabs
    out_shape = [jax.ShapeDtypeStruct((n, SLAB), F32)] * n_slabs
    if t_slabs:
        bsz, t = seq
        assert t % tm == 0
        nt = t // tm
        out_specs += [pl.BlockSpec((None, SLAB, tm), lambda i: (i // nt, 0, i % nt))] * len(t_slabs)
        out_shape += [jax.ShapeDtypeStruct((bsz, SLAB, t), F32)] * len(t_slabs)
    return pl.pallas_call(
        functools.partial(_inproj_kernel, n_slabs=n_slabs, rope_slabs=rope_slabs, t_slabs=t_slabs),
        grid=(n // tm,),
        in_specs=[pl.BlockSpec((tm, d), lambda i: (i, 0)), _const((1, d)), _const(w.shape),
                  tab_spec, tab_spec, tab_spec],
        out_specs=out_specs, out_shape=out_shape,
        name="inproj")(h, g.reshape(1, d), w, *tables)


def _s5_disc_kernel(ar_ref, ai_ref, ldt_ref, pre_ref, pim_ref, cc_ref):
    ar = ar_ref[...]
    ai = ai_ref[...]
    dt = jnp.exp(ldt_ref[...])
    row = lax.broadcasted_iota(jnp.int32, (4 * SUBLANES, 1), 0)
    blk = row // SUBLANES
    r = row % SUBLANES
    n = jnp.where(blk == 0, r + 1, jnp.where(blk == 1, 1, jnp.where(blk == 2, 2, 4)))
    keep = (blk == 0) | (r >= n)
    n = n.astype(F32)
    mag = jnp.exp(ar * dt * n)
    ang = ai * dt * n
    pre = mag * jnp.cos(ang)
    pim = mag * jnp.sin(ang)
    pre_ref[...] = jnp.where(keep, pre, 0.0)
    pim_ref[...] = jnp.where(keep, pim, 0.0)
    abr = pre[0:1]
    abi = pim[0:1]
    den = ar * ar + ai * ai
    nr = abr - 1.0
    cc_ref[0:1, :] = (nr * ar + abi * ai) / den
    cc_ref[1:2, :] = (abi * ar - nr * ai) / den


def _s5_discretize(a_re, a_im, log_dt):
    ar = a_re.reshape(1, S5_LANES)
    ai = a_im.reshape(1, S5_LANES)
    ldt = jnp.repeat(log_dt, S5_STATE).reshape(1, S5_LANES)
    return pl.pallas_call(
        _s5_disc_kernel,
        out_shape=(jax.ShapeDtypeStruct((4 * SUBLANES, S5_LANES), F32),
                   jax.ShapeDtypeStruct((4 * SUBLANES, S5_LANES), F32),
                   jax.ShapeDtypeStruct((2, S5_LANES), F32)),
        name="s5_discretize")(ar, ai, ldt)


SCAN_LANES = 512


def _s5_kernel(u_ref, h0r_ref, h0i_ref, pre_ref, pim_ref, cc_ref, wb_ref, wc_ref, d_ref,
               wglu_ref, bglu_ref, y_ref, hr_out_ref, hi_out_ref, xr_s, xi_s, car_s, *, chain):
    tm = u_ref.shape[0]
    nb = tm // SUBLANES
    u = u_ref[...]
    ub = u.astype(BF16)
    half_in = S5_WIDTH // 2
    half_st = S5_LANES // 2
    for hf in range(2):
        bu = _dot(ub[:, hf * half_in:(hf + 1) * half_in], wb_ref[hf])
        bur = bu[:, :half_st]
        bui = bu[:, half_st:]
        ls = slice(hf * half_st, (hf + 1) * half_st)
        cr = cc_ref[0:1, ls]
        ci = cc_ref[1:2, ls]
        xr_s[:, ls] = cr * bur - ci * bui
        xi_s[:, ls] = cr * bui + ci * bur

    if chain:
        t = pl.program_id(1)

        @pl.when(t == 0)
        def _():
            car_s[0:1, :] = h0r_ref[0]
            car_s[1:2, :] = h0i_ref[0]

    for lc in range(S5_LANES // SCAN_LANES):
        ls = slice(lc * SCAN_LANES, (lc + 1) * SCAN_LANES)

        def body(i, carry, ls=ls):
            r0 = pl.multiple_of(i * SUBLANES, SUBLANES)
            hr = xr_s[pl.ds(r0, SUBLANES), ls]
            hi = xi_s[pl.ds(r0, SUBLANES), ls]
            for step, k in enumerate((1, 2, 4)):
                ar = pre_ref[(step + 1) * SUBLANES:(step + 2) * SUBLANES, ls]
                ai = pim_ref[(step + 1) * SUBLANES:(step + 2) * SUBLANES, ls]
                sr = pltpu.roll(hr, k, 0)
                si = pltpu.roll(hi, k, 0)
                hr, hi = hr + ar * sr - ai * si, hi + ar * si + ai * sr
            if chain:
                cr_, ci_ = carry
            else:
                cr_ = h0r_ref[i, :, ls]
                ci_ = h0i_ref[i, :, ls]
            pr8 = pre_ref[0:SUBLANES, ls]
            pi8 = pim_ref[0:SUBLANES, ls]
            hr, hi = hr + pr8 * cr_ - pi8 * ci_, hi + pr8 * ci_ + pi8 * cr_
            xr_s[pl.ds(r0, SUBLANES), ls] = hr
            xi_s[pl.ds(r0, SUBLANES), ls] = hi
            last_r = hr[SUBLANES - 1:SUBLANES]
            last_i = hi[SUBLANES - 1:SUBLANES]
            if chain:
                return last_r, last_i
            hr_out_ref[i, :, ls] = last_r
            hi_out_ref[i, :, ls] = last_i
            return carry

        if chain:
            cr_, ci_ = lax.fori_loop(0, nb, body, (car_s[0:1, ls], car_s[1:2, ls]))
            car_s[0:1, ls] = cr_
            car_s[1:2, ls] = ci_
        else:
            lax.fori_loop(0, nb, body, 0)

    if chain:
        hr_out_ref[0] = car_s[0:1, :]
        hi_out_ref[0] = car_s[1:2, :]

    ys = []
    for hf in range(2):
        ls = slice(hf * half_st, (hf + 1) * half_st)
        hcat = jnp.concatenate([xr_s[:, ls], xi_s[:, ls]], axis=1).astype(BF16)
        ys.append(_dot(hcat, wc_ref[hf]))
    y = jnp.concatenate(ys, axis=1) + d_ref[...] * u
    y = _gelu(y)
    y_ref[...] = y * _sigmoid(_dot(y.astype(BF16), wglu_ref[...]) + bglu_ref[...])


def _s5_weights(b_re, b_im, c_re, c_im):
    gh = S5_GROUPS // 2
    eye = jnp.eye(gh, dtype=F32)

    def bmat(b):
        bt = b.astype(F32).transpose(0, 2, 1).reshape(2, gh, S5_GROUP, S5_STATE)
        return jnp.einsum('xghp,gk->xghkp', bt, eye).reshape(2, gh * S5_GROUP, gh * S5_STATE)

    def cmat(c):
        ct = c.astype(F32).transpose(0, 2, 1).reshape(2, gh, S5_STATE, S5_GROUP)
        return jnp.einsum('xgph,gk->xgpkh', ct, eye).reshape(2, gh * S5_STATE, gh * S5_GROUP)

    wb = jnp.concatenate([bmat(b_re), bmat(b_im)], axis=2).astype(BF16)
    wc = jnp.concatenate([cmat(c_re), -cmat(c_im)], axis=1).astype(BF16)
    return wb, wc


def _s5(u, h0r, h0i, disc, wb, wc, dvec, wglu, bglu, *, bsz, t):
    n = bsz * t
    pre, pim, cc = disc
    chain = t > SUBLANES
    if chain:
        tm = min(MATMUL_TILE, t)
        grid = (bsz, t // tm)
        row_spec = pl.BlockSpec((tm, S5_WIDTH), lambda b, i: (b * (t // tm) + i, 0))
        st_spec = pl.BlockSpec((1, 1, S5_LANES), lambda b, i: (b, 0, 0))
    else:
        assert t == SUBLANES
        tm = n
        grid = (1, 1)
        row_spec = pl.BlockSpec((tm, S5_WIDTH), lambda b, i: (0, 0))
        st_spec = pl.BlockSpec((bsz, 1, S5_LANES), lambda b, i: (0, 0, 0))
    st_sds = jax.ShapeDtypeStruct((bsz, 1, S5_LANES), F32)
    return pl.pallas_call(
        functools.partial(_s5_kernel, chain=chain),
        grid=grid,
        in_specs=[row_spec, st_spec, st_spec, _const(pre.shape), _const(pim.shape), _const(cc.shape),
                  _const(wb.shape), _const(wc.shape), _const((1, S5_WIDTH)), _const(wglu.shape),
                  _const((1, S5_WIDTH))],
        out_specs=[row_spec, st_spec, st_spec],
        out_shape=[jax.ShapeDtypeStruct((n, S5_WIDTH), F32), st_sds, st_sds],
        scratch_shapes=[pltpu.VMEM((tm, S5_LANES), F32), pltpu.VMEM((tm, S5_LANES), F32),
                        pltpu.VMEM((2, S5_LANES), F32)],
        name="s5_mixer")(u, h0r, h0i, pre, pim, cc, wb, wc, dvec.reshape(1, S5_WIDTH), wglu,
                         bglu.reshape(1, S5_WIDTH))


def _top_blocks(gate, n_past, col):
    g = jnp.where(col < n_past, gate, NEG)
    sel = jnp.zeros(gate.shape, F32)
    colf = col.astype(F32)
    for _ in range(MOBA_TOPK):
        m = jnp.max(g, axis=-1, keepdims=True)
        first = jnp.min(jnp.where(g == m, colf, float(LANES)), axis=-1, keepdims=True)
        pick = colf == first
        sel = jnp.where(pick & (m > 0.5 * NEG), 1.0, sel)
        g = jnp.where(pick, -3e38, g)
    return sel


def _top_blocks_t(gate, n_past):
    rowf = lax.broadcasted_iota(jnp.int32, gate.shape, 0).astype(F32)
    g = jnp.where(rowf < n_past, gate, NEG)
    sel = jnp.zeros(gate.shape, F32)
    for _ in range(MOBA_TOPK):
        m = jnp.max(g, axis=0, keepdims=True)
        first = jnp.min(jnp.where(g == m, rowf, float(LANES)), axis=0, keepdims=True)
        pick = rowf == first
        sel = jnp.where(pick & (m > 0.5 * NEG), 1.0, sel)
        g = jnp.where(pick, -3e38, g)
    return sel


def _moba_prompt_kernel(q_ref, k_ref, v_ref, o_ref, kb_s, vt_s, kmean_s, sel_s, *, nblk):
    bs = MOBA_BLOCK
    i = pl.program_id(2)

    @pl.when(i == 0)
    def _():
        kmean_s[...] = jnp.zeros(kmean_s.shape, F32)
        for j in range(nblk):
            kj = k_ref[j * bs:(j + 1) * bs, :]
            kb_s[j * bs:(j + 1) * bs, :] = kj.astype(BF16)
            kmean_s[j:j + 1, :] = jnp.sum(kj, axis=0, keepdims=True) * (1.0 / bs)
            vt_s[j] = v_ref[j * bs:(j + 1) * bs, :].T.astype(BF16)

    lane = lax.broadcasted_iota(jnp.int32, (1, LANES), 1)
    rowd = lax.broadcasted_iota(jnp.int32, (LANES, 1), 0)
    causal = (lax.broadcasted_iota(jnp.int32, (bs, 2 * bs), 0)
              <= lax.broadcasted_iota(jnp.int32, (bs, 2 * bs), 1) % bs)
    kmb = kmean_s[...].astype(BF16)
    qf = q_ref[...]
    r0 = pl.multiple_of(i * bs, bs)
    qhs = []
    for hd in range(2):
        lm = (lane >= HEAD_DIM * hd) & (lane < HEAD_DIM * (hd + 1))
        qhs.append(jnp.where(lm, qf, 0.0))
    qh2 = jnp.concatenate(qhs, axis=0)
    qs2 = (qh2 * (HEAD_DIM ** -0.5 * LOG2E)).astype(BF16)
    sel = _top_blocks_t(_dot_nt(kmb, qh2.astype(BF16)), i)
    for b in range(nblk):
        sel_s[b] = sel[b:b + 1, :]
    s = jnp.where(causal, _dot_nt(kb_s[pl.ds(r0, bs), :], qs2), NEG)
    m = jnp.max(s, axis=0, keepdims=True)
    p = jnp.exp2(s - m)
    carry = (m, jnp.sum(p, axis=0, keepdims=True), _dot(vt_s[i], p.astype(BF16)))

    def kvpair(jj, carry):
        m, l, acc = carry
        c0 = pl.multiple_of(jj * 2 * bs, 2 * bs)
        s = _dot_nt(kb_s[pl.ds(c0, 2 * bs), :], qs2)
        s0 = jnp.where(sel_s[2 * jj] > 0.5, s[:bs], NEG)
        s1 = jnp.where(sel_s[2 * jj + 1] > 0.5, s[bs:], NEG)
        m_new = jnp.maximum(m, jnp.maximum(jnp.max(s0, axis=0, keepdims=True),
                                           jnp.max(s1, axis=0, keepdims=True)))
        alpha = jnp.exp2(m - m_new)
        p0 = jnp.exp2(s0 - m_new)
        p1 = jnp.exp2(s1 - m_new)
        l = alpha * l + jnp.sum(p0, axis=0, keepdims=True) + jnp.sum(p1, axis=0, keepdims=True)
        acc = (alpha * acc + _dot(vt_s[2 * jj], p0.astype(BF16))
               + _dot(vt_s[2 * jj + 1], p1.astype(BF16)))
        return m_new, l, acc

    m, l, acc = lax.fori_loop(0, (i + 1) // 2, kvpair, carry)
    out2 = acc / l
    o_ref[...] = jnp.where(rowd < HEAD_DIM, out2[:, :bs], out2[:, bs:]).T


def _moba_prompt(q, k, v, *, bsz, t):
    nblk = t // MOBA_BLOCK
    assert t % MOBA_BLOCK == 0 and nblk % 2 == 0 and nblk <= LANES
    nblk_rows = -(-nblk // SUBLANES) * SUBLANES
    n = bsz * t
    q_spec = pl.BlockSpec((MOBA_BLOCK, LANES), lambda b, hp, i: (b * nblk + i, hp))
    kv_spec = pl.BlockSpec((t, LANES), lambda b, hp, i: (b, hp))
    return pl.pallas_call(
        functools.partial(_moba_prompt_kernel, nblk=nblk),
        grid=(bsz, ATT_WIDTH // LANES, nblk),
        in_specs=[q_spec, kv_spec, kv_spec], out_specs=q_spec,
        out_shape=jax.ShapeDtypeStruct((n, ATT_WIDTH), F32),
        scratch_shapes=[pltpu.VMEM((t, LANES), BF16),
                        pltpu.VMEM((nblk, LANES, MOBA_BLOCK), BF16),
                        pltpu.VMEM((nblk_rows, LANES), F32),
                        pltpu.VMEM((nblk, 1, 2 * MOBA_BLOCK), F32)],
        name="moba_prompt")(q, k, v)


def _expand_heads(q, qexp_s):
    t = q.shape[0]
    lane = lax.broadcasted_iota(jnp.int32, (1, ATT_WIDTH), 1)
    for h in range(HEADS):
        lm = (lane >= h * HEAD_DIM) & (lane < (h + 1) * HEAD_DIM)
        qexp_s[h * t:(h + 1) * t, :] = jnp.where(lm, q, 0.0)


def _collapse_heads(acc, t):
    lane = lax.broadcasted_iota(jnp.int32, (1, ATT_WIDTH), 1)
    out = jnp.zeros((t, ATT_WIDTH), F32)
    for h in range(HEADS):
        lm = (lane >= h * HEAD_DIM) & (lane < (h + 1) * HEAD_DIM)
        out = out + jnp.where(lm, acc[h * t:(h + 1) * t, :], 0.0)
    return out


def _pad_rows(x_ref, new_s):
    new_s[...] = jnp.zeros(new_s.shape, F32)
    new_s[0:x_ref.shape[0], :] = x_ref[...]
    return new_s[...].astype(BF16)


def _moba_sample_kernel(pt_ref, q_ref, kn_ref, vn_ref, *rest, n_pages, t):
    pps = PAGES_PER_STEP
    k_refs = rest[:pps]
    v_refs = rest[pps:2 * pps]
    o_ref = rest[2 * pps]
    qexp_s, new_s, s_all, gate_s, bmax_s, sel_s, m_s, l_s, acc_s = rest[2 * pps + 1:]
    ns = n_pages // pps
    s_id = pl.program_id(1)
    rows = HEADS * t
    col = lax.broadcasted_iota(jnp.int32, (rows, LANES), 1)
    scale = HEAD_DIM ** -0.5
    pages_per_block = MOBA_BLOCK // PAGE

    @pl.when(s_id == 0)
    def _():
        _expand_heads(q_ref[...], qexp_s)
        gate_s[...] = jnp.zeros(gate_s.shape, F32)
        bmax_s[...] = jnp.full(bmax_s.shape, NEG, F32)

    @pl.when(s_id < ns)
    def _():
        qe = qexp_s[...].astype(BF16)
        g = gate_s[...]
        bm = bmax_s[...]
        for i in range(pps):
            pg = s_id * pps + i
            s = _dot(qe, k_refs[i][...].astype(BF16))
            s_all[pg] = s
            mine = col == pg // pages_per_block
            g = g + jnp.where(mine, jnp.sum(s, axis=-1, keepdims=True) * (1.0 / MOBA_BLOCK), 0.0)
            bm = jnp.where(mine, jnp.maximum(bm, jnp.max(s, axis=-1, keepdims=True)), bm)
        gate_s[...] = g
        bmax_s[...] = bm

    @pl.when(s_id == ns)
    def _():
        sel = _top_blocks(gate_s[...], n_pages // pages_per_block, col)
        sel_s[...] = sel
        qe = qexp_s[...].astype(BF16)
        sn = _dot_nt(qe, _pad_rows(kn_ref, new_s)) * scale
        rowt = lax.broadcasted_iota(jnp.int32, (rows, LANES), 0) % t
        sn = jnp.where(col <= rowt, sn, NEG)
        m = jnp.maximum(jnp.max(sn, axis=-1, keepdims=True),
                        jnp.max(jnp.where(sel > 0.5, bmax_s[...] * scale, NEG), axis=-1, keepdims=True))
        pn = jnp.exp(sn - m)
        m_s[...] = jnp.broadcast_to(m, m_s.shape)
        l_s[...] = jnp.broadcast_to(jnp.sum(pn, axis=-1, keepdims=True), l_s.shape)
        acc_s[...] = _dot(pn.astype(BF16), _pad_rows(vn_ref, new_s))

    @pl.when(s_id >= ns)
    def _():
        sel = sel_s[...]
        m = m_s[:, 0:1]
        l = l_s[:, 0:1]
        acc = acc_s[...]
        for i in range(pps):
            pg = (s_id - ns) * pps + i
            selc = jnp.sum(jnp.where(col == pg // pages_per_block, sel, 0.0), axis=-1, keepdims=True)
            p = jnp.exp(jnp.where(selc > 0.5, s_all[pg] * scale, NEG) - m)
            l = l + jnp.sum(p, axis=-1, keepdims=True)
            acc = acc + _dot_nt(p.astype(BF16), v_refs[i][...].astype(BF16))
        l_s[...] = jnp.broadcast_to(l, l_s.shape)
        acc_s[...] = acc

    @pl.when(s_id == 2 * ns - 1)
    def _():
        o_ref[...] = _collapse_heads(acc_s[...] / l_s[:, 0:1], t)


def _sample_specs(page_table, n_pages, t, page_index, pps=PAGES_PER_STEP):
    row_spec = pl.BlockSpec((t, ATT_WIDTH), lambda b, s, pt: (b, 0))

    def page_spec(i, which):
        return pl.BlockSpec((None, ATT_WIDTH, PAGE),
                            lambda b, s, pt: (pt[b, page_index(s, i, which)], 0, 0))

    in_specs = ([row_spec] * 3 + [page_spec(i, 0) for i in range(pps)]
                + [page_spec(i, 1) for i in range(pps)])
    return in_specs, row_spec


def _moba_sample(q, k_new, v_new, k_pool, v_pool, page_table, *, bsz, t):
    n_pages = page_table.shape[1]
    pps = PAGES_PER_STEP
    assert n_pages % pps == 0 and (n_pages * PAGE) % MOBA_BLOCK == 0 and t <= SUBLANES
    assert n_pages * PAGE // MOBA_BLOCK <= LANES
    ns = n_pages // pps
    rows = HEADS * t

    def page_index(s, i, which):
        step = jnp.minimum(s, ns - 1) if which == 0 else jnp.maximum(s - ns, 0)
        return step * pps + i

    in_specs, out_spec = _sample_specs(page_table, n_pages, t, page_index)
    grid_spec = pltpu.PrefetchScalarGridSpec(
        num_scalar_prefetch=1, grid=(bsz, 2 * ns), in_specs=in_specs, out_specs=out_spec,
        scratch_shapes=[pltpu.VMEM((rows, ATT_WIDTH), F32),
                        pltpu.VMEM((PAGE, ATT_WIDTH), F32),
                        pltpu.VMEM((n_pages, rows, PAGE), F32)]
                       + [pltpu.VMEM((rows, LANES), F32)] * 5
                       + [pltpu.VMEM((rows, ATT_WIDTH), F32)])
    return pl.pallas_call(
        functools.partial(_moba_sample_kernel, n_pages=n_pages, t=t),
        grid_spec=grid_spec,
        out_shape=jax.ShapeDtypeStruct((bsz * t, ATT_WIDTH), F32),
        name="moba_sample")(page_table, q, k_new, v_new, *([k_pool] * pps), *([v_pool] * pps))


def _sb_weights(z, r, tri, strict):
    n = z.shape[0]
    lg = jnp.log(1.0 + jnp.exp(-jnp.abs(z)))
    log_beta = jnp.minimum(z, 0.0) - lg
    log_keep = jnp.minimum(-z, 0.0) - lg
    if strict is not None:
        log_beta = jnp.where(strict, log_beta, NEG)
        log_keep = jnp.where(strict, log_keep, 0.0)
    st = _dot(jnp.concatenate(_split3(log_keep), axis=0), tri)
    later = st[:n] + st[n:2 * n] + st[2 * n:]
    w = jnp.exp(log_beta + later + r)
    return w.astype(BF16), r + jnp.sum(log_keep, axis=-1, keepdims=True)


def _suffix_matrix(n):
    return (lax.broadcasted_iota(jnp.int32, (n, n), 0)
            > lax.broadcasted_iota(jnp.int32, (n, n), 1)).astype(BF16)


def _sb_prompt_kernel(q_ref, k_ref, v_ref, o_ref, kb_s, vb_s):
    bs = SB_BLOCK
    i = pl.program_id(2)

    @pl.when(i == 0)
    def _():
        for j in range(k_ref.shape[0] // bs):
            kb_s[j * bs:(j + 1) * bs, :] = k_ref[j * bs:(j + 1) * bs, :].astype(BF16)
            vb_s[j * bs:(j + 1) * bs, :] = v_ref[j * bs:(j + 1) * bs, :].astype(BF16)

    lane = lax.broadcasted_iota(jnp.int32, (1, LANES), 1)
    tri = _suffix_matrix(bs)
    strict = (lax.broadcasted_iota(jnp.int32, (2 * bs, bs), 1)
              < lax.broadcasted_iota(jnp.int32, (2 * bs, bs), 0) % bs)
    qf = q_ref[...]
    qhs = []
    for hd in range(2):
        lm = (lane >= HEAD_DIM * hd) & (lane < HEAD_DIM * (hd + 1))
        qhs.append(jnp.where(lm, qf, 0.0))
    qs2 = (jnp.concatenate(qhs, axis=0) * HEAD_DIM ** -0.5).astype(BF16)

    def block(j, r, mask):
        c0 = pl.multiple_of(j * bs, bs)
        w, r = _sb_weights(_dot_nt(qs2, kb_s[pl.ds(c0, bs), :]), r, tri, mask)
        return _dot(w, vb_s[pl.ds(c0, bs), :]), r

    acc, r = block(i, jnp.zeros((2 * bs, 1), F32), strict)
    has_prev = lax.broadcasted_iota(jnp.int32, (2 * bs, bs), 0) < jnp.where(i > 0, 2 * bs, 0)
    d, r = block(jnp.maximum(i - 1, 0), r, has_prev)
    acc = acc + d

    def cond(c):
        j, r, _ = c
        return (j >= 0) & (jnp.max(r) > EXP_UNDERFLOW)

    def body(c):
        j, r, acc = c
        d, r = block(j, r, None)
        return j - 1, r, acc + d

    _, _, acc = lax.while_loop(cond, body, (i - 2, r, acc))
    o_ref[...] = jnp.where(lane < HEAD_DIM, acc[:bs], acc[bs:])


def _sb_prompt(q, k, v, *, bsz, t):
    assert t % SB_BLOCK == 0
    n = bsz * t
    nq = t // SB_BLOCK
    q_spec = pl.BlockSpec((SB_BLOCK, LANES), lambda b, hp, i: (b * nq + i, hp))
    kv_spec = pl.BlockSpec((t, LANES), lambda b, hp, i: (b, hp))
    return pl.pallas_call(
        _sb_prompt_kernel,
        grid=(bsz, ATT_WIDTH // LANES, nq),
        in_specs=[q_spec, kv_spec, kv_spec], out_specs=q_spec,
        out_shape=jax.ShapeDtypeStruct((n, ATT_WIDTH), F32),
        scratch_shapes=[pltpu.VMEM((t, LANES), BF16), pltpu.VMEM((t, LANES), BF16)],
        name="sb_prompt")(q, k, v)


def _sb_sample_kernel(pt_ref, q_ref, a_ref, b_ref, *rest, t, first, pps):
    k_refs = rest[:pps]
    v_refs = rest[pps:2 * pps]
    n_out = 3 if first else 1
    o_ref = rest[2 * pps]
    qexp_s, new_s, r_s, acc_s = rest[2 * pps + n_out:]
    s_id = pl.program_id(1)
    rows = HEADS * t
    tri = _suffix_matrix(PAGE)

    @pl.when(s_id == 0)
    def _():
        _expand_heads(q_ref[...] * HEAD_DIM ** -0.5, qexp_s)
        if first:
            z = _dot_nt(qexp_s[...].astype(BF16), _pad_rows(a_ref, new_s))
            col = lax.broadcasted_iota(jnp.int32, (rows, PAGE), 1)
            rowt = lax.broadcasted_iota(jnp.int32, (rows, PAGE), 0) % t
            w, r = _sb_weights(z, jnp.zeros((rows, 1), F32), tri, col < rowt)
            acc_s[...] = _dot(w, _pad_rows(b_ref, new_s))
            r_s[...] = jnp.broadcast_to(r, r_s.shape)
        else:
            acc_s[...] = a_ref[...]
            r_s[...] = b_ref[...]

    for i in range(pps):
        @pl.when(jnp.max(r_s[...]) > EXP_UNDERFLOW)
        def _(i=i):
            z = _dot(qexp_s[...].astype(BF16), k_refs[i][...].astype(BF16))
            w, r = _sb_weights(z, r_s[:, 0:1], tri, None)
            acc_s[...] = acc_s[...] + _dot_nt(w, v_refs[i][...].astype(BF16))
            r_s[...] = jnp.broadcast_to(r, r_s.shape)

    @pl.when(s_id == pl.num_programs(1) - 1)
    def _():
        o_ref[...] = _collapse_heads(acc_s[...], t)
        if first:
            rest[2 * pps + 1][...] = acc_s[...]
            rest[2 * pps + 2][...] = r_s[...]


def _sb_sample(q, k_new, v_new, k_pool, v_pool, page_table, *, bsz, t):
    n_pages = page_table.shape[1]
    assert t <= SUBLANES
    first = min(SB_FIRST_PAGES, n_pages)
    rows = HEADS * t
    n = bsz * t
    scratch = [pltpu.VMEM((rows, ATT_WIDTH), F32), pltpu.VMEM((PAGE, ATT_WIDTH), F32),
               pltpu.VMEM((rows, LANES), F32), pltpu.VMEM((rows, ATT_WIDTH), F32)]
    acc_spec = pl.BlockSpec((rows, ATT_WIDTH), lambda b, s, pt: (b, 0))
    r_spec = pl.BlockSpec((rows, LANES), lambda b, s, pt: (b, 0))
    y_sds = jax.ShapeDtypeStruct((n, ATT_WIDTH), F32)

    in_specs, y_spec = _sample_specs(page_table, n_pages, t, lambda s, i, which: n_pages - 1 - i, first)
    y, acc, r = pl.pallas_call(
        functools.partial(_sb_sample_kernel, t=t, first=True, pps=first),
        grid_spec=pltpu.PrefetchScalarGridSpec(
            num_scalar_prefetch=1, grid=(bsz, 1), in_specs=in_specs,
            out_specs=[y_spec, acc_spec, r_spec], scratch_shapes=scratch),
        out_shape=[y_sds, jax.ShapeDtypeStruct((bsz * rows, ATT_WIDTH), F32),
                   jax.ShapeDtypeStruct((bsz * rows, LANES), F32)],
        name="sb_sample")(page_table, q, k_new, v_new, *([k_pool] * first), *([v_pool] * first))
    older = n_pages - first
    if older == 0:
        return y
    pps = max(d for d in range(1, PAGES_PER_STEP + 1) if older % d == 0)

    def older_pages():
        specs, _ = _sample_specs(page_table, n_pages, t,
                                 lambda s, i, which: n_pages - 1 - first - (s * pps + i), pps)
        specs = [specs[0], acc_spec, r_spec] + specs[3:]
        return pl.pallas_call(
            functools.partial(_sb_sample_kernel, t=t, first=False, pps=pps),
            grid_spec=pltpu.PrefetchScalarGridSpec(
                num_scalar_prefetch=1, grid=(bsz, older // pps), in_specs=specs,
                out_specs=y_spec, scratch_shapes=scratch),
            out_shape=y_sds,
            name="sb_sample_older")(page_table, q, acc, r, *([k_pool] * pps), *([v_pool] * pps))

    return lax.cond(jnp.max(r) > EXP_UNDERFLOW, older_pages, lambda: y)


def _hgrn_kernel(q_ref, f_ref, i_ref, g_ref, lb_ref, s0_ref, gn_ref, o_ref, s_out_ref, st_s,
                 *, layer):
    tb = q_ref.shape[0]
    t = pl.program_id(1)

    @pl.when(t == 0)
    def _():
        for hh in range(HGRN_HEADS):
            st_s[hh] = s0_ref[0, hh].T

    lbp = lb_ref[...]
    e = jnp.exp(lbp - jnp.max(lbp, axis=0, keepdims=True))
    soft = e / jnp.sum(e, axis=0, keepdims=True)
    lbv = jnp.sum(soft[1:layer + 1], axis=0, keepdims=True)

    pad = max(HGRN_SUB - tb, 0)
    L = min(HGRN_CHUNK, tb + pad)
    c = min(HGRN_SUB, L)
    tri = (lax.broadcasted_iota(jnp.int32, (L, L), 0)
           >= lax.broadcasted_iota(jnp.int32, (L, L), 1)).astype(BF16)
    rowc = lax.broadcasted_iota(jnp.int32, (c, 1), 0)
    gn = gn_ref[...]

    def padrows(x):
        if pad == 0:
            return x
        return jnp.concatenate([x, jnp.zeros((pad, x.shape[1]), F32)], axis=0)

    def chunk(rows, n_valid):
        for hh in range(HGRN_HEADS):
            ls = slice(hh * HGRN_DIM, (hh + 1) * HGRN_DIM)
            lbh = lbv[:, ls]
            f = lbh + (1.0 - lbh) * _sigmoid(f_ref[rows, ls])
            lf = padrows(jnp.log(f))
            kk = padrows(1.0 - f)
            qv = padrows(q_ref[rows, ls])
            vv = padrows(i_ref[rows, ls])
            hi, mid, lo = _split3(lf)
            b = _dot(tri, hi) + _dot(tri, mid) + _dot(tri, lo)
            st = st_s[hh]
            o_inter = _dot_nt((qv * jnp.exp(b)).astype(BF16), st.astype(BF16))
            vb = vv.astype(BF16)
            parts = []
            for si in range(L // c):
                rs = slice(si * c, (si + 1) * c)
                b_i = b[rs]
                q_i = qv[rs]
                k_i = kk[rs]
                v_i = vv[rs]
                o_i = o_inter[rs]
                if si > 0:
                    b_prev = b[si * c - 1:si * c]
                    qt = (q_i * jnp.exp(b_i - b_prev)).astype(BF16)
                    kt = (kk[:si * c] * jnp.exp(b_prev - b[:si * c])).astype(BF16)
                    o_i = o_i + _dot(_dot_nt(qt, kt).astype(BF16), vb[:si * c])
                for s in range(min(c, max(n_valid - si * c, 0))):
                    dec = jnp.exp(jnp.minimum(b_i - b_i[s:s + 1], 0.0))
                    a = jnp.sum(q_i * k_i[s:s + 1] * dec, axis=-1, keepdims=True)
                    o_i = o_i + jnp.where(rowc >= s, a, 0.0) * v_i[s:s + 1]
                parts.append(o_i)
            o = jnp.concatenate(parts, axis=0) if len(parts) > 1 else parts[0]
            o = o[:n_valid]
            o = o * lax.rsqrt(jnp.mean(o * o, axis=-1, keepdims=True) + EPS) * gn
            gv = g_ref[rows, ls]
            o_ref[rows, ls] = o * (gv * _sigmoid(gv))
            b_last = b[L - 1:L]
            kdec = (kk * jnp.exp(b_last - b)).astype(BF16)
            st_s[hh] = st * jnp.exp(b_last) + _dot_tn(vb, kdec)

    if pad:
        chunk(slice(0, tb), tb)
    else:
        def body(ch, _):
            chunk(pl.ds(pl.multiple_of(ch * L, L), L), L)
            return 0

        lax.fori_loop(0, tb // L, body, 0)

    for hh in range(HGRN_HEADS):
        s_out_ref[0, hh] = st_s[hh].T


def _hgrn(qh, fh, ih, gh, lb, s0, gnorm, *, bsz, t, layer):
    n = bsz * t
    tb = min(ROW_TILE, t)
    nt = t // tb
    row_spec = pl.BlockSpec((tb, SLAB), lambda b, i: (b * nt + i, 0))
    st_spec = pl.BlockSpec((1, HGRN_HEADS, HGRN_DIM, HGRN_DIM), lambda b, i: (b, 0, 0, 0))
    return pl.pallas_call(
        functools.partial(_hgrn_kernel, layer=layer),
        grid=(bsz, nt),
        in_specs=[row_spec] * 4 + [_const(lb.shape), st_spec, _const((1, HGRN_DIM))],
        out_specs=[row_spec, st_spec],
        out_shape=[jax.ShapeDtypeStruct((n, SLAB), F32),
                   jax.ShapeDtypeStruct((bsz, HGRN_HEADS, HGRN_DIM, HGRN_DIM), F32)],
        scratch_shapes=[pltpu.VMEM((HGRN_HEADS, HGRN_DIM, HGRN_DIM), F32)],
        name="hgrn2")(qh, fh, ih, gh, lb, s0, gnorm.reshape(1, HGRN_DIM))


def _tail_kernel(*refs, chain):
    if chain:
        (h_ref, ya_ref, yb_ref, p_ref, wo_ref, gmp_ref, gfp_ref, wug_ref, wuv_ref, wcv_ref, wdn_ref,
         gfo_ref, wpp_ref, wpg_ref, gpl_ref, ho_ref, ffn_ref, acc_s, u_s, h1_s, a_s, act_s, car_s) = refs
    else:
        (h_ref, ya_ref, yb_ref, p_ref, p1_ref, p2_ref, wo_ref, gmp_ref, gfp_ref, wug_ref, wuv_ref,
         wcv_ref, wdn_ref, gfo_ref, wpp_ref, wpg_ref, gpl_ref, ho_ref, ffn_ref, acc_s, u_s, h1_s, a_s, act_s) = refs
    tm = h_ref.shape[0]
    nch = wug_ref.shape[0]
    cw = wug_ref.shape[2]
    mix = _dot(ya_ref[...].astype(BF16), wo_ref[0]) + _dot(yb_ref[...].astype(BF16), wo_ref[1])
    h1 = h_ref[...] + _rms(mix, gmp_ref[...])
    h1_s[...] = h1
    a_s[...] = _rms(h1, gfp_ref[...]).astype(BF16)
    acc_s[...] = jnp.zeros(acc_s.shape, F32)
    rowi = lax.broadcasted_iota(jnp.int32, (tm, cw), 0)

    if chain:
        @pl.when(pl.program_id(1) == 0)
        def _():
            car_s[...] = jnp.zeros(car_s.shape, F32)

    def taps(w, p2, p1, u):
        return w[3:4] + w[0:1] * p2 + w[1:2] * p1 + w[2:3] * u

    def up(c):
        a = a_s[...]
        u_s[c % 2, 0, SUBLANES:, :] = _dot(a, wug_ref[c])
        u_s[c % 2, 1, SUBLANES:, :] = _dot(a, wuv_ref[c])

    def down(c):
        slot = c % 2
        w = wcv_ref[c]
        if chain:
            for half in range(2):
                u_s[slot, half, 0:SUBLANES, :] = car_s[half * nch + c]
            for rb in range(tm // CONV_ROWS):
                cs = []
                for half in range(2):
                    x = u_s[slot, half, rb * CONV_ROWS:(rb + 1) * CONV_ROWS + SUBLANES, :]
                    cs.append(taps(w[4 * half:4 * half + 4], pltpu.roll(x, 2, 0)[SUBLANES:],
                                   pltpu.roll(x, 1, 0)[SUBLANES:], x[SUBLANES:]))
                act_s[rb * CONV_ROWS:(rb + 1) * CONV_ROWS, :] = (_gelu(cs[0]) * cs[1]).astype(BF16)
            for half in range(2):
                last = u_s[slot, half, tm:tm + SUBLANES, :]
                car_s[half * nch + c] = last
                ffn_ref[0, half * nch + c] = last
        else:
            rm = rowi % SUBLANES
            cs = []
            for half in range(2):
                u = u_s[slot, half, SUBLANES:, :]
                p1 = jnp.where(rm == 0, p1_ref[half * nch + c], pltpu.roll(u, 1, 0))
                p2 = jnp.where(rm < 2, p2_ref[half * nch + c], pltpu.roll(u, 2, 0))
                cs.append(taps(w[4 * half:4 * half + 4], p2, p1, u))
                ffn_ref[half * nch + c] = u
            act_s[...] = (_gelu(cs[0]) * cs[1]).astype(BF16)
        acc_s[...] += _dot(act_s[...], wdn_ref[c])

    up(0)
    for c in range(nch):
        if c + 1 < nch:
            up(c + 1)
        down(c)
    h2 = h1_s[...] + _rms(acc_s[...], gfo_ref[...])
    ple = _dot(p_ref[...].astype(BF16), wpp_ref[...]) * _sigmoid(_dot(h2.astype(BF16), wpg_ref[...]))
    ho_ref[...] = h2 + _rms(ple, gpl_ref[...])


def _tail(h, ya, yb, p, ffn0, w, *, bsz, t):
    n, d = h.shape
    nch = N_FF_CHUNKS
    cw = FF_CHUNK
    chain = t > SUBLANES
    consts = [w['wo'], w['gmp'], w['gfp'], w['wug'], w['wuv'], w['wcv'], w['wdn'], w['gfo'], w['wpp'],
              w['wpg'], w['gpl']]
    const_specs = [_const(x.shape) for x in consts]
    if chain:
        assert ffn0 is None, "a long sequence starts from an empty ConvFFN buffer"
        tm = min(MATMUL_TILE, t)
        nt = t // tm
        grid = (bsz, nt)
        rows = lambda width: pl.BlockSpec((tm, width), lambda b, i: (b * nt + i, 0))
        ffn_spec = pl.BlockSpec((1, 2 * nch, SUBLANES, cw), lambda b, i: (b, 0, 0, 0))
        ffn_sds = jax.ShapeDtypeStruct((bsz, 2 * nch, SUBLANES, cw), F32)
        extra, extra_specs = [], []
        scratch = [pltpu.VMEM((tm, d), F32), pltpu.VMEM((2, 2, tm + SUBLANES, cw), F32),
                   pltpu.VMEM((tm, d), F32), pltpu.VMEM((tm, d), BF16), pltpu.VMEM((tm, cw), BF16),
                   pltpu.VMEM((2 * nch, SUBLANES, cw), F32)]
    else:
        assert t == SUBLANES
        tm = n
        grid = (1, 1)
        rows = lambda width: pl.BlockSpec((tm, width), lambda b, i: (0, 0))
        ffn_spec = _full((2 * nch, tm, cw))
        ffn_sds = jax.ShapeDtypeStruct((2 * nch, tm, cw), F32)
        buf = ffn0.astype(F32).reshape(bsz, 2, 2 * nch, cw).transpose(2, 0, 1, 3)
        zero = jnp.zeros((2 * nch, bsz, SUBLANES - 2, cw), F32)
        p1 = jnp.concatenate([buf[:, :, 1:2], zero, zero[:, :, :1]], axis=2).reshape(2 * nch, tm, cw)
        p2 = jnp.concatenate([buf, zero], axis=2).reshape(2 * nch, tm, cw)
        extra = [p1, p2]
        extra_specs = [_full(p1.shape), _full(p2.shape)]
        scratch = [pltpu.VMEM((tm, d), F32), pltpu.VMEM((2, 2, tm + SUBLANES, cw), F32),
                   pltpu.VMEM((tm, d), F32), pltpu.VMEM((tm, d), BF16), pltpu.VMEM((tm, cw), BF16)]
    ho, ffn = pl.pallas_call(
        functools.partial(_tail_kernel, chain=chain),
        grid=grid,
        in_specs=[rows(d), rows(SLAB), rows(SLAB), rows(p.shape[1])] + extra_specs + const_specs,
        out_specs=[rows(d), ffn_spec],
        out_shape=[jax.ShapeDtypeStruct((n, d), F32), ffn_sds],
        scratch_shapes=scratch,
        name="layer_tail")(h, ya, yb, p, *extra, *consts)
    if chain:
        st = ffn[:, :, SUBLANES - 2:, :]
    else:
        st = ffn.reshape(2 * nch, bsz, SUBLANES, cw)[:, :, SUBLANES - 2:, :].transpose(1, 0, 2, 3)
    return ho, st.transpose(0, 2, 1, 3).reshape(bsz, 2, 2 * nch * cw)


def _tail_weights(i, w_out, g_mix_post, g_ffn_pre, g_ffn_post, w_ffn_up, w_ffn_conv, b_ffn_conv,
                  w_ffn_down, w_ple_proj, w_ple_gate, g_ple):
    d = w_out.shape[1]
    nch, cw = N_FF_CHUNKS, FF_CHUNK
    up = w_ffn_up[i].astype(BF16).reshape(d, 2, nch, cw).transpose(1, 2, 0, 3)
    taps = jnp.concatenate([w_ffn_conv[i].astype(F32), b_ffn_conv[i].astype(F32)[None]], axis=0)
    wcv = taps.reshape(4, 2, nch, cw).transpose(2, 1, 0, 3).reshape(nch, 8, cw)
    row = lambda g: g.astype(F32).reshape(1, d)
    return dict(
        wo=w_out.astype(BF16).reshape(2, SLAB, d), gmp=row(g_mix_post[i]), gfp=row(g_ffn_pre[i]),
        wug=up[0], wuv=up[1], wcv=wcv, wdn=w_ffn_down[i].astype(BF16).reshape(nch, cw, d),
        gfo=row(g_ffn_post[i]), wpp=w_ple_proj[i].astype(BF16), wpg=w_ple_gate[i].astype(BF16),
        gpl=row(g_ple[i]))


def kernel(x_prompt, x_sample, p_prompt, p_sample, cache_moba_k, cache_moba_v, state_s5_re, state_s5_im,
           state_hgrn, cache_sb_k, cache_sb_v, state_ffn, page_table,
           g_mix_pre, g_mix_post, g_ffn_pre, g_ffn_post, w_ffn_up, w_ffn_conv, b_ffn_conv, w_ffn_down,
           w_ple_proj, w_ple_gate, g_ple,
           w_in_a, w_out_a, s5_a_re, s5_a_im, s5_log_dt, s5_b_re, s5_b_im, s5_c_re, s5_c_im, s5_d,
           s5_w_glu, s5_b_glu,
           w_in_c, w_out_c, hgrn_lb, g_hgrn_norm):
    depth = g_mix_pre.shape[0]
    d_model = x_prompt.shape[-1]
    n_pages = page_table.shape[1]
    past_len = n_pages * cache_moba_k.shape[2]
    page_table = page_table.astype(jnp.int32)

    layers = []
    for i in range(depth):
        j = i // 2
        lw = {}
        if i % 2 == 0:
            lw['w_in'] = w_in_a[j].astype(BF16)
            lw['disc'] = _s5_discretize(s5_a_re[j], s5_a_im[j], s5_log_dt[j])
            lw['wb'], lw['wc'] = _s5_weights(s5_b_re[j], s5_b_im[j], s5_c_re[j], s5_c_im[j])
            lw['d'] = s5_d[j].astype(F32).reshape(S5_WIDTH)
            lw['wglu'] = s5_w_glu[j].astype(BF16)
            lw['bglu'] = s5_b_glu[j].astype(F32)
            w_out = w_out_a[j]
        else:
            lw['w_in'] = w_in_c[j].astype(BF16)
            w_out = w_out_c[j]
        lw['tail'] = _tail_weights(i, w_out, g_mix_post, g_ffn_pre, g_ffn_post, w_ffn_up, w_ffn_conv,
                                   b_ffn_conv, w_ffn_down, w_ple_proj, w_ple_gate, g_ple)
        layers.append(lw)

    def pool2d(pool, j):
        return pool[j].transpose(0, 2, 3, 1).reshape(pool.shape[1], ATT_WIDTH, pool.shape[2])

    def run(x, p, q_start, s5_re0, s5_im0, hgrn0, ffn0, has_past):
        bsz, t, _ = x.shape
        n = bsz * t
        h = x.astype(F32).reshape(n, d_model)
        tables = _rope_tables(t, q_start)
        if t < MATMUL_TILE:
            tables = tuple(jnp.tile(tb, (n // t, 1)) for tb in tables)
        mk, mv, sr, si, hs, sk, sv, fb = [], [], [], [], [], [], [], []
        long_seq = t % MATMUL_TILE == 0

        def kv_out(x, x_t, which):
            if x_t:
                return x_t[which].reshape(bsz, HEADS, HEAD_DIM, t).transpose(0, 3, 1, 2)
            return x.reshape(bsz, t, HEADS, HEAD_DIM)

        for i in range(depth):
            j = i // 2
            lw = layers[i]
            g_pre = g_mix_pre[i].astype(F32)
            if i % 2 == 0:
                u, q, k, v, *kv_t = _inproj(h, g_pre, lw['w_in'], tables, rope_slabs=(1, 2),
                                            t_slabs=(2, 3) if long_seq else (), seq=(bsz, t))
                y_a, hr, hi = _s5(u, s5_re0[j].astype(F32).reshape(bsz, 1, S5_LANES),
                                  s5_im0[j].astype(F32).reshape(bsz, 1, S5_LANES), lw['disc'], lw['wb'],
                                  lw['wc'], lw['d'], lw['wglu'], lw['bglu'], bsz=bsz, t=t)
                if has_past:
                    y_b = _moba_sample(q, k, v, pool2d(cache_moba_k, j), pool2d(cache_moba_v, j),
                                       page_table, bsz=bsz, t=t)
                else:
                    y_b = _moba_prompt(q, k, v, bsz=bsz, t=t)
                mk.append(kv_out(k, kv_t, 0))
                mv.append(kv_out(v, kv_t, 1))
                sr.append(hr.reshape(bsz, S5_GROUPS, S5_STATE))
                si.append(hi.reshape(bsz, S5_GROUPS, S5_STATE))
            else:
                qh, fh, ih, gh, q, k, v, *kv_t = _inproj(h, g_pre, lw['w_in'], tables, rope_slabs=(),
                                                         t_slabs=(5, 6) if long_seq else (), seq=(bsz, t))
                y_a, s_fin = _hgrn(qh, fh, ih, gh, hgrn_lb.astype(F32), hgrn0[j].astype(F32),
                                   g_hgrn_norm[j].astype(F32), bsz=bsz, t=t, layer=i)
                if has_past:
                    y_b = _sb_sample(q, k, v, pool2d(cache_sb_k, j), pool2d(cache_sb_v, j), page_table,
                                     bsz=bsz, t=t)
                else:
                    y_b = _sb_prompt(q, k, v, bsz=bsz, t=t)
                hs.append(s_fin)
                sk.append(kv_out(k, kv_t, 0))
                sv.append(kv_out(v, kv_t, 1))
            h, buf = _tail(h, y_a, y_b, p[i].astype(F32).reshape(n, p.shape[-1]),
                           None if ffn0 is None else ffn0[i], lw['tail'], bsz=bsz, t=t)
            fb.append(buf)
        return (h.reshape(bsz, t, d_model), jnp.stack(mk), jnp.stack(mv), jnp.stack(sr), jnp.stack(si),
                jnp.stack(hs), jnp.stack(sk), jnp.stack(sv), jnp.stack(fb))

    bp = x_prompt.shape[0]
    n_a = (depth + 1) // 2
    n_c = depth // 2
    outs_p = run(x_prompt, p_prompt, 0,
                 jnp.zeros((n_a, bp, S5_GROUPS, S5_STATE), F32), jnp.zeros((n_a, bp, S5_GROUPS, S5_STATE), F32),
                 jnp.zeros((n_c, bp, HGRN_HEADS, HGRN_DIM, HGRN_DIM), F32), None, False)
    outs_s = run(x_sample, p_sample, past_len, state_s5_re, state_s5_im, state_hgrn, state_ffn, True)
    return (outs_p[0], outs_s[0]) + tuple(outs_p[1:]) + tuple(outs_s[1:])
```

```python
import functools
import math

import jax
import jax.numpy as jnp
from jax import lax
from jax.experimental import pallas as pl
from jax.experimental.pallas import tpu as pltpu

F32 = jnp.float32
BF16 = jnp.bfloat16

S5_WIDTH = 512
S5_GROUPS = 32
S5_GROUP = 16
S5_STATE = 64
S5_LANES = S5_GROUPS * S5_STATE
HEADS = 8
HEAD_DIM = 64
ATT_WIDTH = HEADS * HEAD_DIM
ROPE_DIM = 16
ROPE_THETA = 500000.0
MOBA_BLOCK = 256
MOBA_TOPK = 3
MOBA_PAIRS = 4
HGRN_HEADS = 4
HGRN_DIM = 128
HGRN_CHUNK = 64
HGRN_SUB = 16
D_FF = 2816
FF_CHUNK = 256
N_FF_CHUNKS = D_FF // FF_CHUNK
PAGE = 128
EPS = 1e-6
NEG = -1e30
SLAB = 512
LANES = 128
SUBLANES = 8
ROW_TILE = 256
MATMUL_TILE = 512
CONV_ROWS = 128
SB_BLOCK = 256
PAGES_PER_STEP = 16
SB_FIRST_PAGES = 4
LOG2E = 1.4426950408889634
EXP2_UNDERFLOW = -150.0


def _dot(a, b):
    return jnp.dot(a, b, preferred_element_type=F32)


def _dot_nt(a, b):
    return lax.dot_general(a, b, (((1,), (1,)), ((), ())), preferred_element_type=F32)


def _dot_tn(a, b):
    return lax.dot_general(a, b, (((0,), (0,)), ((), ())), preferred_element_type=F32)


def _split3(x):
    hi = x.astype(BF16)
    r1 = x - hi.astype(F32)
    mid = r1.astype(BF16)
    lo = (r1 - mid.astype(F32)).astype(BF16)
    return hi, mid, lo


def _rms(x, g):
    return x * lax.rsqrt(jnp.mean(x * x, axis=-1, keepdims=True) + EPS) * g


def _sigmoid(x):
    return 1.0 / (1.0 + jnp.exp(-x))


def _gelu(x):
    return 0.5 * x * (1.0 + jnp.tanh(0.7978845608028654 * (x + 0.044715 * (x * x * x))))


def _full(shape):
    nd = len(shape)
    return pl.BlockSpec(shape, lambda *_: (0,) * nd)


def _const(shape):
    nd = len(shape)
    return pl.BlockSpec(shape, lambda *_: (0,) * nd, pipeline_mode=pl.Buffered(1))


def _rope_table_kernel(c_ref, s1_ref, s2_ref, *, q_start):
    rows, lanes = c_ref.shape
    i = pl.program_id(0)
    lane = lax.broadcasted_iota(jnp.int32, (rows, lanes), 1)
    pos = lax.broadcasted_iota(jnp.int32, (rows, lanes), 0) + i * rows + q_start
    d = lane % HEAD_DIM
    half = ROPE_DIM // 2
    inv = jnp.exp((d % half).astype(F32) * (-math.log(ROPE_THETA) / half))
    ang = pos.astype(F32) * inv
    cs = jnp.cos(ang)
    sn = jnp.sin(ang)
    c_ref[...] = jnp.where(d < ROPE_DIM, cs, 1.0)
    s1_ref[...] = jnp.where(d < half, -sn, 0.0)
    s2_ref[...] = jnp.where((d >= half) & (d < ROPE_DIM), sn, 0.0)


def _rope_tables(t, q_start):
    rows = min(t, 512)
    sds = jax.ShapeDtypeStruct((t, LANES), F32)
    spec = pl.BlockSpec((rows, LANES), lambda i: (i, 0))
    return pl.pallas_call(
        functools.partial(_rope_table_kernel, q_start=q_start),
        grid=(t // rows,), out_shape=(sds, sds, sds), out_specs=(spec, spec, spec),
        name="rope_tables")()


def _inproj_kernel(h_ref, g_ref, w_ref, c_ref, s1_ref, s2_ref, *out_refs, n_slabs, rope_slabs,
                   t_slabs):
    a = _rms(h_ref[...], g_ref[...]).astype(BF16)
    half = ROPE_DIM // 2
    for s in range(n_slabs):
        o_ref = out_refs[s]
        z = _dot(a, w_ref[:, s * SLAB:(s + 1) * SLAB])
        if s in rope_slabs:
            c = c_ref[...]
            s1 = s1_ref[...]
            s2 = s2_ref[...]
            for j in range(SLAB // LANES):
                x = z[:, j * LANES:(j + 1) * LANES]
                o_ref[:, j * LANES:(j + 1) * LANES] = (
                    x * c + pltpu.roll(x, LANES - half, 1) * s1 + pltpu.roll(x, half, 1) * s2)
        else:
            o_ref[...] = z
        if s in t_slabs:
            out_refs[n_slabs + t_slabs.index(s)][...] = o_ref[...].T


def _inproj(h, g, w, tables, rope_slabs, t_slabs=(), seq=None):
    n, d = h.shape
    n_slabs = w.shape[1] // SLAB
    tm = min(MATMUL_TILE, n)
    t_tiles = tables[0].shape[0] // tm
    tab_spec = pl.BlockSpec((tm, LANES), lambda i: (i % t_tiles, 0))
    out_spec = pl.BlockSpec((tm, SLAB), lambda i: (i, 0))
    out_specs = [out_spec] * n_slabs
    out_shape = [jax.ShapeDtypeStruct((n, SLAB), F32)] * n_slabs
    if t_slabs:
        bsz, t = seq
        assert t % tm == 0
        nt = t // tm
        out_specs += [pl.BlockSpec((None, SLAB, tm), lambda i: (i // nt, 0, i % nt))] * len(t_slabs)
        out_shape += [jax.ShapeDtypeStruct((bsz, SLAB, t), F32)] * len(t_slabs)
    return pl.pallas_call(
        functools.partial(_inproj_kernel, n_slabs=n_slabs, rope_slabs=rope_slabs, t_slabs=t_slabs),
        grid=(n // tm,),
        in_specs=[pl.BlockSpec((tm, d), lambda i: (i, 0)), _const((1, d)), _const(w.shape),
                  tab_spec, tab_spec, tab_spec],
        out_specs=out_specs, out_shape=out_shape,
        name="inproj")(h, g.reshape(1, d), w, *tables)


def _s5_disc_kernel(ar_ref, ai_ref, ldt_ref, pre_ref, pim_ref, cc_ref):
    ar = ar_ref[...]
    ai = ai_ref[...]
    dt = jnp.exp(ldt_ref[...])
    row = lax.broadcasted_iota(jnp.int32, (4 * SUBLANES, 1), 0)
    blk = row // SUBLANES
    r = row % SUBLANES
    n = jnp.where(blk == 0, r + 1, jnp.where(blk == 1, 1, jnp.where(blk == 2, 2, 4)))
    keep = (blk == 0) | (r >= n)
    n = n.astype(F32)
    mag = jnp.exp(ar * dt * n)
    ang = ai * dt * n
    pre = mag * jnp.cos(ang)
    pim = mag * jnp.sin(ang)
    pre_ref[...] = jnp.where(keep, pre, 0.0)
    pim_ref[...] = jnp.where(keep, pim, 0.0)
    abr = pre[0:1]
    abi = pim[0:1]
    den = ar * ar + ai * ai
    nr = abr - 1.0
    cc_ref[0:1, :] = (nr * ar + abi * ai) / den
    cc_ref[1:2, :] = (abi * ar - nr * ai) / den


def _s5_discretize(a_re, a_im, log_dt):
    ar = a_re.reshape(1, S5_LANES)
    ai = a_im.reshape(1, S5_LANES)
    ldt = jnp.repeat(log_dt, S5_STATE).reshape(1, S5_LANES)
    return pl.pallas_call(
        _s5_disc_kernel,
        out_shape=(jax.ShapeDtypeStruct((4 * SUBLANES, S5_LANES), F32),
                   jax.ShapeDtypeStruct((4 * SUBLANES, S5_LANES), F32),
                   jax.ShapeDtypeStruct((2, S5_LANES), F32)),
        name="s5_discretize")(ar, ai, ldt)


SCAN_LANES = 512


def _s5_kernel(u_ref, h0r_ref, h0i_ref, pre_ref, pim_ref, cc_ref, wb_ref, wc_ref, d_ref,
               wglu_ref, bglu_ref, y_ref, hr_out_ref, hi_out_ref, xr_s, xi_s, car_s, *, chain):
    tm = u_ref.shape[0]
    nb = tm // SUBLANES
    u = u_ref[...]
    ub = u.astype(BF16)
    half_in = S5_WIDTH // 2
    half_st = S5_LANES // 2
    for hf in range(2):
        bu = _dot(ub[:, hf * half_in:(hf + 1) * half_in], wb_ref[hf])
        bur = bu[:, :half_st]
        bui = bu[:, half_st:]
        ls = slice(hf * half_st, (hf + 1) * half_st)
        cr = cc_ref[0:1, ls]
        ci = cc_ref[1:2, ls]
        xr_s[:, ls] = cr * bur - ci * bui
        xi_s[:, ls] = cr * bui + ci * bur

    if chain:
        t = pl.program_id(1)

        @pl.when(t == 0)
        def _():
            car_s[0:1, :] = h0r_ref[0]
            car_s[1:2, :] = h0i_ref[0]

    for lc in range(S5_LANES // SCAN_LANES):
        ls = slice(lc * SCAN_LANES, (lc + 1) * SCAN_LANES)

        def body(i, carry, ls=ls):
            r0 = pl.multiple_of(i * SUBLANES, SUBLANES)
            hr = xr_s[pl.ds(r0, SUBLANES), ls]
            hi = xi_s[pl.ds(r0, SUBLANES), ls]
            for step, k in enumerate((1, 2, 4)):
                ar = pre_ref[(step + 1) * SUBLANES:(step + 2) * SUBLANES, ls]
                ai = pim_ref[(step + 1) * SUBLANES:(step + 2) * SUBLANES, ls]
                sr = pltpu.roll(hr, k, 0)
                si = pltpu.roll(hi, k, 0)
                hr, hi = hr + ar * sr - ai * si, hi + ar * si + ai * sr
            if chain:
                cr_, ci_ = carry
            else:
                cr_ = h0r_ref[i, :, ls]
                ci_ = h0i_ref[i, :, ls]
            pr8 = pre_ref[0:SUBLANES, ls]
            pi8 = pim_ref[0:SUBLANES, ls]
            hr, hi = hr + pr8 * cr_ - pi8 * ci_, hi + pr8 * ci_ + pi8 * cr_
            xr_s[pl.ds(r0, SUBLANES), ls] = hr
            xi_s[pl.ds(r0, SUBLANES), ls] = hi
            last_r = hr[SUBLANES - 1:SUBLANES]
            last_i = hi[SUBLANES - 1:SUBLANES]
            if chain:
                return last_r, last_i
            hr_out_ref[i, :, ls] = last_r
            hi_out_ref[i, :, ls] = last_i
            return carry

        if chain:
            cr_, ci_ = lax.fori_loop(0, nb, body, (car_s[0:1, ls], car_s[1:2, ls]))
            car_s[0:1, ls] = cr_
            car_s[1:2, ls] = ci_
        else:
            lax.fori_loop(0, nb, body, 0)

    if chain:
        hr_out_ref[0] = car_s[0:1, :]
        hi_out_ref[0] = car_s[1:2, :]

    ys = []
    for hf in range(2):
        ls = slice(hf * half_st, (hf + 1) * half_st)
        hcat = jnp.concatenate([xr_s[:, ls], xi_s[:, ls]], axis=1).astype(BF16)
        ys.append(_dot(hcat, wc_ref[hf]))
    y = jnp.concatenate(ys, axis=1) + d_ref[...] * u
    y = _gelu(y)
    y_ref[...] = y * _sigmoid(_dot(y.astype(BF16), wglu_ref[...]) + bglu_ref[...])


def _s5_weights(b_re, b_im, c_re, c_im):
    gh = S5_GROUPS // 2
    eye = jnp.eye(gh, dtype=F32)

    def bmat(b):
        bt = b.astype(F32).transpose(0, 2, 1).reshape(2, gh, S5_GROUP, S5_STATE)
        return jnp.einsum('xghp,gk->xghkp', bt, eye).reshape(2, gh * S5_GROUP, gh * S5_STATE)

    def cmat(c):
        ct = c.astype(F32).transpose(0, 2, 1).reshape(2, gh, S5_STATE, S5_GROUP)
        return jnp.einsum('xgph,gk->xgpkh', ct, eye).reshape(2, gh * S5_STATE, gh * S5_GROUP)

    wb = jnp.concatenate([bmat(b_re), bmat(b_im)], axis=2).astype(BF16)
    wc = jnp.concatenate([cmat(c_re), -cmat(c_im)], axis=1).astype(BF16)
    return wb, wc


def _s5(u, h0r, h0i, disc, wb, wc, dvec, wglu, bglu, *, bsz, t):
    n = bsz * t
    pre, pim, cc = disc
    chain = t > SUBLANES
    if chain:
        tm = min(MATMUL_TILE, t)
        grid = (bsz, t // tm)
        row_spec = pl.BlockSpec((tm, S5_WIDTH), lambda b, i: (b * (t // tm) + i, 0))
        st_spec = pl.BlockSpec((1, 1, S5_LANES), lambda b, i: (b, 0, 0))
    else:
        assert t == SUBLANES
        tm = n
        grid = (1, 1)
        row_spec = pl.BlockSpec((tm, S5_WIDTH), lambda b, i: (0, 0))
        st_spec = pl.BlockSpec((bsz, 1, S5_LANES), lambda b, i: (0, 0, 0))
    st_sds = jax.ShapeDtypeStruct((bsz, 1, S5_LANES), F32)
    return pl.pallas_call(
        functools.partial(_s5_kernel, chain=chain),
        grid=grid,
        in_specs=[row_spec, st_spec, st_spec, _const(pre.shape), _const(pim.shape), _const(cc.shape),
                  _const(wb.shape), _const(wc.shape), _const((1, S5_WIDTH)), _const(wglu.shape),
                  _const((1, S5_WIDTH))],
        out_specs=[row_spec, st_spec, st_spec],
        out_shape=[jax.ShapeDtypeStruct((n, S5_WIDTH), F32), st_sds, st_sds],
        scratch_shapes=[pltpu.VMEM((tm, S5_LANES), F32), pltpu.VMEM((tm, S5_LANES), F32),
                        pltpu.VMEM((2, S5_LANES), F32)],
        name="s5_mixer")(u, h0r, h0i, pre, pim, cc, wb, wc, dvec.reshape(1, S5_WIDTH), wglu,
                         bglu.reshape(1, S5_WIDTH))


def _top_blocks(gate, n_past, col):
    g = jnp.where(col < n_past, gate, NEG)
    sel = jnp.zeros(gate.shape, F32)
    colf = col.astype(F32)
    for _ in range(MOBA_TOPK):
        m = jnp.max(g, axis=-1, keepdims=True)
        first = jnp.min(jnp.where(g == m, colf, float(LANES)), axis=-1, keepdims=True)
        pick = colf == first
        sel = jnp.where(pick & (m > 0.5 * NEG), 1.0, sel)
        g = jnp.where(pick, -3e38, g)
    return sel


def _top_blocks_t(gate, n_past):
    rowf = lax.broadcasted_iota(jnp.int32, gate.shape, 0).astype(F32)
    g = jnp.where(rowf < n_past, gate, NEG)
    sel = jnp.zeros(gate.shape, F32)
    for _ in range(MOBA_TOPK):
        m = jnp.max(g, axis=0, keepdims=True)
        first = jnp.min(jnp.where(g == m, rowf, float(LANES)), axis=0, keepdims=True)
        pick = rowf == first
        sel = jnp.where(pick & (m > 0.5 * NEG), 1.0, sel)
        g = jnp.where(pick, -3e38, g)
    return sel


def _moba_prompt_kernel(q_ref, k_ref, v_ref, o_ref, kb_s, vt_s, kmean_s, sel_s, *, nblk):
    bs = MOBA_BLOCK
    npair = MOBA_PAIRS
    width = 2 * bs
    i = pl.program_id(2)

    @pl.when(i == 0)
    def _():
        kmean_s[...] = jnp.zeros(kmean_s.shape, F32)
        for g in range(npair):
            gl = slice(g * LANES, (g + 1) * LANES)
            for j in range(nblk):
                kj = k_ref[j * bs:(j + 1) * bs, gl]
                kb_s[g, j * bs:(j + 1) * bs, :] = kj.astype(BF16)
                kmean_s[g, j:j + 1, :] = jnp.sum(kj, axis=0, keepdims=True) * (1.0 / bs)
                vt_s[g, j] = v_ref[j * bs:(j + 1) * bs, gl].T.astype(BF16)

    lane = lax.broadcasted_iota(jnp.int32, (1, LANES), 1)
    rowd = lax.broadcasted_iota(jnp.int32, (LANES, 1), 0)
    causal = (lax.broadcasted_iota(jnp.int32, (bs, npair * width), 0)
              <= lax.broadcasted_iota(jnp.int32, (bs, npair * width), 1) % bs)
    r0 = pl.multiple_of(i * bs, bs)
    qs2 = []
    gates = []
    for g in range(npair):
        qf = q_ref[:, g * LANES:(g + 1) * LANES]
        qh2 = jnp.concatenate(
            [jnp.where((lane >= HEAD_DIM * hd) & (lane < HEAD_DIM * (hd + 1)), qf, 0.0) for hd in range(2)],
            axis=0)
        qs2.append((qh2 * (HEAD_DIM ** -0.5 * LOG2E)).astype(BF16))
        gates.append(_dot_nt(kmean_s[g].astype(BF16), qh2.astype(BF16)))
    sel = _top_blocks_t(jnp.concatenate(gates, axis=1), i)
    for b in range(nblk):
        sel_s[b] = sel[b:b + 1, :]

    def scores(rows):
        return jnp.concatenate([_dot_nt(kb_s[g, rows, :], qs2[g]) for g in range(npair)], axis=1)

    def weighted(blk, p):
        return jnp.concatenate([_dot(vt_s[g, blk], p[:, g * width:(g + 1) * width]) for g in range(npair)],
                               axis=1)

    s = jnp.where(causal, scores(pl.ds(r0, bs)), NEG)
    m = jnp.max(s, axis=0, keepdims=True)
    p = jnp.exp2(s - m)
    carry = (m, jnp.sum(p, axis=0, keepdims=True), weighted(i, p.astype(BF16)))

    def kvpair(jj, carry):
        m, l, acc = carry
        s = scores(pl.ds(pl.multiple_of(jj * 2 * bs, 2 * bs), 2 * bs))
        s0 = jnp.where(sel_s[2 * jj] > 0.5, s[:bs], NEG)
        s1 = jnp.where(sel_s[2 * jj + 1] > 0.5, s[bs:], NEG)
        m_new = jnp.maximum(m, jnp.maximum(jnp.max(s0, axis=0, keepdims=True),
                                           jnp.max(s1, axis=0, keepdims=True)))
        alpha = jnp.exp2(m - m_new)
        p0 = jnp.exp2(s0 - m_new)
        p1 = jnp.exp2(s1 - m_new)
        l = alpha * l + jnp.sum(p0, axis=0, keepdims=True) + jnp.sum(p1, axis=0, keepdims=True)
        acc = alpha * acc + weighted(2 * jj, p0.astype(BF16)) + weighted(2 * jj + 1, p1.astype(BF16))
        return m_new, l, acc

    m, l, acc = lax.fori_loop(0, (i + 1) // 2, kvpair, carry)
    out = acc / l
    for g in range(npair):
        og = out[:, g * width:(g + 1) * width]
        o_ref[:, g * LANES:(g + 1) * LANES] = jnp.where(rowd < HEAD_DIM, og[:, :bs], og[:, bs:]).T


def _moba_prompt(q, k, v, *, bsz, t):
    nblk = t // MOBA_BLOCK
    assert t % MOBA_BLOCK == 0 and nblk % 2 == 0 and nblk <= LANES
    nblk_rows = -(-nblk // SUBLANES) * SUBLANES
    n = bsz * t
    gw = MOBA_PAIRS * LANES
    q_spec = pl.BlockSpec((MOBA_BLOCK, gw), lambda b, hp, i: (b * nblk + i, hp))
    kv_spec = pl.BlockSpec((t, gw), lambda b, hp, i: (b, hp), pipeline_mode=pl.Buffered(1))
    return pl.pallas_call(
        functools.partial(_moba_prompt_kernel, nblk=nblk),
        grid=(bsz, ATT_WIDTH // gw, nblk),
        in_specs=[q_spec, kv_spec, kv_spec], out_specs=q_spec,
        out_shape=jax.ShapeDtypeStruct((n, ATT_WIDTH), F32),
        scratch_shapes=[pltpu.VMEM((MOBA_PAIRS, t, LANES), BF16),
                        pltpu.VMEM((MOBA_PAIRS, nblk, LANES, MOBA_BLOCK), BF16),
                        pltpu.VMEM((MOBA_PAIRS, nblk_rows, LANES), F32),
                        pltpu.VMEM((nblk, 1, MOBA_PAIRS * 2 * MOBA_BLOCK), F32)],
        name="moba_prompt")(q, k, v)


def _expand_heads(q, qexp_s):
    t = q.shape[0]
    lane = lax.broadcasted_iota(jnp.int32, (1, ATT_WIDTH), 1)
    for h in range(HEADS):
        lm = (lane >= h * HEAD_DIM) & (lane < (h + 1) * HEAD_DIM)
        qexp_s[h * t:(h + 1) * t, :] = jnp.where(lm, q, 0.0)


def _collapse_heads(acc, t):
    lane = lax.broadcasted_iota(jnp.int32, (1, ATT_WIDTH), 1)
    out = jnp.zeros((t, ATT_WIDTH), F32)
    for h in range(HEADS):
        lm = (lane >= h * HEAD_DIM) & (lane < (h + 1) * HEAD_DIM)
        out = out + jnp.where(lm, acc[h * t:(h + 1) * t, :], 0.0)
    return out


def _pad_rows(x_ref, new_s):
    new_s[...] = jnp.zeros(new_s.shape, F32)
    new_s[0:x_ref.shape[0], :] = x_ref[...]
    return new_s[...].astype(BF16)


def _moba_sample_kernel(pt_ref, q_ref, kn_ref, vn_ref, *rest, n_pages, t):
    pps = PAGES_PER_STEP
    k_refs = rest[:pps]
    v_refs = rest[pps:2 * pps]
    o_ref = rest[2 * pps]
    qexp_s, new_s, s_all, gate_s, bmax_s, sel_s, m_s, l_s, acc_s = rest[2 * pps + 1:]
    ns = n_pages // pps
    s_id = pl.program_id(1)
    rows = HEADS * t
    col = lax.broadcasted_iota(jnp.int32, (rows, LANES), 1)
    scale = HEAD_DIM ** -0.5
    pages_per_block = MOBA_BLOCK // PAGE

    @pl.when(s_id == 0)
    def _():
        _expand_heads(q_ref[...], qexp_s)
        gate_s[...] = jnp.zeros(gate_s.shape, F32)
        bmax_s[...] = jnp.full(bmax_s.shape, NEG, F32)

    @pl.when(s_id < ns)
    def _():
        qe = qexp_s[...].astype(BF16)
        g = gate_s[...]
        bm = bmax_s[...]
        for i in range(pps):
            pg = s_id * pps + i
            s = _dot(qe, k_refs[i][...].astype(BF16))
            s_all[pg] = s
            mine = col == pg // pages_per_block
            g = g + jnp.where(mine, jnp.sum(s, axis=-1, keepdims=True) * (1.0 / MOBA_BLOCK), 0.0)
            bm = jnp.where(mine, jnp.maximum(bm, jnp.max(s, axis=-1, keepdims=True)), bm)
        gate_s[...] = g
        bmax_s[...] = bm

    @pl.when(s_id == ns)
    def _():
        sel = _top_blocks(gate_s[...], n_pages // pages_per_block, col)
        sel_s[...] = sel
        qe = qexp_s[...].astype(BF16)
        sn = _dot_nt(qe, _pad_rows(kn_ref, new_s)) * scale
        rowt = lax.broadcasted_iota(jnp.int32, (rows, LANES), 0) % t
        sn = jnp.where(col <= rowt, sn, NEG)
        m = jnp.maximum(jnp.max(sn, axis=-1, keepdims=True),
                        jnp.max(jnp.where(sel > 0.5, bmax_s[...] * scale, NEG), axis=-1, keepdims=True))
        pn = jnp.exp(sn - m)
        m_s[...] = jnp.broadcast_to(m, m_s.shape)
        l_s[...] = jnp.broadcast_to(jnp.sum(pn, axis=-1, keepdims=True), l_s.shape)
        acc_s[...] = _dot(pn.astype(BF16), _pad_rows(vn_ref, new_s))

    @pl.when(s_id >= ns)
    def _():
        sel = sel_s[...]
        m = m_s[:, 0:1]
        l = l_s[:, 0:1]
        acc = acc_s[...]
        for i in range(pps):
            pg = (s_id - ns) * pps + i
            selc = jnp.sum(jnp.where(col == pg // pages_per_block, sel, 0.0), axis=-1, keepdims=True)
            p = jnp.exp(jnp.where(selc > 0.5, s_all[pg] * scale, NEG) - m)
            l = l + jnp.sum(p, axis=-1, keepdims=True)
            acc = acc + _dot_nt(p.astype(BF16), v_refs[i][...].astype(BF16))
        l_s[...] = jnp.broadcast_to(l, l_s.shape)
        acc_s[...] = acc

    @pl.when(s_id == 2 * ns - 1)
    def _():
        o_ref[...] = _collapse_heads(acc_s[...] / l_s[:, 0:1], t)


def _sample_specs(page_table, n_pages, t, page_index, pps=PAGES_PER_STEP):
    row_spec = pl.BlockSpec((t, ATT_WIDTH), lambda b, s, pt: (b, 0))

    def page_spec(i, which):
        return pl.BlockSpec((None, ATT_WIDTH, PAGE),
                            lambda b, s, pt: (pt[b, page_index(s, i, which)], 0, 0))

    in_specs = ([row_spec] * 3 + [page_spec(i, 0) for i in range(pps)]
                + [page_spec(i, 1) for i in range(pps)])
    return in_specs, row_spec


def _moba_sample(q, k_new, v_new, k_pool, v_pool, page_table, *, bsz, t):
    n_pages = page_table.shape[1]
    pps = PAGES_PER_STEP
    assert n_pages % pps == 0 and (n_pages * PAGE) % MOBA_BLOCK == 0 and t <= SUBLANES
    assert n_pages * PAGE // MOBA_BLOCK <= LANES
    ns = n_pages // pps
    rows = HEADS * t

    def page_index(s, i, which):
        step = jnp.minimum(s, ns - 1) if which == 0 else jnp.maximum(s - ns, 0)
        return step * pps + i

    in_specs, out_spec = _sample_specs(page_table, n_pages, t, page_index)
    grid_spec = pltpu.PrefetchScalarGridSpec(
        num_scalar_prefetch=1, grid=(bsz, 2 * ns), in_specs=in_specs, out_specs=out_spec,
        scratch_shapes=[pltpu.VMEM((rows, ATT_WIDTH), F32),
                        pltpu.VMEM((PAGE, ATT_WIDTH), F32),
                        pltpu.VMEM((n_pages, rows, PAGE), F32)]
                       + [pltpu.VMEM((rows, LANES), F32)] * 5
                       + [pltpu.VMEM((rows, ATT_WIDTH), F32)])
    return pl.pallas_call(
        functools.partial(_moba_sample_kernel, n_pages=n_pages, t=t),
        grid_spec=grid_spec,
        out_shape=jax.ShapeDtypeStruct((bsz * t, ATT_WIDTH), F32),
        name="moba_sample")(page_table, q, k_new, v_new, *([k_pool] * pps), *([v_pool] * pps))


def _sb_weights(z, r, tri, strict):
    n = z.shape[0]
    lg = jnp.log2(1.0 + jnp.exp2(-jnp.abs(z)))
    log_beta = jnp.minimum(z, 0.0) - lg
    log_keep = jnp.minimum(-z, 0.0) - lg
    if strict is not None:
        log_beta = jnp.where(strict, log_beta, NEG)
        log_keep = jnp.where(strict, log_keep, 0.0)
    st = _dot(jnp.concatenate(_split3(log_keep), axis=0), tri)
    later = st[:n] + st[n:2 * n] + st[2 * n:]
    w = jnp.exp2(log_beta + later + r)
    return w.astype(BF16), r + jnp.sum(log_keep, axis=-1, keepdims=True)


def _suffix_matrix(n):
    return (lax.broadcasted_iota(jnp.int32, (n, n), 0)
            > lax.broadcasted_iota(jnp.int32, (n, n), 1)).astype(BF16)


def _sb_prompt_kernel(q_ref, k_ref, v_ref, o_ref, kb_s, vb_s):
    bs = SB_BLOCK
    i = pl.program_id(2)

    @pl.when(i == 0)
    def _():
        for j in range(k_ref.shape[0] // bs):
            kb_s[j * bs:(j + 1) * bs, :] = k_ref[j * bs:(j + 1) * bs, :].astype(BF16)
            vb_s[j * bs:(j + 1) * bs, :] = v_ref[j * bs:(j + 1) * bs, :].astype(BF16)

    lane = lax.broadcasted_iota(jnp.int32, (1, LANES), 1)
    tri = _suffix_matrix(bs)
    strict = (lax.broadcasted_iota(jnp.int32, (2 * bs, bs), 1)
              < lax.broadcasted_iota(jnp.int32, (2 * bs, bs), 0) % bs)
    qf = q_ref[...]
    qhs = []
    for hd in range(2):
        lm = (lane >= HEAD_DIM * hd) & (lane < HEAD_DIM * (hd + 1))
        qhs.append(jnp.where(lm, qf, 0.0))
    qs2 = (jnp.concatenate(qhs, axis=0) * (HEAD_DIM ** -0.5 * LOG2E)).astype(BF16)

    def block(j, r, mask):
        c0 = pl.multiple_of(j * bs, bs)
        w, r = _sb_weights(_dot_nt(qs2, kb_s[pl.ds(c0, bs), :]), r, tri, mask)
        return _dot(w, vb_s[pl.ds(c0, bs), :]), r

    acc, r = block(i, jnp.zeros((2 * bs, 1), F32), strict)
    has_prev = lax.broadcasted_iota(jnp.int32, (2 * bs, bs), 0) < jnp.where(i > 0, 2 * bs, 0)
    d, r = block(jnp.maximum(i - 1, 0), r, has_prev)
    acc = acc + d

    def cond(c):
        j, r, _ = c
        return (j >= 0) & (jnp.max(r) > EXP2_UNDERFLOW)

    def body(c):
        j, r, acc = c
        d, r = block(j, r, None)
        return j - 1, r, acc + d

    _, _, acc = lax.while_loop(cond, body, (i - 2, r, acc))
    o_ref[...] = jnp.where(lane < HEAD_DIM, acc[:bs], acc[bs:])


def _sb_prompt(q, k, v, *, bsz, t):
    assert t % SB_BLOCK == 0
    n = bsz * t
    nq = t // SB_BLOCK
    q_spec = pl.BlockSpec((SB_BLOCK, LANES), lambda b, hp, i: (b * nq + i, hp))
    kv_spec = pl.BlockSpec((t, LANES), lambda b, hp, i: (b, hp))
    return pl.pallas_call(
        _sb_prompt_kernel,
        grid=(bsz, ATT_WIDTH // LANES, nq),
        in_specs=[q_spec, kv_spec, kv_spec], out_specs=q_spec,
        out_shape=jax.ShapeDtypeStruct((n, ATT_WIDTH), F32),
        scratch_shapes=[pltpu.VMEM((t, LANES), BF16), pltpu.VMEM((t, LANES), BF16)],
        name="sb_prompt")(q, k, v)


def _sb_sample_kernel(pt_ref, q_ref, a_ref, b_ref, *rest, t, first, pps):
    k_refs = rest[:pps]
    v_refs = rest[pps:2 * pps]
    n_out = 3 if first else 1
    o_ref = rest[2 * pps]
    qexp_s, new_s, r_s, acc_s = rest[2 * pps + n_out:]
    s_id = pl.program_id(1)
    rows = HEADS * t
    tri = _suffix_matrix(PAGE)

    @pl.when(s_id == 0)
    def _():
        _expand_heads(q_ref[...] * (HEAD_DIM ** -0.5 * LOG2E), qexp_s)
        if first:
            z = _dot_nt(qexp_s[...].astype(BF16), _pad_rows(a_ref, new_s))
            col = lax.broadcasted_iota(jnp.int32, (rows, PAGE), 1)
            rowt = lax.broadcasted_iota(jnp.int32, (rows, PAGE), 0) % t
            w, r = _sb_weights(z, jnp.zeros((rows, 1), F32), tri, col < rowt)
            acc_s[...] = _dot(w, _pad_rows(b_ref, new_s))
            r_s[...] = jnp.broadcast_to(r, r_s.shape)
        else:
            acc_s[...] = a_ref[...]
            r_s[...] = b_ref[...]

    for i in range(pps):
        @pl.when(jnp.max(r_s[...]) > EXP2_UNDERFLOW)
        def _(i=i):
            z = _dot(qexp_s[...].astype(BF16), k_refs[i][...].astype(BF16))
            w, r = _sb_weights(z, r_s[:, 0:1], tri, None)
            acc_s[...] = acc_s[...] + _dot_nt(w, v_refs[i][...].astype(BF16))
            r_s[...] = jnp.broadcast_to(r, r_s.shape)

    @pl.when(s_id == pl.num_programs(1) - 1)
    def _():
        o_ref[...] = _collapse_heads(acc_s[...], t)
        if first:
            rest[2 * pps + 1][...] = acc_s[...]
            rest[2 * pps + 2][...] = r_s[...]


def _sb_sample(q, k_new, v_new, k_pool, v_pool, page_table, *, bsz, t):
    n_pages = page_table.shape[1]
    assert t <= SUBLANES
    first = min(SB_FIRST_PAGES, n_pages)
    rows = HEADS * t
    n = bsz * t
    scratch = [pltpu.VMEM((rows, ATT_WIDTH), F32), pltpu.VMEM((PAGE, ATT_WIDTH), F32),
               pltpu.VMEM((rows, LANES), F32), pltpu.VMEM((rows, ATT_WIDTH), F32)]
    acc_spec = pl.BlockSpec((rows, ATT_WIDTH), lambda b, s, pt: (b, 0))
    r_spec = pl.BlockSpec((rows, LANES), lambda b, s, pt: (b, 0))
    y_sds = jax.ShapeDtypeStruct((n, ATT_WIDTH), F32)

    in_specs, y_spec = _sample_specs(page_table, n_pages, t, lambda s, i, which: n_pages - 1 - i, first)
    y, acc, r = pl.pallas_call(
        functools.partial(_sb_sample_kernel, t=t, first=True, pps=first),
        grid_spec=pltpu.PrefetchScalarGridSpec(
            num_scalar_prefetch=1, grid=(bsz, 1), in_specs=in_specs,
            out_specs=[y_spec, acc_spec, r_spec], scratch_shapes=scratch),
        out_shape=[y_sds, jax.ShapeDtypeStruct((bsz * rows, ATT_WIDTH), F32),
                   jax.ShapeDtypeStruct((bsz * rows, LANES), F32)],
        name="sb_sample")(page_table, q, k_new, v_new, *([k_pool] * first), *([v_pool] * first))
    older = n_pages - first
    if older == 0:
        return y
    pps = max(d for d in range(1, PAGES_PER_STEP + 1) if older % d == 0)

    def older_pages():
        specs, _ = _sample_specs(page_table, n_pages, t,
                                 lambda s, i, which: n_pages - 1 - first - (s * pps + i), pps)
        specs = [specs[0], acc_spec, r_spec] + specs[3:]
        return pl.pallas_call(
            functools.partial(_sb_sample_kernel, t=t, first=False, pps=pps),
            grid_spec=pltpu.PrefetchScalarGridSpec(
                num_scalar_prefetch=1, grid=(bsz, older // pps), in_specs=specs,
                out_specs=y_spec, scratch_shapes=scratch),
            out_shape=y_sds,
            name="sb_sample_older")(page_table, q, acc, r, *([k_pool] * pps), *([v_pool] * pps))

    return lax.cond(jnp.max(r) > EXP2_UNDERFLOW, older_pages, lambda: y)


def _hgrn_kernel(q_ref, f_ref, i_ref, g_ref, lb_ref, s0_ref, gn_ref, o_ref, s_out_ref, st_s,
                 *, layer):
    tb = q_ref.shape[0]
    t = pl.program_id(1)

    @pl.when(t == 0)
    def _():
        for hh in range(HGRN_HEADS):
            st_s[hh] = s0_ref[0, hh].T

    lbp = lb_ref[...]
    e = jnp.exp(lbp - jnp.max(lbp, axis=0, keepdims=True))
    soft = e / jnp.sum(e, axis=0, keepdims=True)
    lbv = jnp.sum(soft[1:layer + 1], axis=0, keepdims=True)

    pad = max(HGRN_SUB - tb, 0)
    L = min(HGRN_CHUNK, tb + pad)
    c = min(HGRN_SUB, L)
    tri = (lax.broadcasted_iota(jnp.int32, (L, L), 0)
           >= lax.broadcasted_iota(jnp.int32, (L, L), 1)).astype(BF16)
    rowc = lax.broadcasted_iota(jnp.int32, (c, 1), 0)
    gn = gn_ref[...]

    def padrows(x):
        if pad == 0:
            return x
        return jnp.concatenate([x, jnp.zeros((pad, x.shape[1]), F32)], axis=0)

    def chunk(rows, n_valid):
        for hh in range(HGRN_HEADS):
            ls = slice(hh * HGRN_DIM, (hh + 1) * HGRN_DIM)
            lbh = lbv[:, ls]
            f = lbh + (1.0 - lbh) * _sigmoid(f_ref[rows, ls])
            lf = padrows(jnp.log(f))
            kk = padrows(1.0 - f)
            qv = padrows(q_ref[rows, ls])
            vv = padrows(i_ref[rows, ls])
            hi, mid, lo = _split3(lf)
            b = _dot(tri, hi) + _dot(tri, mid) + _dot(tri, lo)
            st = st_s[hh]
            o_inter = _dot_nt((qv * jnp.exp(b)).astype(BF16), st.astype(BF16))
            vb = vv.astype(BF16)
            parts = []
            for si in range(L // c):
                rs = slice(si * c, (si + 1) * c)
                b_i = b[rs]
                q_i = qv[rs]
                k_i = kk[rs]
                v_i = vv[rs]
                o_i = o_inter[rs]
                if si > 0:
                    b_prev = b[si * c - 1:si * c]
                    qt = (q_i * jnp.exp(b_i - b_prev)).astype(BF16)
                    kt = (kk[:si * c] * jnp.exp(b_prev - b[:si * c])).astype(BF16)
                    o_i = o_i + _dot(_dot_nt(qt, kt).astype(BF16), vb[:si * c])
                for s in range(min(c, max(n_valid - si * c, 0))):
                    dec = jnp.exp(jnp.minimum(b_i - b_i[s:s + 1], 0.0))
                    a = jnp.sum(q_i * k_i[s:s + 1] * dec, axis=-1, keepdims=True)
                    o_i = o_i + jnp.where(rowc >= s, a, 0.0) * v_i[s:s + 1]
                parts.append(o_i)
            o = jnp.concatenate(parts, axis=0) if len(parts) > 1 else parts[0]
            o = o[:n_valid]
            o = o * lax.rsqrt(jnp.mean(o * o, axis=-1, keepdims=True) + EPS) * gn
            gv = g_ref[rows, ls]
            o_ref[rows, ls] = o * (gv * _sigmoid(gv))
            b_last = b[L - 1:L]
            kdec = (kk * jnp.exp(b_last - b)).astype(BF16)
            st_s[hh] = st * jnp.exp(b_last) + _dot_tn(vb, kdec)

    if pad:
        chunk(slice(0, tb), tb)
    else:
        def body(ch, _):
            chunk(pl.ds(pl.multiple_of(ch * L, L), L), L)
            return 0

        lax.fori_loop(0, tb // L, body, 0)

    for hh in range(HGRN_HEADS):
        s_out_ref[0, hh] = st_s[hh].T


def _hgrn(qh, fh, ih, gh, lb, s0, gnorm, *, bsz, t, layer):
    n = bsz * t
    tb = min(ROW_TILE, t)
    nt = t // tb
    row_spec = pl.BlockSpec((tb, SLAB), lambda b, i: (b * nt + i, 0))
    st_spec = pl.BlockSpec((1, HGRN_HEADS, HGRN_DIM, HGRN_DIM), lambda b, i: (b, 0, 0, 0))
    return pl.pallas_call(
        functools.partial(_hgrn_kernel, layer=layer),
        grid=(bsz, nt),
        in_specs=[row_spec] * 4 + [_const(lb.shape), st_spec, _const((1, HGRN_DIM))],
        out_specs=[row_spec, st_spec],
        out_shape=[jax.ShapeDtypeStruct((n, SLAB), F32),
                   jax.ShapeDtypeStruct((bsz, HGRN_HEADS, HGRN_DIM, HGRN_DIM), F32)],
        scratch_shapes=[pltpu.VMEM((HGRN_HEADS, HGRN_DIM, HGRN_DIM), F32)],
        name="hgrn2")(qh, fh, ih, gh, lb, s0, gnorm.reshape(1, HGRN_DIM))


def _tail_kernel(*refs, chain):
    if chain:
        (h_ref, ya_ref, yb_ref, p_ref, wo_ref, gmp_ref, gfp_ref, wup_ref, wcv_ref, wdn_ref,
         gfo_ref, wpp_ref, wpg_ref, gpl_ref, ho_ref, ffn_ref, acc_s, u_s, h1_s, a_s, act_s, car_s) = refs
    else:
        (h_ref, ya_ref, yb_ref, p_ref, p1_ref, p2_ref, wo_ref, gmp_ref, gfp_ref, wup_ref,
         wcv_ref, wdn_ref, gfo_ref, wpp_ref, wpg_ref, gpl_ref, ho_ref, ffn_ref, acc_s, u_s, h1_s, a_s, act_s) = refs
    tm = h_ref.shape[0]
    nch = N_FF_CHUNKS
    cw = FF_CHUNK
    mix = _dot(ya_ref[...].astype(BF16), wo_ref[0]) + _dot(yb_ref[...].astype(BF16), wo_ref[1])
    h1 = h_ref[...] + _rms(mix, gmp_ref[...])
    h1_s[...] = h1
    a_s[...] = _rms(h1, gfp_ref[...]).astype(BF16)
    acc_s[...] = jnp.zeros(acc_s.shape, F32)
    rowi = lax.broadcasted_iota(jnp.int32, (tm, cw), 0)

    if chain:
        @pl.when(pl.program_id(1) == 0)
        def _():
            car_s[...] = jnp.zeros(car_s.shape, F32)

    def taps(w, p2, p1, u):
        return w[3:4] + w[0:1] * p2 + w[1:2] * p1 + w[2:3] * u

    def up(c):
        a = a_s[...]
        for half in range(2):
            cols = slice(half * D_FF + c * cw, half * D_FF + (c + 1) * cw)
            u_s[c % 2, half, SUBLANES:, :] = _dot(a, wup_ref[:, cols])

    def down(c):
        slot = c % 2
        ws = [wcv_ref[:, half * D_FF + c * cw:half * D_FF + (c + 1) * cw] for half in range(2)]
        if chain:
            for half in range(2):
                u_s[slot, half, 0:SUBLANES, :] = car_s[half * nch + c]
            for rb in range(tm // CONV_ROWS):
                cs = []
                for half in range(2):
                    x = u_s[slot, half, rb * CONV_ROWS:(rb + 1) * CONV_ROWS + SUBLANES, :]
                    cs.append(taps(ws[half], pltpu.roll(x, 2, 0)[SUBLANES:],
                                   pltpu.roll(x, 1, 0)[SUBLANES:], x[SUBLANES:]))
                act_s[rb * CONV_ROWS:(rb + 1) * CONV_ROWS, :] = (_gelu(cs[0]) * cs[1]).astype(BF16)
            for half in range(2):
                last = u_s[slot, half, tm:tm + SUBLANES, :]
                car_s[half * nch + c] = last
                ffn_ref[0, half * nch + c] = last
        else:
            rm = rowi % SUBLANES
            cs = []
            for half in range(2):
                u = u_s[slot, half, SUBLANES:, :]
                p1 = jnp.where(rm == 0, p1_ref[half * nch + c], pltpu.roll(u, 1, 0))
                p2 = jnp.where(rm < 2, p2_ref[half * nch + c], pltpu.roll(u, 2, 0))
                cs.append(taps(ws[half], p2, p1, u))
                ffn_ref[half * nch + c] = u
            act_s[...] = (_gelu(cs[0]) * cs[1]).astype(BF16)
        acc_s[...] += _dot(act_s[...], wdn_ref[c * cw:(c + 1) * cw, :])

    up(0)
    for c in range(nch):
        if c + 1 < nch:
            up(c + 1)
        down(c)
    h2 = h1_s[...] + _rms(acc_s[...], gfo_ref[...])
    ple = _dot(p_ref[...].astype(BF16), wpp_ref[...]) * _sigmoid(_dot(h2.astype(BF16), wpg_ref[...]))
    ho_ref[...] = h2 + _rms(ple, gpl_ref[...])


def _tail(h, ya, yb, p, ffn0, w, *, bsz, t):
    n, d = h.shape
    nch = N_FF_CHUNKS
    cw = FF_CHUNK
    chain = t > SUBLANES
    consts = [w['wo'], w['gmp'], w['gfp'], w['wup'], w['wcv'], w['wdn'], w['gfo'], w['wpp'], w['wpg'],
              w['gpl']]
    const_specs = [_const(x.shape) for x in consts]
    if chain:
        assert ffn0 is None, "a long sequence starts from an empty ConvFFN buffer"
        tm = min(MATMUL_TILE, t)
        nt = t // tm
        grid = (bsz, nt)
        rows = lambda width: pl.BlockSpec((tm, width), lambda b, i: (b * nt + i, 0))
        ffn_spec = pl.BlockSpec((1, 2 * nch, SUBLANES, cw), lambda b, i: (b, 0, 0, 0))
        ffn_sds = jax.ShapeDtypeStruct((bsz, 2 * nch, SUBLANES, cw), F32)
        extra, extra_specs = [], []
        scratch = [pltpu.VMEM((tm, d), F32), pltpu.VMEM((2, 2, tm + SUBLANES, cw), F32),
                   pltpu.VMEM((tm, d), F32), pltpu.VMEM((tm, d), BF16), pltpu.VMEM((tm, cw), BF16),
                   pltpu.VMEM((2 * nch, SUBLANES, cw), F32)]
    else:
        assert t == SUBLANES
        tm = n
        grid = (1, 1)
        rows = lambda width: pl.BlockSpec((tm, width), lambda b, i: (0, 0))
        ffn_spec = _full((2 * nch, tm, cw))
        ffn_sds = jax.ShapeDtypeStruct((2 * nch, tm, cw), F32)
        buf = ffn0.astype(F32).reshape(bsz, 2, 2 * nch, cw).transpose(2, 0, 1, 3)
        zero = jnp.zeros((2 * nch, bsz, SUBLANES - 2, cw), F32)
        p1 = jnp.concatenate([buf[:, :, 1:2], zero, zero[:, :, :1]], axis=2).reshape(2 * nch, tm, cw)
        p2 = jnp.concatenate([buf, zero], axis=2).reshape(2 * nch, tm, cw)
        extra = [p1, p2]
        extra_specs = [_full(p1.shape), _full(p2.shape)]
        scratch = [pltpu.VMEM((tm, d), F32), pltpu.VMEM((2, 2, tm + SUBLANES, cw), F32),
                   pltpu.VMEM((tm, d), F32), pltpu.VMEM((tm, d), BF16), pltpu.VMEM((tm, cw), BF16)]
    ho, ffn = pl.pallas_call(
        functools.partial(_tail_kernel, chain=chain),
        grid=grid,
        in_specs=[rows(d), rows(SLAB), rows(SLAB), rows(p.shape[1])] + extra_specs + const_specs,
        out_specs=[rows(d), ffn_spec],
        out_shape=[jax.ShapeDtypeStruct((n, d), F32), ffn_sds],
        scratch_shapes=scratch,
        name="layer_tail")(h, ya, yb, p, *extra, *consts)
    if chain:
        st = ffn[:, :, SUBLANES - 2:, :]
    else:
        st = ffn.reshape(2 * nch, bsz, SUBLANES, cw)[:, :, SUBLANES - 2:, :].transpose(1, 0, 2, 3)
    return ho, st.transpose(0, 2, 1, 3).reshape(bsz, 2, 2 * nch * cw)


def _tail_weights(i, w_out, g_mix_post, g_ffn_pre, g_ffn_post, w_ffn_up, w_ffn_conv, b_ffn_conv,
                  w_ffn_down, w_ple_proj, w_ple_gate, g_ple):
    d = w_out.shape[1]
    taps = jnp.concatenate([w_ffn_conv[i].astype(F32), b_ffn_conv[i].astype(F32)[None]], axis=0)
    row = lambda g: g.astype(F32).reshape(1, d)
    return dict(
        wo=w_out.astype(BF16).reshape(2, SLAB, d), gmp=row(g_mix_post[i]), gfp=row(g_ffn_pre[i]),
        wup=w_ffn_up[i].astype(BF16), wcv=taps, wdn=w_ffn_down[i].astype(BF16),
        gfo=row(g_ffn_post[i]), wpp=w_ple_proj[i].astype(BF16), wpg=w_ple_gate[i].astype(BF16),
        gpl=row(g_ple[i]))


def kernel(x_prompt, x_sample, p_prompt, p_sample, cache_moba_k, cache_moba_v, state_s5_re, state_s5_im,
           state_hgrn, cache_sb_k, cache_sb_v, state_ffn, page_table,
           g_mix_pre, g_mix_post, g_ffn_pre, g_ffn_post, w_ffn_up, w_ffn_conv, b_ffn_conv, w_ffn_down,
           w_ple_proj, w_ple_gate, g_ple,
           w_in_a, w_out_a, s5_a_re, s5_a_im, s5_log_dt, s5_b_re, s5_b_im, s5_c_re, s5_c_im, s5_d,
           s5_w_glu, s5_b_glu,
           w_in_c, w_out_c, hgrn_lb, g_hgrn_norm):
    depth = g_mix_pre.shape[0]
    d_model = x_prompt.shape[-1]
    n_pages = page_table.shape[1]
    past_len = n_pages * cache_moba_k.shape[2]
    page_table = page_table.astype(jnp.int32)

    layers = []
    for i in range(depth):
        j = i // 2
        lw = {}
        if i % 2 == 0:
            lw['w_in'] = w_in_a[j].astype(BF16)
            lw['disc'] = _s5_discretize(s5_a_re[j], s5_a_im[j], s5_log_dt[j])
            lw['wb'], lw['wc'] = _s5_weights(s5_b_re[j], s5_b_im[j], s5_c_re[j], s5_c_im[j])
            lw['d'] = s5_d[j].astype(F32).reshape(S5_WIDTH)
            lw['wglu'] = s5_w_glu[j].astype(BF16)
            lw['bglu'] = s5_b_glu[j].astype(F32)
            w_out = w_out_a[j]
        else:
            lw['w_in'] = w_in_c[j].astype(BF16)
            w_out = w_out_c[j]
        lw['tail'] = _tail_weights(i, w_out, g_mix_post, g_ffn_pre, g_ffn_post, w_ffn_up, w_ffn_conv,
                                   b_ffn_conv, w_ffn_down, w_ple_proj, w_ple_gate, g_ple)
        layers.append(lw)

    def pool2d(pool, j):
        return pool[j].transpose(0, 2, 3, 1).reshape(pool.shape[1], ATT_WIDTH, pool.shape[2])

    def run(x, p, q_start, s5_re0, s5_im0, hgrn0, ffn0, has_past):
        bsz, t, _ = x.shape
        n = bsz * t
        h = x.astype(F32).reshape(n, d_model)
        tables = _rope_tables(t, q_start)
        if t < MATMUL_TILE:
            tables = tuple(jnp.tile(tb, (n // t, 1)) for tb in tables)
        mk, mv, sr, si, hs, sk, sv, fb = [], [], [], [], [], [], [], []
        long_seq = t % MATMUL_TILE == 0

        def kv_out(x, x_t, which):
            if x_t:
                return x_t[which].reshape(bsz, HEADS, HEAD_DIM, t).transpose(0, 3, 1, 2)
            return x.reshape(bsz, t, HEADS, HEAD_DIM)

        for i in range(depth):
            j = i // 2
            lw = layers[i]
            g_pre = g_mix_pre[i].astype(F32)
            if i % 2 == 0:
                u, q, k, v, *kv_t = _inproj(h, g_pre, lw['w_in'], tables, rope_slabs=(1, 2),
                                            t_slabs=(2, 3) if long_seq else (), seq=(bsz, t))
                y_a, hr, hi = _s5(u, s5_re0[j].astype(F32).reshape(bsz, 1, S5_LANES),
                                  s5_im0[j].astype(F32).reshape(bsz, 1, S5_LANES), lw['disc'], lw['wb'],
                                  lw['wc'], lw['d'], lw['wglu'], lw['bglu'], bsz=bsz, t=t)
                if has_past:
                    y_b = _moba_sample(q, k, v, pool2d(cache_moba_k, j), pool2d(cache_moba_v, j),
                                       page_table, bsz=bsz, t=t)
                else:
                    y_b = _moba_prompt(q, k, v, bsz=bsz, t=t)
                mk.append(kv_out(k, kv_t, 0))
                mv.append(kv_out(v, kv_t, 1))
                sr.append(hr.reshape(bsz, S5_GROUPS, S5_STATE))
                si.append(hi.reshape(bsz, S5_GROUPS, S5_STATE))
            else:
                qh, fh, ih, gh, q, k, v, *kv_t = _inproj(h, g_pre, lw['w_in'], tables, rope_slabs=(),
                                                         t_slabs=(5, 6) if long_seq else (), seq=(bsz, t))
                y_a, s_fin = _hgrn(qh, fh, ih, gh, hgrn_lb.astype(F32), hgrn0[j].astype(F32),
                                   g_hgrn_norm[j].astype(F32), bsz=bsz, t=t, layer=i)
                if has_past:
                    y_b = _sb_sample(q, k, v, pool2d(cache_sb_k, j), pool2d(cache_sb_v, j), page_table,
                                     bsz=bsz, t=t)
                else:
                    y_b = _sb_prompt(q, k, v, bsz=bsz, t=t)
                hs.append(s_fin)
                sk.append(kv_out(k, kv_t, 0))
                sv.append(kv_out(v, kv_t, 1))
            h, buf = _tail(h, y_a, y_b, p[i].astype(F32).reshape(n, p.shape[-1]),
                           None if ffn0 is None else ffn0[i], lw['tail'], bsz=bsz, t=t)
            fb.append(buf)
        return (h.reshape(bsz, t, d_model), jnp.stack(mk), jnp.stack(mv), jnp.stack(sr), jnp.stack(si),
                jnp.stack(hs), jnp.stack(sk), jnp.stack(sv), jnp.stack(fb))

    bp = x_prompt.shape[0]
    n_a = (depth + 1) // 2
    n_c = depth // 2
    outs_p = run(x_prompt, p_prompt, 0,
                 jnp.zeros((n_a, bp, S5_GROUPS, S5_STATE), F32), jnp.zeros((n_a, bp, S5_GROUPS, S5_STATE), F32),
                 jnp.zeros((n_c, bp, HGRN_HEADS, HGRN_DIM, HGRN_DIM), F32), None, False)
    outs_s = run(x_sample, p_sample, past_len, state_s5_re, state_s5_im, state_hgrn, state_ffn, True)
    return (outs_p[0], outs_s[0]) + tuple(outs_p[1:]) + tuple(outs_s[1:])
```

```python
import functools
import math

import jax
import jax.numpy as jnp
from jax import lax
from jax.experimental import pallas as pl
from jax.experimental.pallas import tpu as pltpu

F32 = jnp.float32
BF16 = jnp.bfloat16

S5_WIDTH = 512
S5_GROUPS = 32
S5_GROUP = 16
S5_STATE = 64
S5_LANES = S5_GROUPS * S5_STATE
HEADS = 8
HEAD_DIM = 64
ATT_WIDTH = HEADS * HEAD_DIM
ROPE_DIM = 16
ROPE_THETA = 500000.0
MOBA_BLOCK = 256
MOBA_TOPK = 3
MOBA_PAIRS = 4
HGRN_HEADS = 4
HGRN_DIM = 128
HGRN_CHUNK = 64
HGRN_SUB = 16
D_FF = 2816
FF_CHUNK = 256
N_FF_CHUNKS = D_FF // FF_CHUNK
PAGE = 128
EPS = 1e-6
NEG = -1e30
SLAB = 512
LANES = 128
SUBLANES = 8
ROW_TILE = 256
MATMUL_TILE = 512
CONV_ROWS = 128
SB_BLOCK = 256
PAGES_PER_STEP = 16
SB_FIRST_PAGES = 4
LOG2E = 1.4426950408889634
EXP2_UNDERFLOW = -150.0


def _dot(a, b):
    return jnp.dot(a, b, preferred_element_type=F32)


def _dot_nt(a, b):
    return lax.dot_general(a, b, (((1,), (1,)), ((), ())), preferred_element_type=F32)


def _dot_tn(a, b):
    return lax.dot_general(a, b, (((0,), (0,)), ((), ())), preferred_element_type=F32)


def _split3(x):
    hi = x.astype(BF16)
    r1 = x - hi.astype(F32)
    mid = r1.astype(BF16)
    lo = (r1 - mid.astype(F32)).astype(BF16)
    return hi, mid, lo


def _rms(x, g):
    return x * lax.rsqrt(jnp.mean(x * x, axis=-1, keepdims=True) + EPS) * g


def _sigmoid(x):
    return 1.0 / (1.0 + jnp.exp(-x))


def _gelu(x):
    return 0.5 * x * (1.0 + jnp.tanh(0.7978845608028654 * (x + 0.044715 * (x * x * x))))


def _full(shape):
    nd = len(shape)
    return pl.BlockSpec(shape, lambda *_: (0,) * nd)


def _const(shape):
    nd = len(shape)
    return pl.BlockSpec(shape, lambda *_: (0,) * nd, pipeline_mode=pl.Buffered(1))


def _rope_table_kernel(c_ref, s1_ref, s2_ref, *, q_start):
    rows, lanes = c_ref.shape
    i = pl.program_id(0)
    lane = lax.broadcasted_iota(jnp.int32, (rows, lanes), 1)
    pos = lax.broadcasted_iota(jnp.int32, (rows, lanes), 0) + i * rows + q_start
    d = lane % HEAD_DIM
    half = ROPE_DIM // 2
    inv = jnp.exp((d % half).astype(F32) * (-math.log(ROPE_THETA) / half))
    ang = pos.astype(F32) * inv
    cs = jnp.cos(ang)
    sn = jnp.sin(ang)
    c_ref[...] = jnp.where(d < ROPE_DIM, cs, 1.0)
    s1_ref[...] = jnp.where(d < half, -sn, 0.0)
    s2_ref[...] = jnp.where((d >= half) & (d < ROPE_DIM), sn, 0.0)


def _rope_tables(t, q_start):
    rows = min(t, 512)
    sds = jax.ShapeDtypeStruct((t, LANES), F32)
    spec = pl.BlockSpec((rows, LANES), lambda i: (i, 0))
    return pl.pallas_call(
        functools.partial(_rope_table_kernel, q_start=q_start),
        grid=(t // rows,), out_shape=(sds, sds, sds), out_specs=(spec, spec, spec),
        name="rope_tables")()


def _inproj_kernel(h_ref, g_ref, w_ref, c_ref, s1_ref, s2_ref, *out_refs, n_slabs, rope_slabs,
                   t_slabs):
    a = _rms(h_ref[...], g_ref[...]).astype(BF16)
    half = ROPE_DIM // 2
    for s in range(n_slabs):
        o_ref = out_refs[s]
        z = _dot(a, w_ref[:, s * SLAB:(s + 1) * SLAB])
        if s in rope_slabs:
            c = c_ref[...]
            s1 = s1_ref[...]
            s2 = s2_ref[...]
            for j in range(SLAB // LANES):
                x = z[:, j * LANES:(j + 1) * LANES]
                o_ref[:, j * LANES:(j + 1) * LANES] = (
                    x * c + pltpu.roll(x, LANES - half, 1) * s1 + pltpu.roll(x, half, 1) * s2)
        else:
            o_ref[...] = z
        if s in t_slabs:
            out_refs[n_slabs + t_slabs.index(s)][...] = o_ref[...].T


def _inproj(h, g, w, tables, rope_slabs, t_slabs=(), seq=None):
    n, d = h.shape
    n_slabs = w.shape[1] // SLAB
    tm = min(MATMUL_TILE, n)
    t_tiles = tables[0].shape[0] // tm
    tab_spec = pl.BlockSpec((tm, LANES), lambda i: (i % t_tiles, 0))
    out_spec = pl.BlockSpec((tm, SLAB), lambda i: (i, 0))
    out_specs = [out_spec] * n_slabs
    out_shape = [jax.ShapeDtypeStruct((n, SLAB), F32)] * n_slabs
    if t_slabs:
        bsz, t = seq
        assert t % tm == 0
        nt = t // tm
        out_specs += [pl.BlockSpec((None, SLAB, tm), lambda i: (i // nt, 0, i % nt))] * len(t_slabs)
        out_shape += [jax.ShapeDtypeStruct((bsz, SLAB, t), F32)] * len(t_slabs)
    return pl.pallas_call(
        functools.partial(_inproj_kernel, n_slabs=n_slabs, rope_slabs=rope_slabs, t_slabs=t_slabs),
        grid=(n // tm,),
        in_specs=[pl.BlockSpec((tm, d), lambda i: (i, 0)), _const((1, d)), _const(w.shape),
                  tab_spec, tab_spec, tab_spec],
        out_specs=out_specs, out_shape=out_shape,
        name="inproj")(h, g.reshape(1, d), w, *tables)


def _s5_disc_kernel(ar_ref, ai_ref, ldt_ref, pre_ref, pim_ref, cc_ref):
    ar = ar_ref[...]
    ai = ai_ref[...]
    dt = jnp.exp(ldt_ref[...])
    row = lax.broadcasted_iota(jnp.int32, (4 * SUBLANES, 1), 0)
    blk = row // SUBLANES
    r = row % SUBLANES
    n = jnp.where(blk == 0, r + 1, jnp.where(blk == 1, 1, jnp.where(blk == 2, 2, 4)))
    keep = (blk == 0) | (r >= n)
    n = n.astype(F32)
    mag = jnp.exp(ar * dt * n)
    ang = ai * dt * n
    pre = mag * jnp.cos(ang)
    pim = mag * jnp.sin(ang)
    pre_ref[...] = jnp.where(keep, pre, 0.0)
    pim_ref[...] = jnp.where(keep, pim, 0.0)
    abr = pre[0:1]
    abi = pim[0:1]
    den = ar * ar + ai * ai
    nr = abr - 1.0
    cc_ref[0:1, :] = (nr * ar + abi * ai) / den
    cc_ref[1:2, :] = (abi * ar - nr * ai) / den


def _s5_discretize(a_re, a_im, log_dt):
    ar = a_re.reshape(1, S5_LANES)
    ai = a_im.reshape(1, S5_LANES)
    ldt = jnp.repeat(log_dt, S5_STATE).reshape(1, S5_LANES)
    return pl.pallas_call(
        _s5_disc_kernel,
        out_shape=(jax.ShapeDtypeStruct((4 * SUBLANES, S5_LANES), F32),
                   jax.ShapeDtypeStruct((4 * SUBLANES, S5_LANES), F32),
                   jax.ShapeDtypeStruct((2, S5_LANES), F32)),
        name="s5_discretize")(ar, ai, ldt)


SCAN_LANES = 512


def _s5_kernel(u_ref, h0r_ref, h0i_ref, pre_ref, pim_ref, cc_ref, wb_ref, wc_ref, d_ref,
               wglu_ref, bglu_ref, y_ref, hr_out_ref, hi_out_ref, xr_s, xi_s, car_s, *, chain):
    tm = u_ref.shape[0]
    nb = tm // SUBLANES
    u = u_ref[...]
    ub = u.astype(BF16)
    half_in = S5_WIDTH // 2
    half_st = S5_LANES // 2
    for hf in range(2):
        bu = _dot(ub[:, hf * half_in:(hf + 1) * half_in], wb_ref[hf])
        bur = bu[:, :half_st]
        bui = bu[:, half_st:]
        ls = slice(hf * half_st, (hf + 1) * half_st)
        cr = cc_ref[0:1, ls]
        ci = cc_ref[1:2, ls]
        xr_s[:, ls] = cr * bur - ci * bui
        xi_s[:, ls] = cr * bui + ci * bur

    if chain:
        t = pl.program_id(1)

        @pl.when(t == 0)
        def _():
            car_s[0:1, :] = h0r_ref[0]
            car_s[1:2, :] = h0i_ref[0]

    for lc in range(S5_LANES // SCAN_LANES):
        ls = slice(lc * SCAN_LANES, (lc + 1) * SCAN_LANES)

        def body(i, carry, ls=ls):
            r0 = pl.multiple_of(i * SUBLANES, SUBLANES)
            hr = xr_s[pl.ds(r0, SUBLANES), ls]
            hi = xi_s[pl.ds(r0, SUBLANES), ls]
            for step, k in enumerate((1, 2, 4)):
                ar = pre_ref[(step + 1) * SUBLANES:(step + 2) * SUBLANES, ls]
                ai = pim_ref[(step + 1) * SUBLANES:(step + 2) * SUBLANES, ls]
                sr = pltpu.roll(hr, k, 0)
                si = pltpu.roll(hi, k, 0)
                hr, hi = hr + ar * sr - ai * si, hi + ar * si + ai * sr
            if chain:
                cr_, ci_ = carry
            else:
                cr_ = h0r_ref[i, :, ls]
                ci_ = h0i_ref[i, :, ls]
            pr8 = pre_ref[0:SUBLANES, ls]
            pi8 = pim_ref[0:SUBLANES, ls]
            hr, hi = hr + pr8 * cr_ - pi8 * ci_, hi + pr8 * ci_ + pi8 * cr_
            xr_s[pl.ds(r0, SUBLANES), ls] = hr
            xi_s[pl.ds(r0, SUBLANES), ls] = hi
            last_r = hr[SUBLANES - 1:SUBLANES]
            last_i = hi[SUBLANES - 1:SUBLANES]
            if chain:
                return last_r, last_i
            hr_out_ref[i, :, ls] = last_r
            hi_out_ref[i, :, ls] = last_i
            return carry

        if chain:
            cr_, ci_ = lax.fori_loop(0, nb, body, (car_s[0:1, ls], car_s[1:2, ls]))
            car_s[0:1, ls] = cr_
            car_s[1:2, ls] = ci_
        else:
            lax.fori_loop(0, nb, body, 0)

    if chain:
        hr_out_ref[0] = car_s[0:1, :]
        hi_out_ref[0] = car_s[1:2, :]

    ys = []
    for hf in range(2):
        ls = slice(hf * half_st, (hf + 1) * half_st)
        hcat = jnp.concatenate([xr_s[:, ls], xi_s[:, ls]], axis=1).astype(BF16)
        ys.append(_dot(hcat, wc_ref[hf]))
    y = jnp.concatenate(ys, axis=1) + d_ref[...] * u
    y = _gelu(y)
    y_ref[...] = y * _sigmoid(_dot(y.astype(BF16), wglu_ref[...]) + bglu_ref[...])


def _s5_weights(b_re, b_im, c_re, c_im):
    gh = S5_GROUPS // 2
    eye = jnp.eye(gh, dtype=F32)

    def bmat(b):
        bt = b.astype(F32).transpose(0, 2, 1).reshape(2, gh, S5_GROUP, S5_STATE)
        return jnp.einsum('xghp,gk->xghkp', bt, eye).reshape(2, gh * S5_GROUP, gh * S5_STATE)

    def cmat(c):
        ct = c.astype(F32).transpose(0, 2, 1).reshape(2, gh, S5_STATE, S5_GROUP)
        return jnp.einsum('xgph,gk->xgpkh', ct, eye).reshape(2, gh * S5_STATE, gh * S5_GROUP)

    wb = jnp.concatenate([bmat(b_re), bmat(b_im)], axis=2).astype(BF16)
    wc = jnp.concatenate([cmat(c_re), -cmat(c_im)], axis=1).astype(BF16)
    return wb, wc


def _s5(u, h0r, h0i, disc, wb, wc, dvec, wglu, bglu, *, bsz, t):
    n = bsz * t
    pre, pim, cc = disc
    chain = t > SUBLANES
    if chain:
        tm = min(MATMUL_TILE, t)
        grid = (bsz, t // tm)
        row_spec = pl.BlockSpec((tm, S5_WIDTH), lambda b, i: (b * (t // tm) + i, 0))
        st_spec = pl.BlockSpec((1, 1, S5_LANES), lambda b, i: (b, 0, 0))
    else:
        assert t == SUBLANES
        tm = n
        grid = (1, 1)
        row_spec = pl.BlockSpec((tm, S5_WIDTH), lambda b, i: (0, 0))
        st_spec = pl.BlockSpec((bsz, 1, S5_LANES), lambda b, i: (0, 0, 0))
    st_sds = jax.ShapeDtypeStruct((bsz, 1, S5_LANES), F32)
    return pl.pallas_call(
        functools.partial(_s5_kernel, chain=chain),
        grid=grid,
        in_specs=[row_spec, st_spec, st_spec, _const(pre.shape), _const(pim.shape), _const(cc.shape),
                  _const(wb.shape), _const(wc.shape), _const((1, S5_WIDTH)), _const(wglu.shape),
                  _const((1, S5_WIDTH))],
        out_specs=[row_spec, st_spec, st_spec],
        out_shape=[jax.ShapeDtypeStruct((n, S5_WIDTH), F32), st_sds, st_sds],
        scratch_shapes=[pltpu.VMEM((tm, S5_LANES), F32), pltpu.VMEM((tm, S5_LANES), F32),
                        pltpu.VMEM((2, S5_LANES), F32)],
        name="s5_mixer")(u, h0r, h0i, pre, pim, cc, wb, wc, dvec.reshape(1, S5_WIDTH), wglu,
                         bglu.reshape(1, S5_WIDTH))


def _top_blocks(gate, n_past, col):
    g = jnp.where(col < n_past, gate, NEG)
    sel = jnp.zeros(gate.shape, F32)
    colf = col.astype(F32)
    for _ in range(MOBA_TOPK):
        m = jnp.max(g, axis=-1, keepdims=True)
        first = jnp.min(jnp.where(g == m, colf, float(LANES)), axis=-1, keepdims=True)
        pick = colf == first
        sel = jnp.where(pick & (m > 0.5 * NEG), 1.0, sel)
        g = jnp.where(pick, -3e38, g)
    return sel


def _top_blocks_t(gate, n_past):
    rowf = lax.broadcasted_iota(jnp.int32, gate.shape, 0).astype(F32)
    g = jnp.where(rowf < n_past, gate, NEG)
    sel = jnp.zeros(gate.shape, F32)
    for _ in range(MOBA_TOPK):
        m = jnp.max(g, axis=0, keepdims=True)
        first = jnp.min(jnp.where(g == m, rowf, float(LANES)), axis=0, keepdims=True)
        pick = rowf == first
        sel = jnp.where(pick & (m > 0.5 * NEG), 1.0, sel)
        g = jnp.where(pick, -3e38, g)
    return sel


def _moba_prompt_kernel(q_ref, k_ref, v_ref, o_ref, kb_s, vt_s, kmean_s, sel_s, *, nblk):
    bs = MOBA_BLOCK
    npair = MOBA_PAIRS
    width = 2 * bs
    i = pl.program_id(2)

    @pl.when(i == 0)
    def _():
        kmean_s[...] = jnp.zeros(kmean_s.shape, F32)
        for g in range(npair):
            gl = slice(g * LANES, (g + 1) * LANES)
            for j in range(nblk):
                kj = k_ref[j * bs:(j + 1) * bs, gl]
                kb_s[g, j * bs:(j + 1) * bs, :] = kj.astype(BF16)
                kmean_s[g, j:j + 1, :] = jnp.sum(kj, axis=0, keepdims=True) * (1.0 / bs)
                vt_s[g, j] = v_ref[j * bs:(j + 1) * bs, gl].T.astype(BF16)

    lane = lax.broadcasted_iota(jnp.int32, (1, LANES), 1)
    rowd = lax.broadcasted_iota(jnp.int32, (LANES, 1), 0)
    causal = (lax.broadcasted_iota(jnp.int32, (bs, npair * width), 0)
              <= lax.broadcasted_iota(jnp.int32, (bs, npair * width), 1) % bs)
    r0 = pl.multiple_of(i * bs, bs)
    qs2 = []
    gates = []
    for g in range(npair):
        qf = q_ref[:, g * LANES:(g + 1) * LANES]
        qh2 = jnp.concatenate(
            [jnp.where((lane >= HEAD_DIM * hd) & (lane < HEAD_DIM * (hd + 1)), qf, 0.0) for hd in range(2)],
            axis=0)
        qs2.append((qh2 * (HEAD_DIM ** -0.5 * LOG2E)).astype(BF16))
        gates.append(_dot_nt(kmean_s[g].astype(BF16), qh2.astype(BF16)))
    sel = _top_blocks_t(jnp.concatenate(gates, axis=1), i)
    for b in range(nblk):
        sel_s[b] = sel[b:b + 1, :]

    def scores(rows):
        return jnp.concatenate([_dot_nt(kb_s[g, rows, :], qs2[g]) for g in range(npair)], axis=1)

    def weighted(blk, p):
        return jnp.concatenate([_dot(vt_s[g, blk], p[:, g * width:(g + 1) * width]) for g in range(npair)],
                               axis=1)

    s = jnp.where(causal, scores(pl.ds(r0, bs)), NEG)
    m = jnp.max(s, axis=0, keepdims=True)
    p = jnp.exp2(s - m)
    carry = (m, jnp.sum(p, axis=0, keepdims=True), weighted(i, p.astype(BF16)))

    def kvpair(jj, carry):
        m, l, acc = carry
        s = scores(pl.ds(pl.multiple_of(jj * 2 * bs, 2 * bs), 2 * bs))
        s0 = jnp.where(sel_s[2 * jj] > 0.5, s[:bs], NEG)
        s1 = jnp.where(sel_s[2 * jj + 1] > 0.5, s[bs:], NEG)
        m_new = jnp.maximum(m, jnp.maximum(jnp.max(s0, axis=0, keepdims=True),
                                           jnp.max(s1, axis=0, keepdims=True)))
        alpha = jnp.exp2(m - m_new)
        p0 = jnp.exp2(s0 - m_new)
        p1 = jnp.exp2(s1 - m_new)
        l = alpha * l + jnp.sum(p0, axis=0, keepdims=True) + jnp.sum(p1, axis=0, keepdims=True)
        acc = alpha * acc + weighted(2 * jj, p0.astype(BF16)) + weighted(2 * jj + 1, p1.astype(BF16))
        return m_new, l, acc

    m, l, acc = lax.fori_loop(0, (i + 1) // 2, kvpair, carry)
    out = acc / l
    for g in range(npair):
        og = out[:, g * width:(g + 1) * width]
        o_ref[:, g * LANES:(g + 1) * LANES] = jnp.where(rowd < HEAD_DIM, og[:, :bs], og[:, bs:]).T


def _moba_prompt(q, k, v, *, bsz, t):
    nblk = t // MOBA_BLOCK
    assert t % MOBA_BLOCK == 0 and nblk % 2 == 0 and nblk <= LANES
    nblk_rows = -(-nblk // SUBLANES) * SUBLANES
    n = bsz * t
    gw = MOBA_PAIRS * LANES
    q_spec = pl.BlockSpec((MOBA_BLOCK, gw), lambda b, hp, i: (b * nblk + i, hp))
    kv_spec = pl.BlockSpec((t, gw), lambda b, hp, i: (b, hp), pipeline_mode=pl.Buffered(1))
    return pl.pallas_call(
        functools.partial(_moba_prompt_kernel, nblk=nblk),
        grid=(bsz, ATT_WIDTH // gw, nblk),
        in_specs=[q_spec, kv_spec, kv_spec], out_specs=q_spec,
        out_shape=jax.ShapeDtypeStruct((n, ATT_WIDTH), F32),
        scratch_shapes=[pltpu.VMEM((MOBA_PAIRS, t, LANES), BF16),
                        pltpu.VMEM((MOBA_PAIRS, nblk, LANES, MOBA_BLOCK), BF16),
                        pltpu.VMEM((MOBA_PAIRS, nblk_rows, LANES), F32),
                        pltpu.VMEM((nblk, 1, MOBA_PAIRS * 2 * MOBA_BLOCK), F32)],
        name="moba_prompt")(q, k, v)


def _expand_heads(q, qexp_s):
    t = q.shape[0]
    lane = lax.broadcasted_iota(jnp.int32, (1, ATT_WIDTH), 1)
    for h in range(HEADS):
        lm = (lane >= h * HEAD_DIM) & (lane < (h + 1) * HEAD_DIM)
        qexp_s[h * t:(h + 1) * t, :] = jnp.where(lm, q, 0.0)


def _collapse_heads(acc, t):
    lane = lax.broadcasted_iota(jnp.int32, (1, ATT_WIDTH), 1)
    out = jnp.zeros((t, ATT_WIDTH), F32)
    for h in range(HEADS):
        lm = (lane >= h * HEAD_DIM) & (lane < (h + 1) * HEAD_DIM)
        out = out + jnp.where(lm, acc[h * t:(h + 1) * t, :], 0.0)
    return out


def _pad_rows(x_ref, new_s):
    new_s[...] = jnp.zeros(new_s.shape, F32)
    new_s[0:x_ref.shape[0], :] = x_ref[...]
    return new_s[...].astype(BF16)


def _moba_sample_kernel(pt_ref, q_ref, kn_ref, vn_ref, *rest, n_pages, t):
    pps = PAGES_PER_STEP
    k_refs = rest[:pps]
    v_refs = rest[pps:2 * pps]
    o_ref = rest[2 * pps]
    qexp_s, new_s, s_all, gate_s, bmax_s, sel_s, m_s, l_s, acc_s = rest[2 * pps + 1:]
    ns = n_pages // pps
    s_id = pl.program_id(1)
    rows = HEADS * t
    col = lax.broadcasted_iota(jnp.int32, (rows, LANES), 1)
    scale = HEAD_DIM ** -0.5
    pages_per_block = MOBA_BLOCK // PAGE

    @pl.when(s_id == 0)
    def _():
        _expand_heads(q_ref[...], qexp_s)
        gate_s[...] = jnp.zeros(gate_s.shape, F32)
        bmax_s[...] = jnp.full(bmax_s.shape, NEG, F32)

    @pl.when(s_id < ns)
    def _():
        qe = qexp_s[...].astype(BF16)
        g = gate_s[...]
        bm = bmax_s[...]
        for i in range(pps):
            pg = s_id * pps + i
            s = _dot(qe, k_refs[i][...].astype(BF16))
            s_all[pg] = s
            mine = col == pg // pages_per_block
            g = g + jnp.where(mine, jnp.sum(s, axis=-1, keepdims=True) * (1.0 / MOBA_BLOCK), 0.0)
            bm = jnp.where(mine, jnp.maximum(bm, jnp.max(s, axis=-1, keepdims=True)), bm)
        gate_s[...] = g
        bmax_s[...] = bm

    @pl.when(s_id == ns)
    def _():
        sel = _top_blocks(gate_s[...], n_pages // pages_per_block, col)
        sel_s[...] = sel
        qe = qexp_s[...].astype(BF16)
        sn = _dot_nt(qe, _pad_rows(kn_ref, new_s)) * scale
        rowt = lax.broadcasted_iota(jnp.int32, (rows, LANES), 0) % t
        sn = jnp.where(col <= rowt, sn, NEG)
        m = jnp.maximum(jnp.max(sn, axis=-1, keepdims=True),
                        jnp.max(jnp.where(sel > 0.5, bmax_s[...] * scale, NEG), axis=-1, keepdims=True))
        pn = jnp.exp(sn - m)
        m_s[...] = jnp.broadcast_to(m, m_s.shape)
        l_s[...] = jnp.broadcast_to(jnp.sum(pn, axis=-1, keepdims=True), l_s.shape)
        acc_s[...] = _dot(pn.astype(BF16), _pad_rows(vn_ref, new_s))

    @pl.when(s_id >= ns)
    def _():
        sel = sel_s[...]
        m = m_s[:, 0:1]
        l = l_s[:, 0:1]
        acc = acc_s[...]
        for i in range(pps):
            pg = (s_id - ns) * pps + i
            selc = jnp.sum(jnp.where(col == pg // pages_per_block, sel, 0.0), axis=-1, keepdims=True)
            p = jnp.exp(jnp.where(selc > 0.5, s_all[pg] * scale, NEG) - m)
            l = l + jnp.sum(p, axis=-1, keepdims=True)
            acc = acc + _dot_nt(p.astype(BF16), v_refs[i][...].astype(BF16))
        l_s[...] = jnp.broadcast_to(l, l_s.shape)
        acc_s[...] = acc

    @pl.when(s_id == 2 * ns - 1)
    def _():
        o_ref[...] = _collapse_heads(acc_s[...] / l_s[:, 0:1], t)


def _sample_specs(page_table, n_pages, t, page_index, pps=PAGES_PER_STEP):
    row_spec = pl.BlockSpec((t, ATT_WIDTH), lambda b, s, pt: (b, 0))

    def page_spec(i, which):
        return pl.BlockSpec((None, ATT_WIDTH, PAGE),
                            lambda b, s, pt: (pt[b, page_index(s, i, which)], 0, 0))

    in_specs = ([row_spec] * 3 + [page_spec(i, 0) for i in range(pps)]
                + [page_spec(i, 1) for i in range(pps)])
    return in_specs, row_spec


def _moba_sample(q, k_new, v_new, k_pool, v_pool, page_table, *, bsz, t):
    n_pages = page_table.shape[1]
    pps = PAGES_PER_STEP
    assert n_pages % pps == 0 and (n_pages * PAGE) % MOBA_BLOCK == 0 and t <= SUBLANES
    assert n_pages * PAGE // MOBA_BLOCK <= LANES
    ns = n_pages // pps
    rows = HEADS * t

    def page_index(s, i, which):
        step = jnp.minimum(s, ns - 1) if which == 0 else jnp.maximum(s - ns, 0)
        return step * pps + i

    in_specs, out_spec = _sample_specs(page_table, n_pages, t, page_index)
    grid_spec = pltpu.PrefetchScalarGridSpec(
        num_scalar_prefetch=1, grid=(bsz, 2 * ns), in_specs=in_specs, out_specs=out_spec,
        scratch_shapes=[pltpu.VMEM((rows, ATT_WIDTH), F32),
                        pltpu.VMEM((PAGE, ATT_WIDTH), F32),
                        pltpu.VMEM((n_pages, rows, PAGE), F32)]
                       + [pltpu.VMEM((rows, LANES), F32)] * 5
                       + [pltpu.VMEM((rows, ATT_WIDTH), F32)])
    return pl.pallas_call(
        functools.partial(_moba_sample_kernel, n_pages=n_pages, t=t),
        grid_spec=grid_spec,
        out_shape=jax.ShapeDtypeStruct((bsz * t, ATT_WIDTH), F32),
        name="moba_sample")(page_table, q, k_new, v_new, *([k_pool] * pps), *([v_pool] * pps))


def _sb_weights(z, r, tri, strict):
    n = z.shape[0]
    lg = jnp.log2(1.0 + jnp.exp2(-jnp.abs(z)))
    log_beta = jnp.minimum(z, 0.0) - lg
    log_keep = jnp.minimum(-z, 0.0) - lg
    if strict is not None:
        log_beta = jnp.where(strict, log_beta, NEG)
        log_keep = jnp.where(strict, log_keep, 0.0)
    st = _dot(jnp.concatenate(_split3(log_keep), axis=0), tri)
    later = st[:n] + st[n:2 * n] + st[2 * n:]
    w = jnp.exp2(log_beta + later + r)
    return w.astype(BF16), r + jnp.sum(log_keep, axis=-1, keepdims=True)


def _suffix_matrix(n):
    return (lax.broadcasted_iota(jnp.int32, (n, n), 0)
            > lax.broadcasted_iota(jnp.int32, (n, n), 1)).astype(BF16)


def _sb_prompt_kernel(q_ref, k_ref, v_ref, o_ref, kb_s, vb_s):
    bs = SB_BLOCK
    i = pl.program_id(2)

    @pl.when(i == 0)
    def _():
        for j in range(k_ref.shape[0] // bs):
            kb_s[j * bs:(j + 1) * bs, :] = k_ref[j * bs:(j + 1) * bs, :].astype(BF16)
            vb_s[j * bs:(j + 1) * bs, :] = v_ref[j * bs:(j + 1) * bs, :].astype(BF16)

    lane = lax.broadcasted_iota(jnp.int32, (1, LANES), 1)
    tri = _suffix_matrix(bs)
    strict = (lax.broadcasted_iota(jnp.int32, (2 * bs, bs), 1)
              < lax.broadcasted_iota(jnp.int32, (2 * bs, bs), 0) % bs)
    qf = q_ref[...]
    qhs = []
    for hd in range(2):
        lm = (lane >= HEAD_DIM * hd) & (lane < HEAD_DIM * (hd + 1))
        qhs.append(jnp.where(lm, qf, 0.0))
    qs2 = (jnp.concatenate(qhs, axis=0) * (HEAD_DIM ** -0.5 * LOG2E)).astype(BF16)

    def block(j, r, mask):
        c0 = pl.multiple_of(j * bs, bs)
        w, r = _sb_weights(_dot_nt(qs2, kb_s[pl.ds(c0, bs), :]), r, tri, mask)
        return _dot(w, vb_s[pl.ds(c0, bs), :]), r

    acc, r = block(i, jnp.zeros((2 * bs, 1), F32), strict)
    has_prev = lax.broadcasted_iota(jnp.int32, (2 * bs, bs), 0) < jnp.where(i > 0, 2 * bs, 0)
    d, r = block(jnp.maximum(i - 1, 0), r, has_prev)
    acc = acc + d

    def cond(c):
        j, r, _ = c
        return (j >= 0) & (jnp.max(r) > EXP2_UNDERFLOW)

    def body(c):
        j, r, acc = c
        d, r = block(j, r, None)
        return j - 1, r, acc + d

    _, _, acc = lax.while_loop(cond, body, (i - 2, r, acc))
    o_ref[...] = jnp.where(lane < HEAD_DIM, acc[:bs], acc[bs:])


def _sb_prompt(q, k, v, *, bsz, t):
    assert t % SB_BLOCK == 0
    n = bsz * t
    nq = t // SB_BLOCK
    q_spec = pl.BlockSpec((SB_BLOCK, LANES), lambda b, hp, i: (b * nq + i, hp))
    kv_spec = pl.BlockSpec((t, LANES), lambda b, hp, i: (b, hp))
    return pl.pallas_call(
        _sb_prompt_kernel,
        grid=(bsz, ATT_WIDTH // LANES, nq),
        in_specs=[q_spec, kv_spec, kv_spec], out_specs=q_spec,
        out_shape=jax.ShapeDtypeStruct((n, ATT_WIDTH), F32),
        scratch_shapes=[pltpu.VMEM((t, LANES), BF16), pltpu.VMEM((t, LANES), BF16)],
        name="sb_prompt")(q, k, v)


def _sb_sample_kernel(pt_ref, q_ref, a_ref, b_ref, *rest, t, first, pps):
    k_refs = rest[:pps]
    v_refs = rest[pps:2 * pps]
    n_out = 3 if first else 1
    o_ref = rest[2 * pps]
    qexp_s, new_s, r_s, acc_s = rest[2 * pps + n_out:]
    s_id = pl.program_id(1)
    rows = HEADS * t
    tri = _suffix_matrix(PAGE)

    @pl.when(s_id == 0)
    def _():
        _expand_heads(q_ref[...] * (HEAD_DIM ** -0.5 * LOG2E), qexp_s)
        if first:
            z = _dot_nt(qexp_s[...].astype(BF16), _pad_rows(a_ref, new_s))
            col = lax.broadcasted_iota(jnp.int32, (rows, PAGE), 1)
            rowt = lax.broadcasted_iota(jnp.int32, (rows, PAGE), 0) % t
            w, r = _sb_weights(z, jnp.zeros((rows, 1), F32), tri, col < rowt)
            acc_s[...] = _dot(w, _pad_rows(b_ref, new_s))
            r_s[...] = jnp.broadcast_to(r, r_s.shape)
        else:
            acc_s[...] = a_ref[...]
            r_s[...] = b_ref[...]

    for i in range(pps):
        @pl.when(jnp.max(r_s[...]) > EXP2_UNDERFLOW)
        def _(i=i):
            z = _dot(qexp_s[...].astype(BF16), k_refs[i][...].astype(BF16))
            w, r = _sb_weights(z, r_s[:, 0:1], tri, None)
            acc_s[...] = acc_s[...] + _dot_nt(w, v_refs[i][...].astype(BF16))
            r_s[...] = jnp.broadcast_to(r, r_s.shape)

    @pl.when(s_id == pl.num_programs(1) - 1)
    def _():
        o_ref[...] = _collapse_heads(acc_s[...], t)
        if first:
            rest[2 * pps + 1][...] = acc_s[...]
            rest[2 * pps + 2][...] = r_s[...]


def _sb_sample(q, k_new, v_new, k_pool, v_pool, page_table, *, bsz, t):
    n_pages = page_table.shape[1]
    assert t <= SUBLANES
    first = min(SB_FIRST_PAGES, n_pages)
    rows = HEADS * t
    n = bsz * t
    scratch = [pltpu.VMEM((rows, ATT_WIDTH), F32), pltpu.VMEM((PAGE, ATT_WIDTH), F32),
               pltpu.VMEM((rows, LANES), F32), pltpu.VMEM((rows, ATT_WIDTH), F32)]
    acc_spec = pl.BlockSpec((rows, ATT_WIDTH), lambda b, s, pt: (b, 0))
    r_spec = pl.BlockSpec((rows, LANES), lambda b, s, pt: (b, 0))
    y_sds = jax.ShapeDtypeStruct((n, ATT_WIDTH), F32)

    in_specs, y_spec = _sample_specs(page_table, n_pages, t, lambda s, i, which: n_pages - 1 - i, first)
    y, acc, r = pl.pallas_call(
        functools.partial(_sb_sample_kernel, t=t, first=True, pps=first),
        grid_spec=pltpu.PrefetchScalarGridSpec(
            num_scalar_prefetch=1, grid=(bsz, 1), in_specs=in_specs,
            out_specs=[y_spec, acc_spec, r_spec], scratch_shapes=scratch),
        out_shape=[y_sds, jax.ShapeDtypeStruct((bsz * rows, ATT_WIDTH), F32),
                   jax.ShapeDtypeStruct((bsz * rows, LANES), F32)],
        name="sb_sample")(page_table, q, k_new, v_new, *([k_pool] * first), *([v_pool] * first))
    older = n_pages - first
    if older == 0:
        return y
    pps = max(d for d in range(1, PAGES_PER_STEP + 1) if older % d == 0)

    def older_pages():
        specs, _ = _sample_specs(page_table, n_pages, t,
                                 lambda s, i, which: n_pages - 1 - first - (s * pps + i), pps)
        specs = [specs[0], acc_spec, r_spec] + specs[3:]
        return pl.pallas_call(
            functools.partial(_sb_sample_kernel, t=t, first=False, pps=pps),
            grid_spec=pltpu.PrefetchScalarGridSpec(
                num_scalar_prefetch=1, grid=(bsz, older // pps), in_specs=specs,
                out_specs=y_spec, scratch_shapes=scratch),
            out_shape=y_sds,
            name="sb_sample_older")(page_table, q, acc, r, *([k_pool] * pps), *([v_pool] * pps))

    return lax.cond(jnp.max(r) > EXP2_UNDERFLOW, older_pages, lambda: y)


def _hgrn_kernel(q_ref, f_ref, i_ref, g_ref, lb_ref, s0_ref, gn_ref, o_ref, s_out_ref, st_s,
                 *, layer):
    tb = q_ref.shape[0]
    t = pl.program_id(1)

    @pl.when(t == 0)
    def _():
        for hh in range(HGRN_HEADS):
            st_s[hh] = s0_ref[0, hh].T

    lbp = lb_ref[...]
    e = jnp.exp(lbp - jnp.max(lbp, axis=0, keepdims=True))
    soft = e / jnp.sum(e, axis=0, keepdims=True)
    lbv = jnp.sum(soft[1:layer + 1], axis=0, keepdims=True)

    pad = max(HGRN_SUB - tb, 0)
    L = min(HGRN_CHUNK, tb + pad)
    c = min(HGRN_SUB, L)
    tri = (lax.broadcasted_iota(jnp.int32, (L, L), 0)
           >= lax.broadcasted_iota(jnp.int32, (L, L), 1)).astype(BF16)
    rowc = lax.broadcasted_iota(jnp.int32, (c, 1), 0)
    gn = gn_ref[...]

    def padrows(x):
        if pad == 0:
            return x
        return jnp.concatenate([x, jnp.zeros((pad, x.shape[1]), F32)], axis=0)

    def chunk(rows, n_valid):
        for hh in range(HGRN_HEADS):
            ls = slice(hh * HGRN_DIM, (hh + 1) * HGRN_DIM)
            lbh = lbv[:, ls]
            f = lbh + (1.0 - lbh) * _sigmoid(f_ref[rows, ls])
            lf = padrows(jnp.log(f))
            kk = padrows(1.0 - f)
            qv = padrows(q_ref[rows, ls])
            vv = padrows(i_ref[rows, ls])
            hi, mid, lo = _split3(lf)
            b = _dot(tri, hi) + _dot(tri, mid) + _dot(tri, lo)
            st = st_s[hh]
            o_inter = _dot_nt((qv * jnp.exp(b)).astype(BF16), st.astype(BF16))
            vb = vv.astype(BF16)
            parts = []
            for si in range(L // c):
                rs = slice(si * c, (si + 1) * c)
                b_i = b[rs]
                q_i = qv[rs]
                k_i = kk[rs]
                v_i = vv[rs]
                o_i = o_inter[rs]
                if si > 0:
                    b_prev = b[si * c - 1:si * c]
                    qt = (q_i * jnp.exp(b_i - b_prev)).astype(BF16)
                    kt = (kk[:si * c] * jnp.exp(b_prev - b[:si * c])).astype(BF16)
                    o_i = o_i + _dot(_dot_nt(qt, kt).astype(BF16), vb[:si * c])
                for s in range(min(c, max(n_valid - si * c, 0))):
                    dec = jnp.exp(jnp.minimum(b_i - b_i[s:s + 1], 0.0))
                    a = jnp.sum(q_i * k_i[s:s + 1] * dec, axis=-1, keepdims=True)
                    o_i = o_i + jnp.where(rowc >= s, a, 0.0) * v_i[s:s + 1]
                parts.append(o_i)
            o = jnp.concatenate(parts, axis=0) if len(parts) > 1 else parts[0]
            o = o[:n_valid]
            o = o * lax.rsqrt(jnp.mean(o * o, axis=-1, keepdims=True) + EPS) * gn
            gv = g_ref[rows, ls]
            o_ref[rows, ls] = o * (gv * _sigmoid(gv))
            b_last = b[L - 1:L]
            kdec = (kk * jnp.exp(b_last - b)).astype(BF16)
            st_s[hh] = st * jnp.exp(b_last) + _dot_tn(vb, kdec)

    if pad:
        chunk(slice(0, tb), tb)
    else:
        def body(ch, _):
            chunk(pl.ds(pl.multiple_of(ch * L, L), L), L)
            return 0

        lax.fori_loop(0, tb // L, body, 0, unroll=True)

    for hh in range(HGRN_HEADS):
        s_out_ref[0, hh] = st_s[hh].T


def _hgrn(qh, fh, ih, gh, lb, s0, gnorm, *, bsz, t, layer):
    n = bsz * t
    tb = min(ROW_TILE, t)
    nt = t // tb
    row_spec = pl.BlockSpec((tb, SLAB), lambda b, i: (b * nt + i, 0))
    st_spec = pl.BlockSpec((1, HGRN_HEADS, HGRN_DIM, HGRN_DIM), lambda b, i: (b, 0, 0, 0))
    return pl.pallas_call(
        functools.partial(_hgrn_kernel, layer=layer),
        grid=(bsz, nt),
        in_specs=[row_spec] * 4 + [_const(lb.shape), st_spec, _const((1, HGRN_DIM))],
        out_specs=[row_spec, st_spec],
        out_shape=[jax.ShapeDtypeStruct((n, SLAB), F32),
                   jax.ShapeDtypeStruct((bsz, HGRN_HEADS, HGRN_DIM, HGRN_DIM), F32)],
        scratch_shapes=[pltpu.VMEM((HGRN_HEADS, HGRN_DIM, HGRN_DIM), F32)],
        name="hgrn2")(qh, fh, ih, gh, lb, s0, gnorm.reshape(1, HGRN_DIM))


def _tail_kernel(*refs, chain):
    if chain:
        (h_ref, ya_ref, yb_ref, p_ref, wo_ref, gmp_ref, gfp_ref, wup_ref, wcv_ref, wdn_ref,
         gfo_ref, wpp_ref, wpg_ref, gpl_ref, ho_ref, ffn_ref, u_s, h1_s, a_s, act_s, car_s) = refs
    else:
        (h_ref, ya_ref, yb_ref, p_ref, p1_ref, p2_ref, wo_ref, gmp_ref, gfp_ref, wup_ref,
         wcv_ref, wdn_ref, gfo_ref, wpp_ref, wpg_ref, gpl_ref, ho_ref, ffn_ref, u_s, h1_s, a_s, act_s) = refs
    tm = h_ref.shape[0]
    nch = N_FF_CHUNKS
    cw = FF_CHUNK
    mix = _dot(ya_ref[...].astype(BF16), wo_ref[0]) + _dot(yb_ref[...].astype(BF16), wo_ref[1])
    h1 = h_ref[...] + _rms(mix, gmp_ref[...])
    h1_s[...] = h1
    a_s[...] = _rms(h1, gfp_ref[...]).astype(BF16)
    rowi = lax.broadcasted_iota(jnp.int32, (tm, cw), 0)

    if chain:
        @pl.when(pl.program_id(1) == 0)
        def _():
            car_s[...] = jnp.zeros(car_s.shape, F32)

    def taps(w, p2, p1, u):
        return w[3:4] + w[0:1] * p2 + w[1:2] * p1 + w[2:3] * u

    def up(c):
        a = a_s[...]
        for half in range(2):
            cols = slice(half * D_FF + c * cw, half * D_FF + (c + 1) * cw)
            u_s[c % 2, half, SUBLANES:, :] = _dot(a, wup_ref[:, cols])

    def down(c):
        slot = c % 2
        ws = [wcv_ref[:, half * D_FF + c * cw:half * D_FF + (c + 1) * cw] for half in range(2)]
        if chain:
            for half in range(2):
                u_s[slot, half, 0:SUBLANES, :] = car_s[half * nch + c]
            for rb in range(tm // CONV_ROWS):
                cs = []
                for half in range(2):
                    x = u_s[slot, half, rb * CONV_ROWS:(rb + 1) * CONV_ROWS + SUBLANES, :]
                    cs.append(taps(ws[half], pltpu.roll(x, 2, 0)[SUBLANES:],
                                   pltpu.roll(x, 1, 0)[SUBLANES:], x[SUBLANES:]))
                act_s[rb * CONV_ROWS:(rb + 1) * CONV_ROWS, c * cw:(c + 1) * cw] = (
                    _gelu(cs[0]) * cs[1]).astype(BF16)
            for half in range(2):
                last = u_s[slot, half, tm:tm + SUBLANES, :]
                car_s[half * nch + c] = last
                ffn_ref[0, half * nch + c] = last
        else:
            rm = rowi % SUBLANES
            cs = []
            for half in range(2):
                u = u_s[slot, half, SUBLANES:, :]
                p1 = jnp.where(rm == 0, p1_ref[half * nch + c], pltpu.roll(u, 1, 0))
                p2 = jnp.where(rm < 2, p2_ref[half * nch + c], pltpu.roll(u, 2, 0))
                cs.append(taps(ws[half], p2, p1, u))
                ffn_ref[half * nch + c] = u
            act_s[:, c * cw:(c + 1) * cw] = (_gelu(cs[0]) * cs[1]).astype(BF16)

    up(0)
    for c in range(nch):
        if c + 1 < nch:
            up(c + 1)
        down(c)
    h2 = h1_s[...] + _rms(_dot(act_s[...], wdn_ref[...]), gfo_ref[...])
    ple = _dot(p_ref[...].astype(BF16), wpp_ref[...]) * _sigmoid(_dot(h2.astype(BF16), wpg_ref[...]))
    ho_ref[...] = h2 + _rms(ple, gpl_ref[...])


def _tail(h, ya, yb, p, ffn0, w, *, bsz, t):
    n, d = h.shape
    nch = N_FF_CHUNKS
    cw = FF_CHUNK
    chain = t > SUBLANES
    consts = [w['wo'], w['gmp'], w['gfp'], w['wup'], w['wcv'], w['wdn'], w['gfo'], w['wpp'], w['wpg'],
              w['gpl']]
    const_specs = [_const(x.shape) for x in consts]
    if chain:
        assert ffn0 is None, "a long sequence starts from an empty ConvFFN buffer"
        tm = min(MATMUL_TILE, t)
        nt = t // tm
        grid = (bsz, nt)
        rows = lambda width: pl.BlockSpec((tm, width), lambda b, i: (b * nt + i, 0))
        ffn_spec = pl.BlockSpec((1, 2 * nch, SUBLANES, cw), lambda b, i: (b, 0, 0, 0))
        ffn_sds = jax.ShapeDtypeStruct((bsz, 2 * nch, SUBLANES, cw), F32)
        extra, extra_specs = [], []
        scratch = [pltpu.VMEM((2, 2, tm + SUBLANES, cw), F32),
                   pltpu.VMEM((tm, d), F32), pltpu.VMEM((tm, d), BF16), pltpu.VMEM((tm, D_FF), BF16),
                   pltpu.VMEM((2 * nch, SUBLANES, cw), F32)]
    else:
        assert t == SUBLANES
        tm = n
        grid = (1, 1)
        rows = lambda width: pl.BlockSpec((tm, width), lambda b, i: (0, 0))
        ffn_spec = _full((2 * nch, tm, cw))
        ffn_sds = jax.ShapeDtypeStruct((2 * nch, tm, cw), F32)
        buf = ffn0.astype(F32).reshape(bsz, 2, 2 * nch, cw).transpose(2, 0, 1, 3)
        zero = jnp.zeros((2 * nch, bsz, SUBLANES - 2, cw), F32)
        p1 = jnp.concatenate([buf[:, :, 1:2], zero, zero[:, :, :1]], axis=2).reshape(2 * nch, tm, cw)
        p2 = jnp.concatenate([buf, zero], axis=2).reshape(2 * nch, tm, cw)
        extra = [p1, p2]
        extra_specs = [_full(p1.shape), _full(p2.shape)]
        scratch = [pltpu.VMEM((2, 2, tm + SUBLANES, cw), F32),
                   pltpu.VMEM((tm, d), F32), pltpu.VMEM((tm, d), BF16), pltpu.VMEM((tm, D_FF), BF16)]
    ho, ffn = pl.pallas_call(
        functools.partial(_tail_kernel, chain=chain),
        grid=grid,
        in_specs=[rows(d), rows(SLAB), rows(SLAB), rows(p.shape[1])] + extra_specs + const_specs,
        out_specs=[rows(d), ffn_spec],
        out_shape=[jax.ShapeDtypeStruct((n, d), F32), ffn_sds],
        scratch_shapes=scratch,
        name="layer_tail")(h, ya, yb, p, *extra, *consts)
    if chain:
        st = ffn[:, :, SUBLANES - 2:, :]
    else:
        st = ffn.reshape(2 * nch, bsz, SUBLANES, cw)[:, :, SUBLANES - 2:, :].transpose(1, 0, 2, 3)
    return ho, st.transpose(0, 2, 1, 3).reshape(bsz, 2, 2 * nch * cw)


def _tail_weights(i, w_out, g_mix_post, g_ffn_pre, g_ffn_post, w_ffn_up, w_ffn_conv, b_ffn_conv,
                  w_ffn_down, w_ple_proj, w_ple_gate, g_ple):
    d = w_out.shape[1]
    taps = jnp.concatenate([w_ffn_conv[i].astype(F32), b_ffn_conv[i].astype(F32)[None]], axis=0)
    row = lambda g: g.astype(F32).reshape(1, d)
    return dict(
        wo=w_out.astype(BF16).reshape(2, SLAB, d), gmp=row(g_mix_post[i]), gfp=row(g_ffn_pre[i]),
        wup=w_ffn_up[i].astype(BF16), wcv=taps, wdn=w_ffn_down[i].astype(BF16),
        gfo=row(g_ffn_post[i]), wpp=w_ple_proj[i].astype(BF16), wpg=w_ple_gate[i].astype(BF16),
        gpl=row(g_ple[i]))


def kernel(x_prompt, x_sample, p_prompt, p_sample, cache_moba_k, cache_moba_v, state_s5_re, state_s5_im,
           state_hgrn, cache_sb_k, cache_sb_v, state_ffn, page_table,
           g_mix_pre, g_mix_post, g_ffn_pre, g_ffn_post, w_ffn_up, w_ffn_conv, b_ffn_conv, w_ffn_down,
           w_ple_proj, w_ple_gate, g_ple,
           w_in_a, w_out_a, s5_a_re, s5_a_im, s5_log_dt, s5_b_re, s5_b_im, s5_c_re, s5_c_im, s5_d,
           s5_w_glu, s5_b_glu,
           w_in_c, w_out_c, hgrn_lb, g_hgrn_norm):
    depth = g_mix_pre.shape[0]
    d_model = x_prompt.shape[-1]
    n_pages = page_table.shape[1]
    past_len = n_pages * cache_moba_k.shape[2]
    page_table = page_table.astype(jnp.int32)

    layers = []
    for i in range(depth):
        j = i // 2
        lw = {}
        if i % 2 == 0:
            lw['w_in'] = w_in_a[j].astype(BF16)
            lw['disc'] = _s5_discretize(s5_a_re[j], s5_a_im[j], s5_log_dt[j])
            lw['wb'], lw['wc'] = _s5_weights(s5_b_re[j], s5_b_im[j], s5_c_re[j], s5_c_im[j])
            lw['d'] = s5_d[j].astype(F32).reshape(S5_WIDTH)
            lw['wglu'] = s5_w_glu[j].astype(BF16)
            lw['bglu'] = s5_b_glu[j].astype(F32)
            w_out = w_out_a[j]
        else:
            lw['w_in'] = w_in_c[j].astype(BF16)
            w_out = w_out_c[j]
        lw['tail'] = _tail_weights(i, w_out, g_mix_post, g_ffn_pre, g_ffn_post, w_ffn_up, w_ffn_conv,
                                   b_ffn_conv, w_ffn_down, w_ple_proj, w_ple_gate, g_ple)
        layers.append(lw)

    def pool2d(pool, j):
        return pool[j].transpose(0, 2, 3, 1).reshape(pool.shape[1], ATT_WIDTH, pool.shape[2])

    def run(x, p, q_start, s5_re0, s5_im0, hgrn0, ffn0, has_past):
        bsz, t, _ = x.shape
        n = bsz * t
        h = x.astype(F32).reshape(n, d_model)
        tables = _rope_tables(t, q_start)
        if t < MATMUL_TILE:
            tables = tuple(jnp.tile(tb, (n // t, 1)) for tb in tables)
        mk, mv, sr, si, hs, sk, sv, fb = [], [], [], [], [], [], [], []
        long_seq = t % MATMUL_TILE == 0

        def kv_out(x, x_t, which):
            if x_t:
                return x_t[which].reshape(bsz, HEADS, HEAD_DIM, t).transpose(0, 3, 1, 2)
            return x.reshape(bsz, t, HEADS, HEAD_DIM)

        for i in range(depth):
            j = i // 2
            lw = layers[i]
            g_pre = g_mix_pre[i].astype(F32)
            if i % 2 == 0:
                u, q, k, v, *kv_t = _inproj(h, g_pre, lw['w_in'], tables, rope_slabs=(1, 2),
                                            t_slabs=(2, 3) if long_seq else (), seq=(bsz, t))
                y_a, hr, hi = _s5(u, s5_re0[j].astype(F32).reshape(bsz, 1, S5_LANES),
                                  s5_im0[j].astype(F32).reshape(bsz, 1, S5_LANES), lw['disc'], lw['wb'],
                                  lw['wc'], lw['d'], lw['wglu'], lw['bglu'], bsz=bsz, t=t)
                if has_past:
                    y_b = _moba_sample(q, k, v, pool2d(cache_moba_k, j), pool2d(cache_moba_v, j),
                                       page_table, bsz=bsz, t=t)
                else:
                    y_b = _moba_prompt(q, k, v, bsz=bsz, t=t)
                mk.append(kv_out(k, kv_t, 0))
                mv.append(kv_out(v, kv_t, 1))
                sr.append(hr.reshape(bsz, S5_GROUPS, S5_STATE))
                si.append(hi.reshape(bsz, S5_GROUPS, S5_STATE))
            else:
                qh, fh, ih, gh, q, k, v, *kv_t = _inproj(h, g_pre, lw['w_in'], tables, rope_slabs=(),
                                                         t_slabs=(5, 6) if long_seq else (), seq=(bsz, t))
                y_a, s_fin = _hgrn(qh, fh, ih, gh, hgrn_lb.astype(F32), hgrn0[j].astype(F32),
                                   g_hgrn_norm[j].astype(F32), bsz=bsz, t=t, layer=i)
                if has_past:
                    y_b = _sb_sample(q, k, v, pool2d(cache_sb_k, j), pool2d(cache_sb_v, j), page_table,
                                     bsz=bsz, t=t)
                else:
                    y_b = _sb_prompt(q, k, v, bsz=bsz, t=t)
                hs.append(s_fin)
                sk.append(kv_out(k, kv_t, 0))
                sv.append(kv_out(v, kv_t, 1))
            h, buf = _tail(h, y_a, y_b, p[i].astype(F32).reshape(n, p.shape[-1]),
                           None if ffn0 is None else ffn0[i], lw['tail'], bsz=bsz, t=t)
            fb.append(buf)
        return (h.reshape(bsz, t, d_model), jnp.stack(mk), jnp.stack(mv), jnp.stack(sr), jnp.stack(si),
                jnp.stack(hs), jnp.stack(sk), jnp.stack(sv), jnp.stack(fb))

    bp = x_prompt.shape[0]
    n_a = (depth + 1) // 2
    n_c = depth // 2
    outs_p = run(x_prompt, p_prompt, 0,
                 jnp.zeros((n_a, bp, S5_GROUPS, S5_STATE), F32), jnp.zeros((n_a, bp, S5_GROUPS, S5_STATE), F32),
                 jnp.zeros((n_c, bp, HGRN_HEADS, HGRN_DIM, HGRN_DIM), F32), None, False)
    outs_s = run(x_sample, p_sample, past_len, state_s5_re, state_s5_im, state_hgrn, state_ffn, True)
    return (outs_p[0], outs_s[0]) + tuple(outs_p[1:]) + tuple(outs_s[1:])
```

```python
import functools
import math

import jax
import jax.numpy as jnp
from jax import lax
from jax.experimental import pallas as pl
from jax.experimental.pallas import tpu as pltpu

F32 = jnp.float32
BF16 = jnp.bfloat16

S5_WIDTH = 512
S5_GROUPS = 32
S5_GROUP = 16
S5_STATE = 64
S5_LANES = S5_GROUPS * S5_STATE
HEADS = 8
HEAD_DIM = 64
ATT_WIDTH = HEADS * HEAD_DIM
ROPE_DIM = 16
ROPE_THETA = 500000.0
MOBA_BLOCK = 256
MOBA_TOPK = 3
MOBA_PAIRS = 4
HGRN_HEADS = 4
HGRN_DIM = 128
HGRN_CHUNK = 64
HGRN_SUB = 16
D_FF = 2816
FF_CHUNK = 256
N_FF_CHUNKS = D_FF // FF_CHUNK
PAGE = 128
EPS = 1e-6
NEG = -1e30
SLAB = 512
LANES = 128
SUBLANES = 8
MATMUL_TILE = 512
CONV_ROWS = 128
SB_BLOCK = 256
PAGES_PER_STEP = 16
SB_FIRST_PAGES = 4
LOG2E = 1.4426950408889634
EXP2_UNDERFLOW = -150.0


def _dot(a, b):
    return jnp.dot(a, b, preferred_element_type=F32)


def _dot_nt(a, b):
    return lax.dot_general(a, b, (((1,), (1,)), ((), ())), preferred_element_type=F32)


def _dot_tn(a, b):
    return lax.dot_general(a, b, (((0,), (0,)), ((), ())), preferred_element_type=F32)


def _split3(x):
    hi = x.astype(BF16)
    r1 = x - hi.astype(F32)
    mid = r1.astype(BF16)
    lo = (r1 - mid.astype(F32)).astype(BF16)
    return hi, mid, lo


def _rms(x, g):
    return x * lax.rsqrt(jnp.mean(x * x, axis=-1, keepdims=True) + EPS) * g


def _sigmoid(x):
    return 1.0 / (1.0 + jnp.exp(-x))


def _gelu(x):
    return 0.5 * x * (1.0 + jnp.tanh(0.7978845608028654 * (x + 0.044715 * (x * x * x))))


def _full(shape):
    nd = len(shape)
    return pl.BlockSpec(shape, lambda *_: (0,) * nd)


def _const(shape):
    nd = len(shape)
    return pl.BlockSpec(shape, lambda *_: (0,) * nd, pipeline_mode=pl.Buffered(1))


def _rope_table_kernel(c_ref, s1_ref, s2_ref, *, q_start):
    rows, lanes = c_ref.shape
    i = pl.program_id(0)
    lane = lax.broadcasted_iota(jnp.int32, (rows, lanes), 1)
    pos = lax.broadcasted_iota(jnp.int32, (rows, lanes), 0) + i * rows + q_start
    d = lane % HEAD_DIM
    half = ROPE_DIM // 2
    inv = jnp.exp((d % half).astype(F32) * (-math.log(ROPE_THETA) / half))
    ang = pos.astype(F32) * inv
    cs = jnp.cos(ang)
    sn = jnp.sin(ang)
    c_ref[...] = jnp.where(d < ROPE_DIM, cs, 1.0)
    s1_ref[...] = jnp.where(d < half, -sn, 0.0)
    s2_ref[...] = jnp.where((d >= half) & (d < ROPE_DIM), sn, 0.0)


def _rope_tables(t, q_start):
    rows = min(t, 512)
    sds = jax.ShapeDtypeStruct((t, LANES), F32)
    spec = pl.BlockSpec((rows, LANES), lambda i: (i, 0))
    return pl.pallas_call(
        functools.partial(_rope_table_kernel, q_start=q_start),
        grid=(t // rows,), out_shape=(sds, sds, sds), out_specs=(spec, spec, spec),
        name="rope_tables")()


def _inproj_kernel(h_ref, g_ref, w_ref, c_ref, s1_ref, s2_ref, *out_refs, n_slabs, rope_slabs,
                   t_slabs):
    a = _rms(h_ref[...], g_ref[...]).astype(BF16)
    half = ROPE_DIM // 2
    for s in range(n_slabs):
        o_ref = out_refs[s]
        z = _dot(a, w_ref[:, s * SLAB:(s + 1) * SLAB])
        if s in rope_slabs:
            c = c_ref[...]
            s1 = s1_ref[...]
            s2 = s2_ref[...]
            for j in range(SLAB // LANES):
                x = z[:, j * LANES:(j + 1) * LANES]
                o_ref[:, j * LANES:(j + 1) * LANES] = (
                    x * c + pltpu.roll(x, LANES - half, 1) * s1 + pltpu.roll(x, half, 1) * s2)
        else:
            o_ref[...] = z
        if s in t_slabs:
            out_refs[n_slabs + t_slabs.index(s)][...] = o_ref[...].T


def _inproj(h, g, w, tables, rope_slabs, t_slabs=(), seq=None):
    n, d = h.shape
    n_slabs = w.shape[1] // SLAB
    tm = min(MATMUL_TILE, n)
    t_tiles = tables[0].shape[0] // tm
    tab_spec = pl.BlockSpec((tm, LANES), lambda i: (i % t_tiles, 0))
    out_spec = pl.BlockSpec((tm, SLAB), lambda i: (i, 0))
    out_specs = [out_spec] * n_slabs
    out_shape = [jax.ShapeDtypeStruct((n, SLAB), F32)] * n_slabs
    if t_slabs:
        bsz, t = seq
        assert t % tm == 0
        nt = t // tm
        out_specs += [pl.BlockSpec((None, SLAB, tm), lambda i: (i // nt, 0, i % nt))] * len(t_slabs)
        out_shape += [jax.ShapeDtypeStruct((bsz, SLAB, t), F32)] * len(t_slabs)
    return pl.pallas_call(
        functools.partial(_inproj_kernel, n_slabs=n_slabs, rope_slabs=rope_slabs, t_slabs=t_slabs),
        grid=(n // tm,),
        in_specs=[pl.BlockSpec((tm, d), lambda i: (i, 0)), _const((1, d)), _const(w.shape),
                  tab_spec, tab_spec, tab_spec],
        out_specs=out_specs, out_shape=out_shape,
        name="inproj")(h, g.reshape(1, d), w, *tables)


def _s5_disc_kernel(ar_ref, ai_ref, ldt_ref, pre_ref, pim_ref, cc_ref):
    ar = ar_ref[...]
    ai = ai_ref[...]
    dt = jnp.exp(ldt_ref[...])
    row = lax.broadcasted_iota(jnp.int32, (4 * SUBLANES, 1), 0)
    blk = row // SUBLANES
    r = row % SUBLANES
    n = jnp.where(blk == 0, r + 1, jnp.where(blk == 1, 1, jnp.where(blk == 2, 2, 4)))
    keep = (blk == 0) | (r >= n)
    n = n.astype(F32)
    mag = jnp.exp(ar * dt * n)
    ang = ai * dt * n
    pre = mag * jnp.cos(ang)
    pim = mag * jnp.sin(ang)
    pre_ref[...] = jnp.where(keep, pre, 0.0)
    pim_ref[...] = jnp.where(keep, pim, 0.0)
    abr = pre[0:1]
    abi = pim[0:1]
    den = ar * ar + ai * ai
    nr = abr - 1.0
    cc_ref[0:1, :] = (nr * ar + abi * ai) / den
    cc_ref[1:2, :] = (abi * ar - nr * ai) / den


def _s5_discretize(a_re, a_im, log_dt):
    ar = a_re.reshape(1, S5_LANES)
    ai = a_im.reshape(1, S5_LANES)
    ldt = jnp.repeat(log_dt, S5_STATE).reshape(1, S5_LANES)
    return pl.pallas_call(
        _s5_disc_kernel,
        out_shape=(jax.ShapeDtypeStruct((4 * SUBLANES, S5_LANES), F32),
                   jax.ShapeDtypeStruct((4 * SUBLANES, S5_LANES), F32),
                   jax.ShapeDtypeStruct((2, S5_LANES), F32)),
        name="s5_discretize")(ar, ai, ldt)


SCAN_LANES = 512


def _s5_kernel(u_ref, h0r_ref, h0i_ref, pre_ref, pim_ref, cc_ref, wb_ref, wc_ref, d_ref,
               wglu_ref, bglu_ref, y_ref, hr_out_ref, hi_out_ref, xr_s, xi_s, car_s, *, chain):
    tm = u_ref.shape[0]
    nb = tm // SUBLANES
    u = u_ref[...]
    ub = u.astype(BF16)
    half_in = S5_WIDTH // 2
    half_st = S5_LANES // 2
    for hf in range(2):
        bu = _dot(ub[:, hf * half_in:(hf + 1) * half_in], wb_ref[hf])
        bur = bu[:, :half_st]
        bui = bu[:, half_st:]
        ls = slice(hf * half_st, (hf + 1) * half_st)
        cr = cc_ref[0:1, ls]
        ci = cc_ref[1:2, ls]
        xr_s[:, ls] = cr * bur - ci * bui
        xi_s[:, ls] = cr * bui + ci * bur

    if chain:
        t = pl.program_id(1)

        @pl.when(t == 0)
        def _():
            car_s[0:1, :] = h0r_ref[0]
            car_s[1:2, :] = h0i_ref[0]

    for lc in range(S5_LANES // SCAN_LANES):
        ls = slice(lc * SCAN_LANES, (lc + 1) * SCAN_LANES)

        def body(i, carry, ls=ls):
            r0 = pl.multiple_of(i * SUBLANES, SUBLANES)
            hr = xr_s[pl.ds(r0, SUBLANES), ls]
            hi = xi_s[pl.ds(r0, SUBLANES), ls]
            for step, k in enumerate((1, 2, 4)):
                ar = pre_ref[(step + 1) * SUBLANES:(step + 2) * SUBLANES, ls]
                ai = pim_ref[(step + 1) * SUBLANES:(step + 2) * SUBLANES, ls]
                sr = pltpu.roll(hr, k, 0)
                si = pltpu.roll(hi, k, 0)
                hr, hi = hr + ar * sr - ai * si, hi + ar * si + ai * sr
            if chain:
                cr_, ci_ = carry
            else:
                cr_ = h0r_ref[i, :, ls]
                ci_ = h0i_ref[i, :, ls]
            pr8 = pre_ref[0:SUBLANES, ls]
            pi8 = pim_ref[0:SUBLANES, ls]
            hr, hi = hr + pr8 * cr_ - pi8 * ci_, hi + pr8 * ci_ + pi8 * cr_
            xr_s[pl.ds(r0, SUBLANES), ls] = hr
            xi_s[pl.ds(r0, SUBLANES), ls] = hi
            last_r = hr[SUBLANES - 1:SUBLANES]
            last_i = hi[SUBLANES - 1:SUBLANES]
            if chain:
                return last_r, last_i
            hr_out_ref[i, :, ls] = last_r
            hi_out_ref[i, :, ls] = last_i
            return carry

        if chain:
            cr_, ci_ = lax.fori_loop(0, nb, body, (car_s[0:1, ls], car_s[1:2, ls]))
            car_s[0:1, ls] = cr_
            car_s[1:2, ls] = ci_
        else:
            lax.fori_loop(0, nb, body, 0)

    if chain:
        hr_out_ref[0] = car_s[0:1, :]
        hi_out_ref[0] = car_s[1:2, :]

    ys = []
    for hf in range(2):
        ls = slice(hf * half_st, (hf + 1) * half_st)
        hcat = jnp.concatenate([xr_s[:, ls], xi_s[:, ls]], axis=1).astype(BF16)
        ys.append(_dot(hcat, wc_ref[hf]))
    y = jnp.concatenate(ys, axis=1) + d_ref[...] * u
    y = _gelu(y)
    y_ref[...] = y * _sigmoid(_dot(y.astype(BF16), wglu_ref[...]) + bglu_ref[...])


def _s5_weights(b_re, b_im, c_re, c_im):
    gh = S5_GROUPS // 2
    eye = jnp.eye(gh, dtype=F32)

    def bmat(b):
        bt = b.astype(F32).transpose(0, 2, 1).reshape(2, gh, S5_GROUP, S5_STATE)
        return jnp.einsum('xghp,gk->xghkp', bt, eye).reshape(2, gh * S5_GROUP, gh * S5_STATE)

    def cmat(c):
        ct = c.astype(F32).transpose(0, 2, 1).reshape(2, gh, S5_STATE, S5_GROUP)
        return jnp.einsum('xgph,gk->xgpkh', ct, eye).reshape(2, gh * S5_STATE, gh * S5_GROUP)

    wb = jnp.concatenate([bmat(b_re), bmat(b_im)], axis=2).astype(BF16)
    wc = jnp.concatenate([cmat(c_re), -cmat(c_im)], axis=1).astype(BF16)
    return wb, wc


def _s5(u, h0r, h0i, disc, wb, wc, dvec, wglu, bglu, *, bsz, t):
    n = bsz * t
    pre, pim, cc = disc
    chain = t > SUBLANES
    if chain:
        tm = min(MATMUL_TILE, t)
        grid = (bsz, t // tm)
        row_spec = pl.BlockSpec((tm, S5_WIDTH), lambda b, i: (b * (t // tm) + i, 0))
        st_spec = pl.BlockSpec((1, 1, S5_LANES), lambda b, i: (b, 0, 0))
    else:
        assert t == SUBLANES
        tm = n
        grid = (1, 1)
        row_spec = pl.BlockSpec((tm, S5_WIDTH), lambda b, i: (0, 0))
        st_spec = pl.BlockSpec((bsz, 1, S5_LANES), lambda b, i: (0, 0, 0))
    st_sds = jax.ShapeDtypeStruct((bsz, 1, S5_LANES), F32)
    return pl.pallas_call(
        functools.partial(_s5_kernel, chain=chain),
        grid=grid,
        in_specs=[row_spec, st_spec, st_spec, _const(pre.shape), _const(pim.shape), _const(cc.shape),
                  _const(wb.shape), _const(wc.shape), _const((1, S5_WIDTH)), _const(wglu.shape),
                  _const((1, S5_WIDTH))],
        out_specs=[row_spec, st_spec, st_spec],
        out_shape=[jax.ShapeDtypeStruct((n, S5_WIDTH), F32), st_sds, st_sds],
        scratch_shapes=[pltpu.VMEM((tm, S5_LANES), F32), pltpu.VMEM((tm, S5_LANES), F32),
                        pltpu.VMEM((2, S5_LANES), F32)],
        name="s5_mixer")(u, h0r, h0i, pre, pim, cc, wb, wc, dvec.reshape(1, S5_WIDTH), wglu,
                         bglu.reshape(1, S5_WIDTH))


def _top_blocks(gate, n_past, col):
    g = jnp.where(col < n_past, gate, NEG)
    sel = jnp.zeros(gate.shape, F32)
    colf = col.astype(F32)
    for _ in range(MOBA_TOPK):
        m = jnp.max(g, axis=-1, keepdims=True)
        first = jnp.min(jnp.where(g == m, colf, float(LANES)), axis=-1, keepdims=True)
        pick = colf == first
        sel = jnp.where(pick & (m > 0.5 * NEG), 1.0, sel)
        g = jnp.where(pick, -3e38, g)
    return sel


def _top_blocks_t(gate, n_past):
    rowf = lax.broadcasted_iota(jnp.int32, gate.shape, 0).astype(F32)
    g = jnp.where(rowf < n_past, gate, NEG)
    sel = jnp.zeros(gate.shape, F32)
    for _ in range(MOBA_TOPK):
        m = jnp.max(g, axis=0, keepdims=True)
        first = jnp.min(jnp.where(g == m, rowf, float(LANES)), axis=0, keepdims=True)
        pick = rowf == first
        sel = jnp.where(pick & (m > 0.5 * NEG), 1.0, sel)
        g = jnp.where(pick, -3e38, g)
    return sel


def _moba_prompt_kernel(q_ref, k_ref, v_ref, o_ref, kb_s, vt_s, kmean_s, sel_s, *, nblk):
    bs = MOBA_BLOCK
    npair = MOBA_PAIRS
    width = 2 * bs
    i = pl.program_id(2)

    @pl.when(i == 0)
    def _():
        kmean_s[...] = jnp.zeros(kmean_s.shape, F32)
        for g in range(npair):
            gl = slice(g * LANES, (g + 1) * LANES)
            for j in range(nblk):
                kj = k_ref[j * bs:(j + 1) * bs, gl]
                kb_s[g, j * bs:(j + 1) * bs, :] = kj.astype(BF16)
                kmean_s[g, j:j + 1, :] = jnp.sum(kj, axis=0, keepdims=True) * (1.0 / bs)
                vt_s[g, j] = v_ref[j * bs:(j + 1) * bs, gl].T.astype(BF16)

    lane = lax.broadcasted_iota(jnp.int32, (1, LANES), 1)
    rowd = lax.broadcasted_iota(jnp.int32, (LANES, 1), 0)
    causal = (lax.broadcasted_iota(jnp.int32, (bs, npair * width), 0)
              <= lax.broadcasted_iota(jnp.int32, (bs, npair * width), 1) % bs)
    r0 = pl.multiple_of(i * bs, bs)
    qs2 = []
    gates = []
    for g in range(npair):
        qf = q_ref[:, g * LANES:(g + 1) * LANES]
        qh2 = jnp.concatenate(
            [jnp.where((lane >= HEAD_DIM * hd) & (lane < HEAD_DIM * (hd + 1)), qf, 0.0) for hd in range(2)],
            axis=0)
        qs2.append((qh2 * (HEAD_DIM ** -0.5 * LOG2E)).astype(BF16))
        gates.append(_dot_nt(kmean_s[g].astype(BF16), qh2.astype(BF16)))
    sel = _top_blocks_t(jnp.concatenate(gates, axis=1), i)
    for b in range(nblk):
        sel_s[b] = sel[b:b + 1, :]

    def scores(rows):
        return jnp.concatenate([_dot_nt(kb_s[g, rows, :], qs2[g]) for g in range(npair)], axis=1)

    def weighted(blk, p):
        return jnp.concatenate([_dot(vt_s[g, blk], p[:, g * width:(g + 1) * width]) for g in range(npair)],
                               axis=1)

    s = jnp.where(causal, scores(pl.ds(r0, bs)), NEG)
    m = jnp.max(s, axis=0, keepdims=True)
    p = jnp.exp2(s - m)
    carry = (m, jnp.sum(p, axis=0, keepdims=True), weighted(i, p.astype(BF16)))

    def kvpair(jj, carry):
        m, l, acc = carry
        s = scores(pl.ds(pl.multiple_of(jj * 2 * bs, 2 * bs), 2 * bs))
        s0 = jnp.where(sel_s[2 * jj] > 0.5, s[:bs], NEG)
        s1 = jnp.where(sel_s[2 * jj + 1] > 0.5, s[bs:], NEG)
        m_new = jnp.maximum(m, jnp.maximum(jnp.max(s0, axis=0, keepdims=True),
                                           jnp.max(s1, axis=0, keepdims=True)))
        alpha = jnp.exp2(m - m_new)
        p0 = jnp.exp2(s0 - m_new)
        p1 = jnp.exp2(s1 - m_new)
        l = alpha * l + jnp.sum(p0, axis=0, keepdims=True) + jnp.sum(p1, axis=0, keepdims=True)
        acc = alpha * acc + weighted(2 * jj, p0.astype(BF16)) + weighted(2 * jj + 1, p1.astype(BF16))
        return m_new, l, acc

    m, l, acc = lax.fori_loop(0, (i + 1) // 2, kvpair, carry)
    out = acc / l
    for g in range(npair):
        og = out[:, g * width:(g + 1) * width]
        o_ref[:, g * LANES:(g + 1) * LANES] = jnp.where(rowd < HEAD_DIM, og[:, :bs], og[:, bs:]).T


def _moba_prompt(q, k, v, *, bsz, t):
    nblk = t // MOBA_BLOCK
    assert t % MOBA_BLOCK == 0 and nblk % 2 == 0 and nblk <= LANES
    nblk_rows = -(-nblk // SUBLANES) * SUBLANES
    n = bsz * t
    gw = MOBA_PAIRS * LANES
    q_spec = pl.BlockSpec((MOBA_BLOCK, gw), lambda b, hp, i: (b * nblk + i, hp))
    kv_spec = pl.BlockSpec((t, gw), lambda b, hp, i: (b, hp), pipeline_mode=pl.Buffered(1))
    return pl.pallas_call(
        functools.partial(_moba_prompt_kernel, nblk=nblk),
        grid=(bsz, ATT_WIDTH // gw, nblk),
        in_specs=[q_spec, kv_spec, kv_spec], out_specs=q_spec,
        out_shape=jax.ShapeDtypeStruct((n, ATT_WIDTH), F32),
        scratch_shapes=[pltpu.VMEM((MOBA_PAIRS, t, LANES), BF16),
                        pltpu.VMEM((MOBA_PAIRS, nblk, LANES, MOBA_BLOCK), BF16),
                        pltpu.VMEM((MOBA_PAIRS, nblk_rows, LANES), F32),
                        pltpu.VMEM((nblk, 1, MOBA_PAIRS * 2 * MOBA_BLOCK), F32)],
        name="moba_prompt")(q, k, v)


def _expand_heads(q, qexp_s):
    t = q.shape[0]
    lane = lax.broadcasted_iota(jnp.int32, (1, ATT_WIDTH), 1)
    for h in range(HEADS):
        lm = (lane >= h * HEAD_DIM) & (lane < (h + 1) * HEAD_DIM)
        qexp_s[h * t:(h + 1) * t, :] = jnp.where(lm, q, 0.0)


def _collapse_heads(acc, t):
    lane = lax.broadcasted_iota(jnp.int32, (1, ATT_WIDTH), 1)
    out = jnp.zeros((t, ATT_WIDTH), F32)
    for h in range(HEADS):
        lm = (lane >= h * HEAD_DIM) & (lane < (h + 1) * HEAD_DIM)
        out = out + jnp.where(lm, acc[h * t:(h + 1) * t, :], 0.0)
    return out


def _pad_rows(x_ref, new_s):
    new_s[...] = jnp.zeros(new_s.shape, F32)
    new_s[0:x_ref.shape[0], :] = x_ref[...]
    return new_s[...].astype(BF16)


def _moba_sample_kernel(pt_ref, q_ref, kn_ref, vn_ref, *rest, n_pages, t):
    pps = PAGES_PER_STEP
    k_refs = rest[:pps]
    v_refs = rest[pps:2 * pps]
    o_ref = rest[2 * pps]
    qexp_s, new_s, s_all, gate_s, bmax_s, sel_s, m_s, l_s, acc_s = rest[2 * pps + 1:]
    ns = n_pages // pps
    s_id = pl.program_id(1)
    rows = HEADS * t
    col = lax.broadcasted_iota(jnp.int32, (rows, LANES), 1)
    scale = HEAD_DIM ** -0.5
    pages_per_block = MOBA_BLOCK // PAGE

    @pl.when(s_id == 0)
    def _():
        _expand_heads(q_ref[...], qexp_s)
        gate_s[...] = jnp.zeros(gate_s.shape, F32)
        bmax_s[...] = jnp.full(bmax_s.shape, NEG, F32)

    @pl.when(s_id < ns)
    def _():
        qe = qexp_s[...].astype(BF16)
        g = gate_s[...]
        bm = bmax_s[...]
        for i in range(pps):
            pg = s_id * pps + i
            s = _dot(qe, k_refs[i][...].astype(BF16))
            s_all[pg] = s
            mine = col == pg // pages_per_block
            g = g + jnp.where(mine, jnp.sum(s, axis=-1, keepdims=True) * (1.0 / MOBA_BLOCK), 0.0)
            bm = jnp.where(mine, jnp.maximum(bm, jnp.max(s, axis=-1, keepdims=True)), bm)
        gate_s[...] = g
        bmax_s[...] = bm

    @pl.when(s_id == ns)
    def _():
        sel = _top_blocks(gate_s[...], n_pages // pages_per_block, col)
        sel_s[...] = sel
        qe = qexp_s[...].astype(BF16)
        sn = _dot_nt(qe, _pad_rows(kn_ref, new_s)) * scale
        rowt = lax.broadcasted_iota(jnp.int32, (rows, LANES), 0) % t
        sn = jnp.where(col <= rowt, sn, NEG)
        m = jnp.maximum(jnp.max(sn, axis=-1, keepdims=True),
                        jnp.max(jnp.where(sel > 0.5, bmax_s[...] * scale, NEG), axis=-1, keepdims=True))
        pn = jnp.exp(sn - m)
        m_s[...] = jnp.broadcast_to(m, m_s.shape)
        l_s[...] = jnp.broadcast_to(jnp.sum(pn, axis=-1, keepdims=True), l_s.shape)
        acc_s[...] = _dot(pn.astype(BF16), _pad_rows(vn_ref, new_s))

    @pl.when(s_id >= ns)
    def _():
        sel = sel_s[...]
        m = m_s[:, 0:1]
        l = l_s[:, 0:1]
        acc = acc_s[...]
        for i in range(pps):
            pg = (s_id - ns) * pps + i
            selc = jnp.sum(jnp.where(col == pg // pages_per_block, sel, 0.0), axis=-1, keepdims=True)
            p = jnp.exp(jnp.where(selc > 0.5, s_all[pg] * scale, NEG) - m)
            l = l + jnp.sum(p, axis=-1, keepdims=True)
            acc = acc + _dot_nt(p.astype(BF16), v_refs[i][...].astype(BF16))
        l_s[...] = jnp.broadcast_to(l, l_s.shape)
        acc_s[...] = acc

    @pl.when(s_id == 2 * ns - 1)
    def _():
        o_ref[...] = _collapse_heads(acc_s[...] / l_s[:, 0:1], t)


def _sample_specs(page_table, n_pages, t, page_index, pps=PAGES_PER_STEP):
    row_spec = pl.BlockSpec((t, ATT_WIDTH), lambda b, s, pt: (b, 0))

    def page_spec(i, which):
        return pl.BlockSpec((None, ATT_WIDTH, PAGE),
                            lambda b, s, pt: (pt[b, page_index(s, i, which)], 0, 0))

    in_specs = ([row_spec] * 3 + [page_spec(i, 0) for i in range(pps)]
                + [page_spec(i, 1) for i in range(pps)])
    return in_specs, row_spec


def _moba_sample(q, k_new, v_new, k_pool, v_pool, page_table, *, bsz, t):
    n_pages = page_table.shape[1]
    pps = PAGES_PER_STEP
    assert n_pages % pps == 0 and (n_pages * PAGE) % MOBA_BLOCK == 0 and t <= SUBLANES
    assert n_pages * PAGE // MOBA_BLOCK <= LANES
    ns = n_pages // pps
    rows = HEADS * t

    def page_index(s, i, which):
        step = jnp.minimum(s, ns - 1) if which == 0 else jnp.maximum(s - ns, 0)
        return step * pps + i

    in_specs, out_spec = _sample_specs(page_table, n_pages, t, page_index)
    grid_spec = pltpu.PrefetchScalarGridSpec(
        num_scalar_prefetch=1, grid=(bsz, 2 * ns), in_specs=in_specs, out_specs=out_spec,
        scratch_shapes=[pltpu.VMEM((rows, ATT_WIDTH), F32),
                        pltpu.VMEM((PAGE, ATT_WIDTH), F32),
                        pltpu.VMEM((n_pages, rows, PAGE), F32)]
                       + [pltpu.VMEM((rows, LANES), F32)] * 5
                       + [pltpu.VMEM((rows, ATT_WIDTH), F32)])
    return pl.pallas_call(
        functools.partial(_moba_sample_kernel, n_pages=n_pages, t=t),
        grid_spec=grid_spec,
        out_shape=jax.ShapeDtypeStruct((bsz * t, ATT_WIDTH), F32),
        name="moba_sample")(page_table, q, k_new, v_new, *([k_pool] * pps), *([v_pool] * pps))


def _sb_weights(z, r, tri, strict):
    n = z.shape[0]
    lg = jnp.log2(1.0 + jnp.exp2(-jnp.abs(z)))
    log_beta = jnp.minimum(z, 0.0) - lg
    log_keep = jnp.minimum(-z, 0.0) - lg
    if strict is not None:
        log_beta = jnp.where(strict, log_beta, NEG)
        log_keep = jnp.where(strict, log_keep, 0.0)
    st = _dot(jnp.concatenate(_split3(log_keep), axis=0), tri)
    later = st[:n] + st[n:2 * n] + st[2 * n:]
    w = jnp.exp2(log_beta + later + r)
    return w.astype(BF16), r + jnp.sum(log_keep, axis=-1, keepdims=True)


def _suffix_matrix(n):
    return (lax.broadcasted_iota(jnp.int32, (n, n), 0)
            > lax.broadcasted_iota(jnp.int32, (n, n), 1)).astype(BF16)


def _sb_prompt_kernel(q_ref, k_ref, v_ref, o_ref, kb_s, vb_s):
    bs = SB_BLOCK
    i = pl.program_id(2)

    @pl.when(i == 0)
    def _():
        for j in range(k_ref.shape[0] // bs):
            kb_s[j * bs:(j + 1) * bs, :] = k_ref[j * bs:(j + 1) * bs, :].astype(BF16)
            vb_s[j * bs:(j + 1) * bs, :] = v_ref[j * bs:(j + 1) * bs, :].astype(BF16)

    lane = lax.broadcasted_iota(jnp.int32, (1, LANES), 1)
    tri = _suffix_matrix(bs)
    strict = (lax.broadcasted_iota(jnp.int32, (2 * bs, bs), 1)
              < lax.broadcasted_iota(jnp.int32, (2 * bs, bs), 0) % bs)
    qf = q_ref[...]
    qhs = []
    for hd in range(2):
        lm = (lane >= HEAD_DIM * hd) & (lane < HEAD_DIM * (hd + 1))
        qhs.append(jnp.where(lm, qf, 0.0))
    qs2 = (jnp.concatenate(qhs, axis=0) * (HEAD_DIM ** -0.5 * LOG2E)).astype(BF16)

    def block(j, r, mask):
        c0 = pl.multiple_of(j * bs, bs)
        w, r = _sb_weights(_dot_nt(qs2, kb_s[pl.ds(c0, bs), :]), r, tri, mask)
        return _dot(w, vb_s[pl.ds(c0, bs), :]), r

    acc, r = block(i, jnp.zeros((2 * bs, 1), F32), strict)
    has_prev = lax.broadcasted_iota(jnp.int32, (2 * bs, bs), 0) < jnp.where(i > 0, 2 * bs, 0)
    d, r = block(jnp.maximum(i - 1, 0), r, has_prev)
    acc = acc + d

    def cond(c):
        j, r, _ = c
        return (j >= 0) & (jnp.max(r) > EXP2_UNDERFLOW)

    def body(c):
        j, r, acc = c
        d, r = block(j, r, None)
        return j - 1, r, acc + d

    _, _, acc = lax.while_loop(cond, body, (i - 2, r, acc))
    o_ref[...] = jnp.where(lane < HEAD_DIM, acc[:bs], acc[bs:])


def _sb_prompt(q, k, v, *, bsz, t):
    assert t % SB_BLOCK == 0
    n = bsz * t
    nq = t // SB_BLOCK
    q_spec = pl.BlockSpec((SB_BLOCK, LANES), lambda b, hp, i: (b * nq + i, hp))
    kv_spec = pl.BlockSpec((t, LANES), lambda b, hp, i: (b, hp))
    return pl.pallas_call(
        _sb_prompt_kernel,
        grid=(bsz, ATT_WIDTH // LANES, nq),
        in_specs=[q_spec, kv_spec, kv_spec], out_specs=q_spec,
        out_shape=jax.ShapeDtypeStruct((n, ATT_WIDTH), F32),
        scratch_shapes=[pltpu.VMEM((t, LANES), BF16), pltpu.VMEM((t, LANES), BF16)],
        name="sb_prompt")(q, k, v)


def _sb_sample_kernel(pt_ref, q_ref, a_ref, b_ref, *rest, t, first, pps):
    k_refs = rest[:pps]
    v_refs = rest[pps:2 * pps]
    n_out = 3 if first else 1
    o_ref = rest[2 * pps]
    qexp_s, new_s, r_s, acc_s = rest[2 * pps + n_out:]
    s_id = pl.program_id(1)
    rows = HEADS * t
    tri = _suffix_matrix(PAGE)

    @pl.when(s_id == 0)
    def _():
        _expand_heads(q_ref[...] * (HEAD_DIM ** -0.5 * LOG2E), qexp_s)
        if first:
            z = _dot_nt(qexp_s[...].astype(BF16), _pad_rows(a_ref, new_s))
            col = lax.broadcasted_iota(jnp.int32, (rows, PAGE), 1)
            rowt = lax.broadcasted_iota(jnp.int32, (rows, PAGE), 0) % t
            w, r = _sb_weights(z, jnp.zeros((rows, 1), F32), tri, col < rowt)
            acc_s[...] = _dot(w, _pad_rows(b_ref, new_s))
            r_s[...] = jnp.broadcast_to(r, r_s.shape)
        else:
            acc_s[...] = a_ref[...]
            r_s[...] = b_ref[...]

    for i in range(pps):
        @pl.when(jnp.max(r_s[...]) > EXP2_UNDERFLOW)
        def _(i=i):
            z = _dot(qexp_s[...].astype(BF16), k_refs[i][...].astype(BF16))
            w, r = _sb_weights(z, r_s[:, 0:1], tri, None)
            acc_s[...] = acc_s[...] + _dot_nt(w, v_refs[i][...].astype(BF16))
            r_s[...] = jnp.broadcast_to(r, r_s.shape)

    @pl.when(s_id == pl.num_programs(1) - 1)
    def _():
        o_ref[...] = _collapse_heads(acc_s[...], t)
        if first:
            rest[2 * pps + 1][...] = acc_s[...]
            rest[2 * pps + 2][...] = r_s[...]


def _sb_sample(q, k_new, v_new, k_pool, v_pool, page_table, *, bsz, t):
    n_pages = page_table.shape[1]
    assert t <= SUBLANES
    first = min(SB_FIRST_PAGES, n_pages)
    rows = HEADS * t
    n = bsz * t
    scratch = [pltpu.VMEM((rows, ATT_WIDTH), F32), pltpu.VMEM((PAGE, ATT_WIDTH), F32),
               pltpu.VMEM((rows, LANES), F32), pltpu.VMEM((rows, ATT_WIDTH), F32)]
    acc_spec = pl.BlockSpec((rows, ATT_WIDTH), lambda b, s, pt: (b, 0))
    r_spec = pl.BlockSpec((rows, LANES), lambda b, s, pt: (b, 0))
    y_sds = jax.ShapeDtypeStruct((n, ATT_WIDTH), F32)

    in_specs, y_spec = _sample_specs(page_table, n_pages, t, lambda s, i, which: n_pages - 1 - i, first)
    y, acc, r = pl.pallas_call(
        functools.partial(_sb_sample_kernel, t=t, first=True, pps=first),
        grid_spec=pltpu.PrefetchScalarGridSpec(
            num_scalar_prefetch=1, grid=(bsz, 1), in_specs=in_specs,
            out_specs=[y_spec, acc_spec, r_spec], scratch_shapes=scratch),
        out_shape=[y_sds, jax.ShapeDtypeStruct((bsz * rows, ATT_WIDTH), F32),
                   jax.ShapeDtypeStruct((bsz * rows, LANES), F32)],
        name="sb_sample")(page_table, q, k_new, v_new, *([k_pool] * first), *([v_pool] * first))
    older = n_pages - first
    if older == 0:
        return y
    pps = max(d for d in range(1, PAGES_PER_STEP + 1) if older % d == 0)

    def older_pages():
        specs, _ = _sample_specs(page_table, n_pages, t,
                                 lambda s, i, which: n_pages - 1 - first - (s * pps + i), pps)
        specs = [specs[0], acc_spec, r_spec] + specs[3:]
        return pl.pallas_call(
            functools.partial(_sb_sample_kernel, t=t, first=False, pps=pps),
            grid_spec=pltpu.PrefetchScalarGridSpec(
                num_scalar_prefetch=1, grid=(bsz, older // pps), in_specs=specs,
                out_specs=y_spec, scratch_shapes=scratch),
            out_shape=y_sds,
            name="sb_sample_older")(page_table, q, acc, r, *([k_pool] * pps), *([v_pool] * pps))

    return lax.cond(jnp.max(r) > EXP2_UNDERFLOW, older_pages, lambda: y)


def _hgrn_kernel(q_ref, f_ref, i_ref, g_ref, lb_ref, s0_ref, gn_ref, o_ref, s_out_ref, st_s,
                 *, layer):
    tb = q_ref.shape[0]
    t = pl.program_id(1)

    @pl.when(t == 0)
    def _():
        for hh in range(HGRN_HEADS):
            st_s[hh] = s0_ref[0, hh].T

    lbp = lb_ref[...]
    e = jnp.exp(lbp - jnp.max(lbp, axis=0, keepdims=True))
    soft = e / jnp.sum(e, axis=0, keepdims=True)
    lbv = jnp.sum(soft[1:layer + 1], axis=0, keepdims=True)

    pad = max(HGRN_SUB - tb, 0)
    L = min(HGRN_CHUNK, tb + pad)
    c = min(HGRN_SUB, L)
    tri = (lax.broadcasted_iota(jnp.int32, (L, L), 0)
           >= lax.broadcasted_iota(jnp.int32, (L, L), 1)).astype(BF16)
    rowc = lax.broadcasted_iota(jnp.int32, (c, 1), 0)
    gn = gn_ref[...]

    def padrows(x):
        if pad == 0:
            return x
        return jnp.concatenate([x, jnp.zeros((pad, x.shape[1]), F32)], axis=0)

    def chunk(rows, n_valid):
        for hh in range(HGRN_HEADS):
            ls = slice(hh * HGRN_DIM, (hh + 1) * HGRN_DIM)
            lbh = lbv[:, ls]
            f = lbh + (1.0 - lbh) * _sigmoid(f_ref[rows, ls])
            lf = padrows(jnp.log(f))
            kk = padrows(1.0 - f)
            qv = padrows(q_ref[rows, ls])
            vv = padrows(i_ref[rows, ls])
            hi, mid, lo = _split3(lf)
            b = _dot(tri, hi) + _dot(tri, mid) + _dot(tri, lo)
            st = st_s[hh]
            o_inter = _dot_nt((qv * jnp.exp(b)).astype(BF16), st.astype(BF16))
            vb = vv.astype(BF16)
            parts = []
            for si in range(L // c):
                rs = slice(si * c, (si + 1) * c)
                b_i = b[rs]
                q_i = qv[rs]
                k_i = kk[rs]
                v_i = vv[rs]
                o_i = o_inter[rs]
                if si > 0:
                    b_prev = b[si * c - 1:si * c]
                    qt = (q_i * jnp.exp(b_i - b_prev)).astype(BF16)
                    kt = (kk[:si * c] * jnp.exp(b_prev - b[:si * c])).astype(BF16)
                    o_i = o_i + _dot(_dot_nt(qt, kt).astype(BF16), vb[:si * c])
                for s in range(min(c, max(n_valid - si * c, 0))):
                    dec = jnp.exp(jnp.minimum(b_i - b_i[s:s + 1], 0.0))
                    a = jnp.sum(q_i * k_i[s:s + 1] * dec, axis=-1, keepdims=True)
                    o_i = o_i + jnp.where(rowc >= s, a, 0.0) * v_i[s:s + 1]
                parts.append(o_i)
            o = jnp.concatenate(parts, axis=0) if len(parts) > 1 else parts[0]
            o = o[:n_valid]
            o = o * lax.rsqrt(jnp.mean(o * o, axis=-1, keepdims=True) + EPS) * gn
            gv = g_ref[rows, ls]
            o_ref[rows, ls] = o * (gv * _sigmoid(gv))
            b_last = b[L - 1:L]
            kdec = (kk * jnp.exp(b_last - b)).astype(BF16)
            st_s[hh] = st * jnp.exp(b_last) + _dot_tn(vb, kdec)

    if pad:
        chunk(slice(0, tb), tb)
    else:
        def body(ch, _):
            chunk(pl.ds(pl.multiple_of(ch * L, L), L), L)
            return 0

        lax.fori_loop(0, tb // L, body, 0, unroll=True)

    for hh in range(HGRN_HEADS):
        s_out_ref[0, hh] = st_s[hh].T


def _hgrn(qh, fh, ih, gh, lb, s0, gnorm, *, bsz, t, layer):
    n = bsz * t
    tb = min(MATMUL_TILE, t)
    nt = t // tb
    row_spec = pl.BlockSpec((tb, SLAB), lambda b, i: (b * nt + i, 0))
    st_spec = pl.BlockSpec((1, HGRN_HEADS, HGRN_DIM, HGRN_DIM), lambda b, i: (b, 0, 0, 0))
    return pl.pallas_call(
        functools.partial(_hgrn_kernel, layer=layer),
        grid=(bsz, nt),
        in_specs=[row_spec] * 4 + [_const(lb.shape), st_spec, _const((1, HGRN_DIM))],
        out_specs=[row_spec, st_spec],
        out_shape=[jax.ShapeDtypeStruct((n, SLAB), F32),
                   jax.ShapeDtypeStruct((bsz, HGRN_HEADS, HGRN_DIM, HGRN_DIM), F32)],
        scratch_shapes=[pltpu.VMEM((HGRN_HEADS, HGRN_DIM, HGRN_DIM), F32)],
        name="hgrn2")(qh, fh, ih, gh, lb, s0, gnorm.reshape(1, HGRN_DIM))


def _tail_kernel(*refs, chain):
    if chain:
        (h_ref, ya_ref, yb_ref, p_ref, wo_ref, gmp_ref, gfp_ref, wup_ref, wcv_ref, wdn_ref,
         gfo_ref, wpp_ref, wpg_ref, gpl_ref, ho_ref, ffn_ref, u_s, h1_s, a_s, act_s, car_s) = refs
    else:
        (h_ref, ya_ref, yb_ref, p_ref, p1_ref, p2_ref, wo_ref, gmp_ref, gfp_ref, wup_ref,
         wcv_ref, wdn_ref, gfo_ref, wpp_ref, wpg_ref, gpl_ref, ho_ref, ffn_ref, u_s, h1_s, a_s, act_s) = refs
    tm = h_ref.shape[0]
    nch = N_FF_CHUNKS
    cw = FF_CHUNK
    mix = _dot(ya_ref[...].astype(BF16), wo_ref[0]) + _dot(yb_ref[...].astype(BF16), wo_ref[1])
    h1 = h_ref[...] + _rms(mix, gmp_ref[...])
    h1_s[...] = h1
    a_s[...] = _rms(h1, gfp_ref[...]).astype(BF16)
    rowi = lax.broadcasted_iota(jnp.int32, (tm, cw), 0)

    if chain:
        @pl.when(pl.program_id(1) == 0)
        def _():
            car_s[...] = jnp.zeros(car_s.shape, F32)

    def taps(w, p2, p1, u):
        return w[3:4] + w[0:1] * p2 + w[1:2] * p1 + w[2:3] * u

    def up(c):
        a = a_s[...]
        for half in range(2):
            cols = slice(half * D_FF + c * cw, half * D_FF + (c + 1) * cw)
            u_s[c % 2, half, SUBLANES:, :] = _dot(a, wup_ref[:, cols])

    def down(c):
        slot = c % 2
        ws = [wcv_ref[:, half * D_FF + c * cw:half * D_FF + (c + 1) * cw] for half in range(2)]
        if chain:
            for half in range(2):
                u_s[slot, half, 0:SUBLANES, :] = car_s[half * nch + c]
            for rb in range(tm // CONV_ROWS):
                cs = []
                for half in range(2):
                    x = u_s[slot, half, rb * CONV_ROWS:(rb + 1) * CONV_ROWS + SUBLANES, :]
                    cs.append(taps(ws[half], pltpu.roll(x, 2, 0)[SUBLANES:],
                                   pltpu.roll(x, 1, 0)[SUBLANES:], x[SUBLANES:]))
                act_s[rb * CONV_ROWS:(rb + 1) * CONV_ROWS, c * cw:(c + 1) * cw] = (
                    _gelu(cs[0]) * cs[1]).astype(BF16)
            for half in range(2):
                last = u_s[slot, half, tm:tm + SUBLANES, :]
                car_s[half * nch + c] = last
                ffn_ref[0, half * nch + c] = last
        else:
            rm = rowi % SUBLANES
            cs = []
            for half in range(2):
                u = u_s[slot, half, SUBLANES:, :]
                p1 = jnp.where(rm == 0, p1_ref[half * nch + c], pltpu.roll(u, 1, 0))
                p2 = jnp.where(rm < 2, p2_ref[half * nch + c], pltpu.roll(u, 2, 0))
                cs.append(taps(ws[half], p2, p1, u))
                ffn_ref[half * nch + c] = u
            act_s[:, c * cw:(c + 1) * cw] = (_gelu(cs[0]) * cs[1]).astype(BF16)

    up(0)
    for c in range(nch):
        if c + 1 < nch:
            up(c + 1)
        down(c)
    h2 = h1_s[...] + _rms(_dot(act_s[...], wdn_ref[...]), gfo_ref[...])
    ple = _dot(p_ref[...].astype(BF16), wpp_ref[...]) * _sigmoid(_dot(h2.astype(BF16), wpg_ref[...]))
    ho_ref[...] = h2 + _rms(ple, gpl_ref[...])


def _tail(h, ya, yb, p, ffn0, w, *, bsz, t):
    n, d = h.shape
    nch = N_FF_CHUNKS
    cw = FF_CHUNK
    chain = t > SUBLANES
    consts = [w['wo'], w['gmp'], w['gfp'], w['wup'], w['wcv'], w['wdn'], w['gfo'], w['wpp'], w['wpg'],
              w['gpl']]
    const_specs = [_const(x.shape) for x in consts]
    if chain:
        assert ffn0 is None, "a long sequence starts from an empty ConvFFN buffer"
        tm = min(MATMUL_TILE, t)
        nt = t // tm
        grid = (bsz, nt)
        rows = lambda width: pl.BlockSpec((tm, width), lambda b, i: (b * nt + i, 0))
        ffn_spec = pl.BlockSpec((1, 2 * nch, SUBLANES, cw), lambda b, i: (b, 0, 0, 0))
        ffn_sds = jax.ShapeDtypeStruct((bsz, 2 * nch, SUBLANES, cw), F32)
        extra, extra_specs = [], []
        scratch = [pltpu.VMEM((2, 2, tm + SUBLANES, cw), F32),
                   pltpu.VMEM((tm, d), F32), pltpu.VMEM((tm, d), BF16), pltpu.VMEM((tm, D_FF), BF16),
                   pltpu.VMEM((2 * nch, SUBLANES, cw), F32)]
    else:
        assert t == SUBLANES
        tm = n
        grid = (1, 1)
        rows = lambda width: pl.BlockSpec((tm, width), lambda b, i: (0, 0))
        ffn_spec = _full((2 * nch, tm, cw))
        ffn_sds = jax.ShapeDtypeStruct((2 * nch, tm, cw), F32)
        buf = ffn0.astype(F32).reshape(bsz, 2, 2 * nch, cw).transpose(2, 0, 1, 3)
        zero = jnp.zeros((2 * nch, bsz, SUBLANES - 2, cw), F32)
        p1 = jnp.concatenate([buf[:, :, 1:2], zero, zero[:, :, :1]], axis=2).reshape(2 * nch, tm, cw)
        p2 = jnp.concatenate([buf, zero], axis=2).reshape(2 * nch, tm, cw)
        extra = [p1, p2]
        extra_specs = [_full(p1.shape), _full(p2.shape)]
        scratch = [pltpu.VMEM((2, 2, tm + SUBLANES, cw), F32),
                   pltpu.VMEM((tm, d), F32), pltpu.VMEM((tm, d), BF16), pltpu.VMEM((tm, D_FF), BF16)]
    ho, ffn = pl.pallas_call(
        functools.partial(_tail_kernel, chain=chain),
        grid=grid,
        in_specs=[rows(d), rows(SLAB), rows(SLAB), rows(p.shape[1])] + extra_specs + const_specs,
        out_specs=[rows(d), ffn_spec],
        out_shape=[jax.ShapeDtypeStruct((n, d), F32), ffn_sds],
        scratch_shapes=scratch,
        name="layer_tail")(h, ya, yb, p, *extra, *consts)
    if chain:
        st = ffn[:, :, SUBLANES - 2:, :]
    else:
        st = ffn.reshape(2 * nch, bsz, SUBLANES, cw)[:, :, SUBLANES - 2:, :].transpose(1, 0, 2, 3)
    return ho, st.transpose(0, 2, 1, 3).reshape(bsz, 2, 2 * nch * cw)


def _tail_weights(i, w_out, g_mix_post, g_ffn_pre, g_ffn_post, w_ffn_up, w_ffn_conv, b_ffn_conv,
                  w_ffn_down, w_ple_proj, w_ple_gate, g_ple):
    d = w_out.shape[1]
    taps = jnp.concatenate([w_ffn_conv[i].astype(F32), b_ffn_conv[i].astype(F32)[None]], axis=0)
    row = lambda g: g.astype(F32).reshape(1, d)
    return dict(
        wo=w_out.astype(BF16).reshape(2, SLAB, d), gmp=row(g_mix_post[i]), gfp=row(g_ffn_pre[i]),
        wup=w_ffn_up[i].astype(BF16), wcv=taps, wdn=w_ffn_down[i].astype(BF16),
        gfo=row(g_ffn_post[i]), wpp=w_ple_proj[i].astype(BF16), wpg=w_ple_gate[i].astype(BF16),
        gpl=row(g_ple[i]))


def kernel(x_prompt, x_sample, p_prompt, p_sample, cache_moba_k, cache_moba_v, state_s5_re, state_s5_im,
           state_hgrn, cache_sb_k, cache_sb_v, state_ffn, page_table,
           g_mix_pre, g_mix_post, g_ffn_pre, g_ffn_post, w_ffn_up, w_ffn_conv, b_ffn_conv, w_ffn_down,
           w_ple_proj, w_ple_gate, g_ple,
           w_in_a, w_out_a, s5_a_re, s5_a_im, s5_log_dt, s5_b_re, s5_b_im, s5_c_re, s5_c_im, s5_d,
           s5_w_glu, s5_b_glu,
           w_in_c, w_out_c, hgrn_lb, g_hgrn_norm):
    depth = g_mix_pre.shape[0]
    d_model = x_prompt.shape[-1]
    n_pages = page_table.shape[1]
    past_len = n_pages * cache_moba_k.shape[2]
    page_table = page_table.astype(jnp.int32)

    layers = []
    for i in range(depth):
        j = i // 2
        lw = {}
        if i % 2 == 0:
            lw['w_in'] = w_in_a[j].astype(BF16)
            lw['disc'] = _s5_discretize(s5_a_re[j], s5_a_im[j], s5_log_dt[j])
            lw['wb'], lw['wc'] = _s5_weights(s5_b_re[j], s5_b_im[j], s5_c_re[j], s5_c_im[j])
            lw['d'] = s5_d[j].astype(F32).reshape(S5_WIDTH)
            lw['wglu'] = s5_w_glu[j].astype(BF16)
            lw['bglu'] = s5_b_glu[j].astype(F32)
            w_out = w_out_a[j]
        else:
            lw['w_in'] = w_in_c[j].astype(BF16)
            w_out = w_out_c[j]
        lw['tail'] = _tail_weights(i, w_out, g_mix_post, g_ffn_pre, g_ffn_post, w_ffn_up, w_ffn_conv,
                                   b_ffn_conv, w_ffn_down, w_ple_proj, w_ple_gate, g_ple)
        layers.append(lw)

    def pool2d(pool, j):
        return pool[j].transpose(0, 2, 3, 1).reshape(pool.shape[1], ATT_WIDTH, pool.shape[2])

    def run(x, p, q_start, s5_re0, s5_im0, hgrn0, ffn0, has_past):
        bsz, t, _ = x.shape
        n = bsz * t
        h = x.astype(F32).reshape(n, d_model)
        tables = _rope_tables(t, q_start)
        if t < MATMUL_TILE:
            tables = tuple(jnp.tile(tb, (n // t, 1)) for tb in tables)
        mk, mv, sr, si, hs, sk, sv, fb = [], [], [], [], [], [], [], []
        long_seq = t % MATMUL_TILE == 0

        def kv_out(x, x_t, which):
            if x_t:
                return x_t[which].reshape(bsz, HEADS, HEAD_DIM, t).transpose(0, 3, 1, 2)
            return x.reshape(bsz, t, HEADS, HEAD_DIM)

        for i in range(depth):
            j = i // 2
            lw = layers[i]
            g_pre = g_mix_pre[i].astype(F32)
            if i % 2 == 0:
                u, q, k, v, *kv_t = _inproj(h, g_pre, lw['w_in'], tables, rope_slabs=(1, 2),
                                            t_slabs=(2, 3) if long_seq else (), seq=(bsz, t))
                y_a, hr, hi = _s5(u, s5_re0[j].astype(F32).reshape(bsz, 1, S5_LANES),
                                  s5_im0[j].astype(F32).reshape(bsz, 1, S5_LANES), lw['disc'], lw['wb'],
                                  lw['wc'], lw['d'], lw['wglu'], lw['bglu'], bsz=bsz, t=t)
                if has_past:
                    y_b = _moba_sample(q, k, v, pool2d(cache_moba_k, j), pool2d(cache_moba_v, j),
                                       page_table, bsz=bsz, t=t)
                else:
                    y_b = _moba_prompt(q, k, v, bsz=bsz, t=t)
                mk.append(kv_out(k, kv_t, 0))
                mv.append(kv_out(v, kv_t, 1))
                sr.append(hr.reshape(bsz, S5_GROUPS, S5_STATE))
                si.append(hi.reshape(bsz, S5_GROUPS, S5_STATE))
            else:
                qh, fh, ih, gh, q, k, v, *kv_t = _inproj(h, g_pre, lw['w_in'], tables, rope_slabs=(),
                                                         t_slabs=(5, 6) if long_seq else (), seq=(bsz, t))
                y_a, s_fin = _hgrn(qh, fh, ih, gh, hgrn_lb.astype(F32), hgrn0[j].astype(F32),
                                   g_hgrn_norm[j].astype(F32), bsz=bsz, t=t, layer=i)
                if has_past:
                    y_b = _sb_sample(q, k, v, pool2d(cache_sb_k, j), pool2d(cache_sb_v, j), page_table,
                                     bsz=bsz, t=t)
                else:
                    y_b = _sb_prompt(q, k, v, bsz=bsz, t=t)
                hs.append(s_fin)
                sk.append(kv_out(k, kv_t, 0))
                sv.append(kv_out(v, kv_t, 1))
            h, buf = _tail(h, y_a, y_b, p[i].astype(F32).reshape(n, p.shape[-1]),
                           None if ffn0 is None else ffn0[i], lw['tail'], bsz=bsz, t=t)
            fb.append(buf)
        return (h.reshape(bsz, t, d_model), jnp.stack(mk), jnp.stack(mv), jnp.stack(sr), jnp.stack(si),
                jnp.stack(hs), jnp.stack(sk), jnp.stack(sv), jnp.stack(fb))

    bp = x_prompt.shape[0]
    n_a = (depth + 1) // 2
    n_c = depth // 2
    outs_p = run(x_prompt, p_prompt, 0,
                 jnp.zeros((n_a, bp, S5_GROUPS, S5_STATE), F32), jnp.zeros((n_a, bp, S5_GROUPS, S5_STATE), F32),
                 jnp.zeros((n_c, bp, HGRN_HEADS, HGRN_DIM, HGRN_DIM), F32), None, False)
    outs_s = run(x_sample, p_sample, past_len, state_s5_re, state_s5_im, state_hgrn, state_ffn, True)
    return (outs_p[0], outs_s[0]) + tuple(outs_p[1:]) + tuple(outs_s[1:])
```

```python
import functools
import math

import jax
import jax.numpy as jnp
from jax import lax
from jax.experimental import pallas as pl
from jax.experimental.pallas import tpu as pltpu

F32 = jnp.float32
BF16 = jnp.bfloat16

S5_WIDTH = 512
S5_GROUPS = 32
S5_GROUP = 16
S5_STATE = 64
S5_LANES = S5_GROUPS * S5_STATE
HEADS = 8
HEAD_DIM = 64
ATT_WIDTH = HEADS * HEAD_DIM
ROPE_DIM = 16
ROPE_THETA = 500000.0
MOBA_BLOCK = 256
MOBA_TOPK = 3
MOBA_PAIRS = 4
HGRN_HEADS = 4
HGRN_DIM = 128
HGRN_CHUNK = 64
HGRN_SUB = 16
D_FF = 2816
FF_CHUNK = 256
N_FF_CHUNKS = D_FF // FF_CHUNK
PAGE = 128
EPS = 1e-6
NEG = -1e30
SLAB = 512
LANES = 128
SUBLANES = 8
MATMUL_TILE = 512
CONV_ROWS = 128
SB_BLOCK = 256
PAGES_PER_STEP = 16
SB_FIRST_PAGES = 4
LOG2E = 1.4426950408889634
EXP2_UNDERFLOW = -150.0


def _dot(a, b):
    return jnp.dot(a, b, preferred_element_type=F32)


def _dot_nt(a, b):
    return lax.dot_general(a, b, (((1,), (1,)), ((), ())), preferred_element_type=F32)


def _dot_tn(a, b):
    return lax.dot_general(a, b, (((0,), (0,)), ((), ())), preferred_element_type=F32)


def _split3(x):
    hi = x.astype(BF16)
    r1 = x - hi.astype(F32)
    mid = r1.astype(BF16)
    lo = (r1 - mid.astype(F32)).astype(BF16)
    return hi, mid, lo


def _rms(x, g):
    return x * lax.rsqrt(jnp.mean(x * x, axis=-1, keepdims=True) + EPS) * g


def _sigmoid(x):
    return 1.0 / (1.0 + jnp.exp(-x))


def _gelu(x):
    return 0.5 * x * (1.0 + jnp.tanh(0.7978845608028654 * (x + 0.044715 * (x * x * x))))


def _full(shape):
    nd = len(shape)
    return pl.BlockSpec(shape, lambda *_: (0,) * nd)


def _const(shape):
    nd = len(shape)
    return pl.BlockSpec(shape, lambda *_: (0,) * nd, pipeline_mode=pl.Buffered(1))


def _rope_table_kernel(c_ref, s1_ref, s2_ref, *, q_start):
    rows, lanes = c_ref.shape
    i = pl.program_id(0)
    lane = lax.broadcasted_iota(jnp.int32, (rows, lanes), 1)
    pos = lax.broadcasted_iota(jnp.int32, (rows, lanes), 0) + i * rows + q_start
    d = lane % HEAD_DIM
    half = ROPE_DIM // 2
    inv = jnp.exp((d % half).astype(F32) * (-math.log(ROPE_THETA) / half))
    ang = pos.astype(F32) * inv
    cs = jnp.cos(ang)
    sn = jnp.sin(ang)
    c_ref[...] = jnp.where(d < ROPE_DIM, cs, 1.0)
    s1_ref[...] = jnp.where(d < half, -sn, 0.0)
    s2_ref[...] = jnp.where((d >= half) & (d < ROPE_DIM), sn, 0.0)


def _rope_tables(t, q_start):
    rows = min(t, 512)
    sds = jax.ShapeDtypeStruct((t, LANES), F32)
    spec = pl.BlockSpec((rows, LANES), lambda i: (i, 0))
    return pl.pallas_call(
        functools.partial(_rope_table_kernel, q_start=q_start),
        grid=(t // rows,), out_shape=(sds, sds, sds), out_specs=(spec, spec, spec),
        name="rope_tables")()


def _inproj_kernel(h_ref, g_ref, w_ref, c_ref, s1_ref, s2_ref, *out_refs, n_slabs, rope_slabs,
                   t_slabs):
    a = _rms(h_ref[...], g_ref[...]).astype(BF16)
    half = ROPE_DIM // 2
    for s in range(n_slabs):
        o_ref = out_refs[s]
        z = _dot(a, w_ref[:, s * SLAB:(s + 1) * SLAB])
        if s in rope_slabs:
            c = c_ref[...]
            s1 = s1_ref[...]
            s2 = s2_ref[...]
            for j in range(SLAB // LANES):
                x = z[:, j * LANES:(j + 1) * LANES]
                o_ref[:, j * LANES:(j + 1) * LANES] = (
                    x * c + pltpu.roll(x, LANES - half, 1) * s1 + pltpu.roll(x, half, 1) * s2)
        else:
            o_ref[...] = z
        if s in t_slabs:
            out_refs[n_slabs + t_slabs.index(s)][...] = o_ref[...].T


def _inproj(h, g, w, tables, rope_slabs, t_slabs=(), seq=None):
    n, d = h.shape
    n_slabs = w.shape[1] // SLAB
    tm = min(MATMUL_TILE, n)
    assert n % tm == 0 and tables[0].shape[0] % tm == 0
    t_tiles = tables[0].shape[0] // tm
    tab_spec = pl.BlockSpec((tm, LANES), lambda i: (i % t_tiles, 0))
    out_spec = pl.BlockSpec((tm, SLAB), lambda i: (i, 0))
    out_specs = [out_spec] * n_slabs
    out_shape = [jax.ShapeDtypeStruct((n, SLAB), F32)] * n_slabs
    if t_slabs:
        bsz, t = seq
        assert t % tm == 0
        nt = t // tm
        out_specs += [pl.BlockSpec((None, SLAB, tm), lambda i: (i // nt, 0, i % nt))] * len(t_slabs)
        out_shape += [jax.ShapeDtypeStruct((bsz, SLAB, t), F32)] * len(t_slabs)
    return pl.pallas_call(
        functools.partial(_inproj_kernel, n_slabs=n_slabs, rope_slabs=rope_slabs, t_slabs=t_slabs),
        grid=(n // tm,),
        in_specs=[pl.BlockSpec((tm, d), lambda i: (i, 0)), _const((1, d)), _const(w.shape),
                  tab_spec, tab_spec, tab_spec],
        out_specs=out_specs, out_shape=out_shape,
        name="inproj")(h, g.reshape(1, d), w, *tables)


def _s5_disc_kernel(ar_ref, ai_ref, ldt_ref, pre_ref, pim_ref, cc_ref):
    ar = ar_ref[...]
    ai = ai_ref[...]
    dt = jnp.exp(ldt_ref[...])
    row = lax.broadcasted_iota(jnp.int32, (4 * SUBLANES, 1), 0)
    blk = row // SUBLANES
    r = row % SUBLANES
    n = jnp.where(blk == 0, r + 1, jnp.where(blk == 1, 1, jnp.where(blk == 2, 2, 4)))
    keep = (blk == 0) | (r >= n)
    n = n.astype(F32)
    mag = jnp.exp(ar * dt * n)
    ang = ai * dt * n
    pre = mag * jnp.cos(ang)
    pim = mag * jnp.sin(ang)
    pre_ref[...] = jnp.where(keep, pre, 0.0)
    pim_ref[...] = jnp.where(keep, pim, 0.0)
    abr = pre[0:1]
    abi = pim[0:1]
    den = ar * ar + ai * ai
    nr = abr - 1.0
    cc_ref[0:1, :] = (nr * ar + abi * ai) / den
    cc_ref[1:2, :] = (abi * ar - nr * ai) / den


def _s5_discretize(a_re, a_im, log_dt):
    ar = a_re.reshape(1, S5_LANES)
    ai = a_im.reshape(1, S5_LANES)
    ldt = jnp.repeat(log_dt, S5_STATE).reshape(1, S5_LANES)
    return pl.pallas_call(
        _s5_disc_kernel,
        out_shape=(jax.ShapeDtypeStruct((4 * SUBLANES, S5_LANES), F32),
                   jax.ShapeDtypeStruct((4 * SUBLANES, S5_LANES), F32),
                   jax.ShapeDtypeStruct((2, S5_LANES), F32)),
        name="s5_discretize")(ar, ai, ldt)


SCAN_LANES = 512


def _s5_kernel(u_ref, h0r_ref, h0i_ref, pre_ref, pim_ref, cc_ref, wb_ref, wc_ref, d_ref,
               wglu_ref, bglu_ref, y_ref, hr_out_ref, hi_out_ref, xr_s, xi_s, car_s, *, chain):
    tm = u_ref.shape[0]
    nb = tm // SUBLANES
    u = u_ref[...]
    ub = u.astype(BF16)
    half_in = S5_WIDTH // 2
    half_st = S5_LANES // 2
    for hf in range(2):
        bu = _dot(ub[:, hf * half_in:(hf + 1) * half_in], wb_ref[hf])
        bur = bu[:, :half_st]
        bui = bu[:, half_st:]
        ls = slice(hf * half_st, (hf + 1) * half_st)
        cr = cc_ref[0:1, ls]
        ci = cc_ref[1:2, ls]
        xr_s[:, ls] = cr * bur - ci * bui
        xi_s[:, ls] = cr * bui + ci * bur

    if chain:
        t = pl.program_id(1)

        @pl.when(t == 0)
        def _():
            car_s[0:1, :] = h0r_ref[0]
            car_s[1:2, :] = h0i_ref[0]

    for lc in range(S5_LANES // SCAN_LANES):
        ls = slice(lc * SCAN_LANES, (lc + 1) * SCAN_LANES)

        def body(i, carry, ls=ls):
            r0 = pl.multiple_of(i * SUBLANES, SUBLANES)
            hr = xr_s[pl.ds(r0, SUBLANES), ls]
            hi = xi_s[pl.ds(r0, SUBLANES), ls]
            for step, k in enumerate((1, 2, 4)):
                ar = pre_ref[(step + 1) * SUBLANES:(step + 2) * SUBLANES, ls]
                ai = pim_ref[(step + 1) * SUBLANES:(step + 2) * SUBLANES, ls]
                sr = pltpu.roll(hr, k, 0)
                si = pltpu.roll(hi, k, 0)
                hr, hi = hr + ar * sr - ai * si, hi + ar * si + ai * sr
            if chain:
                cr_, ci_ = carry
            else:
                cr_ = h0r_ref[i, :, ls]
                ci_ = h0i_ref[i, :, ls]
            pr8 = pre_ref[0:SUBLANES, ls]
            pi8 = pim_ref[0:SUBLANES, ls]
            hr, hi = hr + pr8 * cr_ - pi8 * ci_, hi + pr8 * ci_ + pi8 * cr_
            xr_s[pl.ds(r0, SUBLANES), ls] = hr
            xi_s[pl.ds(r0, SUBLANES), ls] = hi
            last_r = hr[SUBLANES - 1:SUBLANES]
            last_i = hi[SUBLANES - 1:SUBLANES]
            if chain:
                return last_r, last_i
            hr_out_ref[i, :, ls] = last_r
            hi_out_ref[i, :, ls] = last_i
            return carry

        if chain:
            cr_, ci_ = lax.fori_loop(0, nb, body, (car_s[0:1, ls], car_s[1:2, ls]))
            car_s[0:1, ls] = cr_
            car_s[1:2, ls] = ci_
        else:
            lax.fori_loop(0, nb, body, 0)

    if chain:
        hr_out_ref[0] = car_s[0:1, :]
        hi_out_ref[0] = car_s[1:2, :]

    ys = []
    for hf in range(2):
        ls = slice(hf * half_st, (hf + 1) * half_st)
        hcat = jnp.concatenate([xr_s[:, ls], xi_s[:, ls]], axis=1).astype(BF16)
        ys.append(_dot(hcat, wc_ref[hf]))
    y = jnp.concatenate(ys, axis=1) + d_ref[...] * u
    y = _gelu(y)
    y_ref[...] = y * _sigmoid(_dot(y.astype(BF16), wglu_ref[...]) + bglu_ref[...])


def _s5_weights(b_re, b_im, c_re, c_im):
    gh = S5_GROUPS // 2
    eye = jnp.eye(gh, dtype=F32)

    def bmat(b):
        bt = b.astype(F32).transpose(0, 2, 1).reshape(2, gh, S5_GROUP, S5_STATE)
        return jnp.einsum('xghp,gk->xghkp', bt, eye).reshape(2, gh * S5_GROUP, gh * S5_STATE)

    def cmat(c):
        ct = c.astype(F32).transpose(0, 2, 1).reshape(2, gh, S5_STATE, S5_GROUP)
        return jnp.einsum('xgph,gk->xgpkh', ct, eye).reshape(2, gh * S5_STATE, gh * S5_GROUP)

    wb = jnp.concatenate([bmat(b_re), bmat(b_im)], axis=2).astype(BF16)
    wc = jnp.concatenate([cmat(c_re), -cmat(c_im)], axis=1).astype(BF16)
    return wb, wc


def _s5(u, h0r, h0i, disc, wb, wc, dvec, wglu, bglu, *, bsz, t):
    n = bsz * t
    pre, pim, cc = disc
    chain = t > SUBLANES
    if chain:
        tm = min(MATMUL_TILE, t)
        assert t % tm == 0 and tm % SUBLANES == 0
        grid = (bsz, t // tm)
        row_spec = pl.BlockSpec((tm, S5_WIDTH), lambda b, i: (b * (t // tm) + i, 0))
        st_spec = pl.BlockSpec((1, 1, S5_LANES), lambda b, i: (b, 0, 0))
    else:
        assert t == SUBLANES
        tm = n
        grid = (1, 1)
        row_spec = pl.BlockSpec((tm, S5_WIDTH), lambda b, i: (0, 0))
        st_spec = pl.BlockSpec((bsz, 1, S5_LANES), lambda b, i: (0, 0, 0))
    st_sds = jax.ShapeDtypeStruct((bsz, 1, S5_LANES), F32)
    return pl.pallas_call(
        functools.partial(_s5_kernel, chain=chain),
        grid=grid,
        in_specs=[row_spec, st_spec, st_spec, _const(pre.shape), _const(pim.shape), _const(cc.shape),
                  _const(wb.shape), _const(wc.shape), _const((1, S5_WIDTH)), _const(wglu.shape),
                  _const((1, S5_WIDTH))],
        out_specs=[row_spec, st_spec, st_spec],
        out_shape=[jax.ShapeDtypeStruct((n, S5_WIDTH), F32), st_sds, st_sds],
        scratch_shapes=[pltpu.VMEM((tm, S5_LANES), F32), pltpu.VMEM((tm, S5_LANES), F32),
                        pltpu.VMEM((2, S5_LANES), F32)],
        name="s5_mixer")(u, h0r, h0i, pre, pim, cc, wb, wc, dvec.reshape(1, S5_WIDTH), wglu,
                         bglu.reshape(1, S5_WIDTH))


def _top_blocks(gate, n_past, col):
    g = jnp.where(col < n_past, gate, NEG)
    sel = jnp.zeros(gate.shape, F32)
    colf = col.astype(F32)
    for _ in range(MOBA_TOPK):
        m = jnp.max(g, axis=-1, keepdims=True)
        first = jnp.min(jnp.where(g == m, colf, float(LANES)), axis=-1, keepdims=True)
        pick = colf == first
        sel = jnp.where(pick & (m > 0.5 * NEG), 1.0, sel)
        g = jnp.where(pick, -3e38, g)
    return sel


def _top_blocks_t(gate, n_past):
    rowf = lax.broadcasted_iota(jnp.int32, gate.shape, 0).astype(F32)
    g = jnp.where(rowf < n_past, gate, NEG)
    sel = jnp.zeros(gate.shape, F32)
    for _ in range(MOBA_TOPK):
        m = jnp.max(g, axis=0, keepdims=True)
        first = jnp.min(jnp.where(g == m, rowf, float(LANES)), axis=0, keepdims=True)
        pick = rowf == first
        sel = jnp.where(pick & (m > 0.5 * NEG), 1.0, sel)
        g = jnp.where(pick, -3e38, g)
    return sel


def _moba_prompt_kernel(q_ref, k_ref, v_ref, o_ref, kb_s, vt_s, kmean_s, sel_s, *, nblk):
    bs = MOBA_BLOCK
    npair = MOBA_PAIRS
    width = 2 * bs
    i = pl.program_id(2)

    @pl.when(i == 0)
    def _():
        kmean_s[...] = jnp.zeros(kmean_s.shape, F32)
        for g in range(npair):
            gl = slice(g * LANES, (g + 1) * LANES)
            for j in range(nblk):
                kj = k_ref[j * bs:(j + 1) * bs, gl]
                kb_s[g, j * bs:(j + 1) * bs, :] = kj.astype(BF16)
                kmean_s[g, j:j + 1, :] = jnp.sum(kj, axis=0, keepdims=True) * (1.0 / bs)
                vt_s[g, j] = v_ref[j * bs:(j + 1) * bs, gl].T.astype(BF16)

    lane = lax.broadcasted_iota(jnp.int32, (1, LANES), 1)
    rowd = lax.broadcasted_iota(jnp.int32, (LANES, 1), 0)
    causal = (lax.broadcasted_iota(jnp.int32, (bs, npair * width), 0)
              <= lax.broadcasted_iota(jnp.int32, (bs, npair * width), 1) % bs)
    r0 = pl.multiple_of(i * bs, bs)
    qs2 = []
    gates = []
    for g in range(npair):
        qf = q_ref[:, g * LANES:(g + 1) * LANES]
        qh2 = jnp.concatenate(
            [jnp.where((lane >= HEAD_DIM * hd) & (lane < HEAD_DIM * (hd + 1)), qf, 0.0) for hd in range(2)],
            axis=0)
        qs2.append((qh2 * (HEAD_DIM ** -0.5 * LOG2E)).astype(BF16))
        gates.append(_dot_nt(kmean_s[g].astype(BF16), qh2.astype(BF16)))
    sel = _top_blocks_t(jnp.concatenate(gates, axis=1), i)
    for b in range(nblk):
        sel_s[b] = sel[b:b + 1, :]

    def scores(rows):
        return jnp.concatenate([_dot_nt(kb_s[g, rows, :], qs2[g]) for g in range(npair)], axis=1)

    def weighted(blk, p):
        return jnp.concatenate([_dot(vt_s[g, blk], p[:, g * width:(g + 1) * width]) for g in range(npair)],
                               axis=1)

    s = jnp.where(causal, scores(pl.ds(r0, bs)), NEG)
    m = jnp.max(s, axis=0, keepdims=True)
    p = jnp.exp2(s - m)
    carry = (m, jnp.sum(p, axis=0, keepdims=True), weighted(i, p.astype(BF16)))

    def kvpair(jj, carry):
        m, l, acc = carry
        s = scores(pl.ds(pl.multiple_of(jj * 2 * bs, 2 * bs), 2 * bs))
        s0 = jnp.where(sel_s[2 * jj] > 0.5, s[:bs], NEG)
        s1 = jnp.where(sel_s[2 * jj + 1] > 0.5, s[bs:], NEG)
        m_new = jnp.maximum(m, jnp.maximum(jnp.max(s0, axis=0, keepdims=True),
                                           jnp.max(s1, axis=0, keepdims=True)))
        alpha = jnp.exp2(m - m_new)
        p0 = jnp.exp2(s0 - m_new)
        p1 = jnp.exp2(s1 - m_new)
        l = alpha * l + jnp.sum(p0, axis=0, keepdims=True) + jnp.sum(p1, axis=0, keepdims=True)
        acc = alpha * acc + weighted(2 * jj, p0.astype(BF16)) + weighted(2 * jj + 1, p1.astype(BF16))
        return m_new, l, acc

    m, l, acc = lax.fori_loop(0, (i + 1) // 2, kvpair, carry)
    out = acc / l
    for g in range(npair):
        og = out[:, g * width:(g + 1) * width]
        o_ref[:, g * LANES:(g + 1) * LANES] = jnp.where(rowd < HEAD_DIM, og[:, :bs], og[:, bs:]).T


def _moba_prompt(q, k, v, *, bsz, t):
    nblk = t // MOBA_BLOCK
    assert t % MOBA_BLOCK == 0 and nblk % 2 == 0 and nblk <= LANES
    nblk_rows = -(-nblk // SUBLANES) * SUBLANES
    n = bsz * t
    gw = MOBA_PAIRS * LANES
    q_spec = pl.BlockSpec((MOBA_BLOCK, gw), lambda b, hp, i: (b * nblk + i, hp))
    kv_spec = pl.BlockSpec((t, gw), lambda b, hp, i: (b, hp), pipeline_mode=pl.Buffered(1))
    return pl.pallas_call(
        functools.partial(_moba_prompt_kernel, nblk=nblk),
        grid=(bsz, ATT_WIDTH // gw, nblk),
        in_specs=[q_spec, kv_spec, kv_spec], out_specs=q_spec,
        out_shape=jax.ShapeDtypeStruct((n, ATT_WIDTH), F32),
        scratch_shapes=[pltpu.VMEM((MOBA_PAIRS, t, LANES), BF16),
                        pltpu.VMEM((MOBA_PAIRS, nblk, LANES, MOBA_BLOCK), BF16),
                        pltpu.VMEM((MOBA_PAIRS, nblk_rows, LANES), F32),
                        pltpu.VMEM((nblk, 1, MOBA_PAIRS * 2 * MOBA_BLOCK), F32)],
        name="moba_prompt")(q, k, v)


def _expand_heads(q, qexp_s):
    t = q.shape[0]
    lane = lax.broadcasted_iota(jnp.int32, (1, ATT_WIDTH), 1)
    for h in range(HEADS):
        lm = (lane >= h * HEAD_DIM) & (lane < (h + 1) * HEAD_DIM)
        qexp_s[h * t:(h + 1) * t, :] = jnp.where(lm, q, 0.0)


def _collapse_heads(acc, t):
    lane = lax.broadcasted_iota(jnp.int32, (1, ATT_WIDTH), 1)
    out = jnp.zeros((t, ATT_WIDTH), F32)
    for h in range(HEADS):
        lm = (lane >= h * HEAD_DIM) & (lane < (h + 1) * HEAD_DIM)
        out = out + jnp.where(lm, acc[h * t:(h + 1) * t, :], 0.0)
    return out


def _pad_rows(x_ref, new_s):
    new_s[...] = jnp.zeros(new_s.shape, F32)
    new_s[0:x_ref.shape[0], :] = x_ref[...]
    return new_s[...].astype(BF16)


def _moba_sample_kernel(pt_ref, q_ref, kn_ref, vn_ref, *rest, n_pages, t):
    pps = PAGES_PER_STEP
    k_refs = rest[:pps]
    v_refs = rest[pps:2 * pps]
    o_ref = rest[2 * pps]
    qexp_s, new_s, s_all, gate_s, bmax_s, sel_s, m_s, l_s, acc_s = rest[2 * pps + 1:]
    ns = n_pages // pps
    s_id = pl.program_id(1)
    rows = HEADS * t
    col = lax.broadcasted_iota(jnp.int32, (rows, LANES), 1)
    scale = HEAD_DIM ** -0.5
    pages_per_block = MOBA_BLOCK // PAGE

    @pl.when(s_id == 0)
    def _():
        _expand_heads(q_ref[...], qexp_s)
        gate_s[...] = jnp.zeros(gate_s.shape, F32)
        bmax_s[...] = jnp.full(bmax_s.shape, NEG, F32)

    @pl.when(s_id < ns)
    def _():
        qe = qexp_s[...].astype(BF16)
        g = gate_s[...]
        bm = bmax_s[...]
        for i in range(pps):
            pg = s_id * pps + i
            s = _dot(qe, k_refs[i][...].astype(BF16))
            s_all[pg] = s
            mine = col == pg // pages_per_block
            g = g + jnp.where(mine, jnp.sum(s, axis=-1, keepdims=True) * (1.0 / MOBA_BLOCK), 0.0)
            bm = jnp.where(mine, jnp.maximum(bm, jnp.max(s, axis=-1, keepdims=True)), bm)
        gate_s[...] = g
        bmax_s[...] = bm

    @pl.when(s_id == ns)
    def _():
        sel = _top_blocks(gate_s[...], n_pages // pages_per_block, col)
        sel_s[...] = sel
        qe = qexp_s[...].astype(BF16)
        sn = _dot_nt(qe, _pad_rows(kn_ref, new_s)) * scale
        rowt = lax.broadcasted_iota(jnp.int32, (rows, LANES), 0) % t
        sn = jnp.where(col <= rowt, sn, NEG)
        m = jnp.maximum(jnp.max(sn, axis=-1, keepdims=True),
                        jnp.max(jnp.where(sel > 0.5, bmax_s[...] * scale, NEG), axis=-1, keepdims=True))
        pn = jnp.exp(sn - m)
        m_s[...] = jnp.broadcast_to(m, m_s.shape)
        l_s[...] = jnp.broadcast_to(jnp.sum(pn, axis=-1, keepdims=True), l_s.shape)
        acc_s[...] = _dot(pn.astype(BF16), _pad_rows(vn_ref, new_s))

    @pl.when(s_id >= ns)
    def _():
        sel = sel_s[...]
        m = m_s[:, 0:1]
        l = l_s[:, 0:1]
        acc = acc_s[...]
        for i in range(pps):
            pg = (s_id - ns) * pps + i
            selc = jnp.sum(jnp.where(col == pg // pages_per_block, sel, 0.0), axis=-1, keepdims=True)
            p = jnp.exp(jnp.where(selc > 0.5, s_all[pg] * scale, NEG) - m)
            l = l + jnp.sum(p, axis=-1, keepdims=True)
            acc = acc + _dot_nt(p.astype(BF16), v_refs[i][...].astype(BF16))
        l_s[...] = jnp.broadcast_to(l, l_s.shape)
        acc_s[...] = acc

    @pl.when(s_id == 2 * ns - 1)
    def _():
        o_ref[...] = _collapse_heads(acc_s[...] / l_s[:, 0:1], t)


def _sample_specs(page_table, n_pages, t, page_index, pps=PAGES_PER_STEP):
    row_spec = pl.BlockSpec((t, ATT_WIDTH), lambda b, s, pt: (b, 0))

    def page_spec(i, which):
        return pl.BlockSpec((None, ATT_WIDTH, PAGE),
                            lambda b, s, pt: (pt[b, page_index(s, i, which)], 0, 0))

    in_specs = ([row_spec] * 3 + [page_spec(i, 0) for i in range(pps)]
                + [page_spec(i, 1) for i in range(pps)])
    return in_specs, row_spec


def _moba_sample(q, k_new, v_new, k_pool, v_pool, page_table, *, bsz, t):
    n_pages = page_table.shape[1]
    pps = PAGES_PER_STEP
    assert n_pages % pps == 0 and (n_pages * PAGE) % MOBA_BLOCK == 0 and t <= SUBLANES
    assert n_pages * PAGE // MOBA_BLOCK <= LANES
    ns = n_pages // pps
    rows = HEADS * t

    def page_index(s, i, which):
        step = jnp.minimum(s, ns - 1) if which == 0 else jnp.maximum(s - ns, 0)
        return step * pps + i

    in_specs, out_spec = _sample_specs(page_table, n_pages, t, page_index)
    grid_spec = pltpu.PrefetchScalarGridSpec(
        num_scalar_prefetch=1, grid=(bsz, 2 * ns), in_specs=in_specs, out_specs=out_spec,
        scratch_shapes=[pltpu.VMEM((rows, ATT_WIDTH), F32),
                        pltpu.VMEM((PAGE, ATT_WIDTH), F32),
                        pltpu.VMEM((n_pages, rows, PAGE), F32)]
                       + [pltpu.VMEM((rows, LANES), F32)] * 5
                       + [pltpu.VMEM((rows, ATT_WIDTH), F32)])
    return pl.pallas_call(
        functools.partial(_moba_sample_kernel, n_pages=n_pages, t=t),
        grid_spec=grid_spec,
        out_shape=jax.ShapeDtypeStruct((bsz * t, ATT_WIDTH), F32),
        name="moba_sample")(page_table, q, k_new, v_new, *([k_pool] * pps), *([v_pool] * pps))


def _sb_weights(z, r, tri, strict):
    n = z.shape[0]
    lg = jnp.log2(1.0 + jnp.exp2(-jnp.abs(z)))
    log_beta = jnp.minimum(z, 0.0) - lg
    log_keep = jnp.minimum(-z, 0.0) - lg
    if strict is not None:
        log_beta = jnp.where(strict, log_beta, NEG)
        log_keep = jnp.where(strict, log_keep, 0.0)
    st = _dot(jnp.concatenate(_split3(log_keep), axis=0), tri)
    later = st[:n] + st[n:2 * n] + st[2 * n:]
    w = jnp.exp2(log_beta + later + r)
    return w.astype(BF16), r + jnp.sum(log_keep, axis=-1, keepdims=True)


def _suffix_matrix(n):
    return (lax.broadcasted_iota(jnp.int32, (n, n), 0)
            > lax.broadcasted_iota(jnp.int32, (n, n), 1)).astype(BF16)


def _sb_prompt_kernel(q_ref, k_ref, v_ref, o_ref, kb_s, vb_s):
    bs = SB_BLOCK
    i = pl.program_id(2)

    @pl.when(i == 0)
    def _():
        for j in range(k_ref.shape[0] // bs):
            kb_s[j * bs:(j + 1) * bs, :] = k_ref[j * bs:(j + 1) * bs, :].astype(BF16)
            vb_s[j * bs:(j + 1) * bs, :] = v_ref[j * bs:(j + 1) * bs, :].astype(BF16)

    lane = lax.broadcasted_iota(jnp.int32, (1, LANES), 1)
    tri = _suffix_matrix(bs)
    strict = (lax.broadcasted_iota(jnp.int32, (2 * bs, bs), 1)
              < lax.broadcasted_iota(jnp.int32, (2 * bs, bs), 0) % bs)
    qf = q_ref[...]
    qhs = []
    for hd in range(2):
        lm = (lane >= HEAD_DIM * hd) & (lane < HEAD_DIM * (hd + 1))
        qhs.append(jnp.where(lm, qf, 0.0))
    qs2 = (jnp.concatenate(qhs, axis=0) * (HEAD_DIM ** -0.5 * LOG2E)).astype(BF16)

    def block(j, r, mask):
        c0 = pl.multiple_of(j * bs, bs)
        w, r = _sb_weights(_dot_nt(qs2, kb_s[pl.ds(c0, bs), :]), r, tri, mask)
        return _dot(w, vb_s[pl.ds(c0, bs), :]), r

    acc, r = block(i, jnp.zeros((2 * bs, 1), F32), strict)
    has_prev = lax.broadcasted_iota(jnp.int32, (2 * bs, bs), 0) < jnp.where(i > 0, 2 * bs, 0)
    d, r = block(jnp.maximum(i - 1, 0), r, has_prev)
    acc = acc + d

    def cond(c):
        j, r, _ = c
        return (j >= 0) & (jnp.max(r) > EXP2_UNDERFLOW)

    def body(c):
        j, r, acc = c
        d, r = block(j, r, None)
        return j - 1, r, acc + d

    _, _, acc = lax.while_loop(cond, body, (i - 2, r, acc))
    o_ref[...] = jnp.where(lane < HEAD_DIM, acc[:bs], acc[bs:])


def _sb_prompt(q, k, v, *, bsz, t):
    assert t % SB_BLOCK == 0
    n = bsz * t
    nq = t // SB_BLOCK
    q_spec = pl.BlockSpec((SB_BLOCK, LANES), lambda b, hp, i: (b * nq + i, hp))
    kv_spec = pl.BlockSpec((t, LANES), lambda b, hp, i: (b, hp))
    return pl.pallas_call(
        _sb_prompt_kernel,
        grid=(bsz, ATT_WIDTH // LANES, nq),
        in_specs=[q_spec, kv_spec, kv_spec], out_specs=q_spec,
        out_shape=jax.ShapeDtypeStruct((n, ATT_WIDTH), F32),
        scratch_shapes=[pltpu.VMEM((t, LANES), BF16), pltpu.VMEM((t, LANES), BF16)],
        name="sb_prompt")(q, k, v)


def _sb_sample_kernel(pt_ref, q_ref, a_ref, b_ref, *rest, t, first, pps):
    k_refs = rest[:pps]
    v_refs = rest[pps:2 * pps]
    n_out = 3 if first else 1
    o_ref = rest[2 * pps]
    qexp_s, new_s, r_s, acc_s = rest[2 * pps + n_out:]
    s_id = pl.program_id(1)
    rows = HEADS * t
    tri = _suffix_matrix(PAGE)

    @pl.when(s_id == 0)
    def _():
        _expand_heads(q_ref[...] * (HEAD_DIM ** -0.5 * LOG2E), qexp_s)
        if first:
            z = _dot_nt(qexp_s[...].astype(BF16), _pad_rows(a_ref, new_s))
            col = lax.broadcasted_iota(jnp.int32, (rows, PAGE), 1)
            rowt = lax.broadcasted_iota(jnp.int32, (rows, PAGE), 0) % t
            w, r = _sb_weights(z, jnp.zeros((rows, 1), F32), tri, col < rowt)
            acc_s[...] = _dot(w, _pad_rows(b_ref, new_s))
            r_s[...] = jnp.broadcast_to(r, r_s.shape)
        else:
            acc_s[...] = a_ref[...]
            r_s[...] = b_ref[...]

    for i in range(pps):
        @pl.when(jnp.max(r_s[...]) > EXP2_UNDERFLOW)
        def _(i=i):
            z = _dot(qexp_s[...].astype(BF16), k_refs[i][...].astype(BF16))
            w, r = _sb_weights(z, r_s[:, 0:1], tri, None)
            acc_s[...] = acc_s[...] + _dot_nt(w, v_refs[i][...].astype(BF16))
            r_s[...] = jnp.broadcast_to(r, r_s.shape)

    @pl.when(s_id == pl.num_programs(1) - 1)
    def _():
        o_ref[...] = _collapse_heads(acc_s[...], t)
        if first:
            rest[2 * pps + 1][...] = acc_s[...]
            rest[2 * pps + 2][...] = r_s[...]


def _sb_sample(q, k_new, v_new, k_pool, v_pool, page_table, *, bsz, t):
    n_pages = page_table.shape[1]
    assert t <= SUBLANES
    first = min(SB_FIRST_PAGES, n_pages)
    rows = HEADS * t
    n = bsz * t
    scratch = [pltpu.VMEM((rows, ATT_WIDTH), F32), pltpu.VMEM((PAGE, ATT_WIDTH), F32),
               pltpu.VMEM((rows, LANES), F32), pltpu.VMEM((rows, ATT_WIDTH), F32)]
    acc_spec = pl.BlockSpec((rows, ATT_WIDTH), lambda b, s, pt: (b, 0))
    r_spec = pl.BlockSpec((rows, LANES), lambda b, s, pt: (b, 0))
    y_sds = jax.ShapeDtypeStruct((n, ATT_WIDTH), F32)

    in_specs, y_spec = _sample_specs(page_table, n_pages, t, lambda s, i, which: n_pages - 1 - i, first)
    y, acc, r = pl.pallas_call(
        functools.partial(_sb_sample_kernel, t=t, first=True, pps=first),
        grid_spec=pltpu.PrefetchScalarGridSpec(
            num_scalar_prefetch=1, grid=(bsz, 1), in_specs=in_specs,
            out_specs=[y_spec, acc_spec, r_spec], scratch_shapes=scratch),
        out_shape=[y_sds, jax.ShapeDtypeStruct((bsz * rows, ATT_WIDTH), F32),
                   jax.ShapeDtypeStruct((bsz * rows, LANES), F32)],
        name="sb_sample")(page_table, q, k_new, v_new, *([k_pool] * first), *([v_pool] * first))
    older = n_pages - first
    if older == 0:
        return y
    pps = max(d for d in range(1, PAGES_PER_STEP + 1) if older % d == 0)

    def older_pages():
        specs, _ = _sample_specs(page_table, n_pages, t,
                                 lambda s, i, which: n_pages - 1 - first - (s * pps + i), pps)
        specs = [specs[0], acc_spec, r_spec] + specs[3:]
        return pl.pallas_call(
            functools.partial(_sb_sample_kernel, t=t, first=False, pps=pps),
            grid_spec=pltpu.PrefetchScalarGridSpec(
                num_scalar_prefetch=1, grid=(bsz, older // pps), in_specs=specs,
                out_specs=y_spec, scratch_shapes=scratch),
            out_shape=y_sds,
            name="sb_sample_older")(page_table, q, acc, r, *([k_pool] * pps), *([v_pool] * pps))

    return lax.cond(jnp.max(r) > EXP2_UNDERFLOW, older_pages, lambda: y)


def _hgrn_kernel(q_ref, f_ref, i_ref, g_ref, lb_ref, s0_ref, gn_ref, o_ref, s_out_ref, st_s,
                 *, layer):
    tb = q_ref.shape[0]
    t = pl.program_id(1)

    @pl.when(t == 0)
    def _():
        for hh in range(HGRN_HEADS):
            st_s[hh] = s0_ref[0, hh].T

    lbp = lb_ref[...]
    e = jnp.exp(lbp - jnp.max(lbp, axis=0, keepdims=True))
    soft = e / jnp.sum(e, axis=0, keepdims=True)
    lbv = jnp.sum(soft[1:layer + 1], axis=0, keepdims=True)

    pad = max(HGRN_SUB - tb, 0)
    L = min(HGRN_CHUNK, tb + pad)
    c = min(HGRN_SUB, L)
    tri = (lax.broadcasted_iota(jnp.int32, (L, L), 0)
           >= lax.broadcasted_iota(jnp.int32, (L, L), 1)).astype(BF16)
    rowc = lax.broadcasted_iota(jnp.int32, (c, 1), 0)
    gn = gn_ref[...]

    def padrows(x):
        if pad == 0:
            return x
        return jnp.concatenate([x, jnp.zeros((pad, x.shape[1]), F32)], axis=0)

    def chunk(rows, n_valid):
        for hh in range(HGRN_HEADS):
            ls = slice(hh * HGRN_DIM, (hh + 1) * HGRN_DIM)
            lbh = lbv[:, ls]
            f = lbh + (1.0 - lbh) * _sigmoid(f_ref[rows, ls])
            lf = padrows(jnp.log(f))
            kk = padrows(1.0 - f)
            qv = padrows(q_ref[rows, ls])
            vv = padrows(i_ref[rows, ls])
            hi, mid, lo = _split3(lf)
            b = _dot(tri, hi) + _dot(tri, mid) + _dot(tri, lo)
            st = st_s[hh]
            o_inter = _dot_nt((qv * jnp.exp(b)).astype(BF16), st.astype(BF16))
            vb = vv.astype(BF16)
            parts = []
            for si in range(L // c):
                rs = slice(si * c, (si + 1) * c)
                b_i = b[rs]
                q_i = qv[rs]
                k_i = kk[rs]
                v_i = vv[rs]
                o_i = o_inter[rs]
                if si > 0:
                    b_prev = b[si * c - 1:si * c]
                    qt = (q_i * jnp.exp(b_i - b_prev)).astype(BF16)
                    kt = (kk[:si * c] * jnp.exp(b_prev - b[:si * c])).astype(BF16)
                    o_i = o_i + _dot(_dot_nt(qt, kt).astype(BF16), vb[:si * c])
                for s in range(min(c, max(n_valid - si * c, 0))):
                    dec = jnp.exp(jnp.minimum(b_i - b_i[s:s + 1], 0.0))
                    a = jnp.sum(q_i * k_i[s:s + 1] * dec, axis=-1, keepdims=True)
                    o_i = o_i + jnp.where(rowc >= s, a, 0.0) * v_i[s:s + 1]
                parts.append(o_i)
            o = jnp.concatenate(parts, axis=0) if len(parts) > 1 else parts[0]
            o = o[:n_valid]
            o = o * lax.rsqrt(jnp.mean(o * o, axis=-1, keepdims=True) + EPS) * gn
            gv = g_ref[rows, ls]
            o_ref[rows, ls] = o * (gv * _sigmoid(gv))
            b_last = b[L - 1:L]
            kdec = (kk * jnp.exp(b_last - b)).astype(BF16)
            st_s[hh] = st * jnp.exp(b_last) + _dot_tn(vb, kdec)

    if pad:
        chunk(slice(0, tb), tb)
    else:
        def body(ch, _):
            chunk(pl.ds(pl.multiple_of(ch * L, L), L), L)
            return 0

        lax.fori_loop(0, tb // L, body, 0, unroll=True)

    for hh in range(HGRN_HEADS):
        s_out_ref[0, hh] = st_s[hh].T


def _hgrn(qh, fh, ih, gh, lb, s0, gnorm, *, bsz, t, layer):
    n = bsz * t
    tb = min(MATMUL_TILE, t)
    assert t % tb == 0 and (tb % HGRN_CHUNK == 0 or (tb == t and t <= HGRN_SUB))
    nt = t // tb
    row_spec = pl.BlockSpec((tb, SLAB), lambda b, i: (b * nt + i, 0))
    st_spec = pl.BlockSpec((1, HGRN_HEADS, HGRN_DIM, HGRN_DIM), lambda b, i: (b, 0, 0, 0))
    return pl.pallas_call(
        functools.partial(_hgrn_kernel, layer=layer),
        grid=(bsz, nt),
        in_specs=[row_spec] * 4 + [_const(lb.shape), st_spec, _const((1, HGRN_DIM))],
        out_specs=[row_spec, st_spec],
        out_shape=[jax.ShapeDtypeStruct((n, SLAB), F32),
                   jax.ShapeDtypeStruct((bsz, HGRN_HEADS, HGRN_DIM, HGRN_DIM), F32)],
        scratch_shapes=[pltpu.VMEM((HGRN_HEADS, HGRN_DIM, HGRN_DIM), F32)],
        name="hgrn2")(qh, fh, ih, gh, lb, s0, gnorm.reshape(1, HGRN_DIM))


def _tail_kernel(*refs, chain):
    if chain:
        (h_ref, ya_ref, yb_ref, p_ref, wo_ref, gmp_ref, gfp_ref, wup_ref, wcv_ref, wdn_ref,
         gfo_ref, wpp_ref, wpg_ref, gpl_ref, ho_ref, ffn_ref, u_s, h1_s, a_s, act_s, car_s) = refs
    else:
        (h_ref, ya_ref, yb_ref, p_ref, p1_ref, p2_ref, wo_ref, gmp_ref, gfp_ref, wup_ref,
         wcv_ref, wdn_ref, gfo_ref, wpp_ref, wpg_ref, gpl_ref, ho_ref, ffn_ref, u_s, h1_s, a_s, act_s) = refs
    tm = h_ref.shape[0]
    nch = N_FF_CHUNKS
    cw = FF_CHUNK
    mix = _dot(ya_ref[...].astype(BF16), wo_ref[0]) + _dot(yb_ref[...].astype(BF16), wo_ref[1])
    h1 = h_ref[...] + _rms(mix, gmp_ref[...])
    h1_s[...] = h1
    a_s[...] = _rms(h1, gfp_ref[...]).astype(BF16)
    rowi = lax.broadcasted_iota(jnp.int32, (tm, cw), 0)

    if chain:
        @pl.when(pl.program_id(1) == 0)
        def _():
            car_s[...] = jnp.zeros(car_s.shape, F32)

    def taps(w, p2, p1, u):
        return w[3:4] + w[0:1] * p2 + w[1:2] * p1 + w[2:3] * u

    def up(c):
        a = a_s[...]
        for half in range(2):
            cols = slice(half * D_FF + c * cw, half * D_FF + (c + 1) * cw)
            u_s[c % 2, half, SUBLANES:, :] = _dot(a, wup_ref[:, cols])

    def down(c):
        slot = c % 2
        ws = [wcv_ref[:, half * D_FF + c * cw:half * D_FF + (c + 1) * cw] for half in range(2)]
        if chain:
            for half in range(2):
                u_s[slot, half, 0:SUBLANES, :] = car_s[half * nch + c]
            for rb in range(tm // CONV_ROWS):
                cs = []
                for half in range(2):
                    x = u_s[slot, half, rb * CONV_ROWS:(rb + 1) * CONV_ROWS + SUBLANES, :]
                    cs.append(taps(ws[half], pltpu.roll(x, 2, 0)[SUBLANES:],
                                   pltpu.roll(x, 1, 0)[SUBLANES:], x[SUBLANES:]))
                act_s[rb * CONV_ROWS:(rb + 1) * CONV_ROWS, c * cw:(c + 1) * cw] = (
                    _gelu(cs[0]) * cs[1]).astype(BF16)
            for half in range(2):
                last = u_s[slot, half, tm:tm + SUBLANES, :]
                car_s[half * nch + c] = last
                ffn_ref[0, half * nch + c] = last
        else:
            rm = rowi % SUBLANES
            cs = []
            for half in range(2):
                u = u_s[slot, half, SUBLANES:, :]
                p1 = jnp.where(rm == 0, p1_ref[half * nch + c], pltpu.roll(u, 1, 0))
                p2 = jnp.where(rm < 2, p2_ref[half * nch + c], pltpu.roll(u, 2, 0))
                cs.append(taps(ws[half], p2, p1, u))
                ffn_ref[half * nch + c] = u
            act_s[:, c * cw:(c + 1) * cw] = (_gelu(cs[0]) * cs[1]).astype(BF16)

    up(0)
    for c in range(nch):
        if c + 1 < nch:
            up(c + 1)
        down(c)
    h2 = h1_s[...] + _rms(_dot(act_s[...], wdn_ref[...]), gfo_ref[...])
    ple = _dot(p_ref[...].astype(BF16), wpp_ref[...]) * _sigmoid(_dot(h2.astype(BF16), wpg_ref[...]))
    ho_ref[...] = h2 + _rms(ple, gpl_ref[...])


def _tail(h, ya, yb, p, ffn0, w, *, bsz, t):
    n, d = h.shape
    nch = N_FF_CHUNKS
    cw = FF_CHUNK
    chain = t > SUBLANES
    consts = [w['wo'], w['gmp'], w['gfp'], w['wup'], w['wcv'], w['wdn'], w['gfo'], w['wpp'], w['wpg'],
              w['gpl']]
    const_specs = [_const(x.shape) for x in consts]
    if chain:
        assert ffn0 is None, "a long sequence starts from an empty ConvFFN buffer"
        tm = min(MATMUL_TILE, t)
        assert t % tm == 0 and tm % CONV_ROWS == 0
        nt = t // tm
        grid = (bsz, nt)
        rows = lambda width: pl.BlockSpec((tm, width), lambda b, i: (b * nt + i, 0))
        ffn_spec = pl.BlockSpec((1, 2 * nch, SUBLANES, cw), lambda b, i: (b, 0, 0, 0))
        ffn_sds = jax.ShapeDtypeStruct((bsz, 2 * nch, SUBLANES, cw), F32)
        extra, extra_specs = [], []
        scratch = [pltpu.VMEM((2, 2, tm + SUBLANES, cw), F32),
                   pltpu.VMEM((tm, d), F32), pltpu.VMEM((tm, d), BF16), pltpu.VMEM((tm, D_FF), BF16),
                   pltpu.VMEM((2 * nch, SUBLANES, cw), F32)]
    else:
        assert t == SUBLANES
        tm = n
        grid = (1, 1)
        rows = lambda width: pl.BlockSpec((tm, width), lambda b, i: (0, 0))
        ffn_spec = _full((2 * nch, tm, cw))
        ffn_sds = jax.ShapeDtypeStruct((2 * nch, tm, cw), F32)
        buf = ffn0.astype(F32).reshape(bsz, 2, 2 * nch, cw).transpose(2, 0, 1, 3)
        zero = jnp.zeros((2 * nch, bsz, SUBLANES - 2, cw), F32)
        p1 = jnp.concatenate([buf[:, :, 1:2], zero, zero[:, :, :1]], axis=2).reshape(2 * nch, tm, cw)
        p2 = jnp.concatenate([buf, zero], axis=2).reshape(2 * nch, tm, cw)
        extra = [p1, p2]
        extra_specs = [_full(p1.shape), _full(p2.shape)]
        scratch = [pltpu.VMEM((2, 2, tm + SUBLANES, cw), F32),
                   pltpu.VMEM((tm, d), F32), pltpu.VMEM((tm, d), BF16), pltpu.VMEM((tm, D_FF), BF16)]
    ho, ffn = pl.pallas_call(
        functools.partial(_tail_kernel, chain=chain),
        grid=grid,
        in_specs=[rows(d), rows(SLAB), rows(SLAB), rows(p.shape[1])] + extra_specs + const_specs,
        out_specs=[rows(d), ffn_spec],
        out_shape=[jax.ShapeDtypeStruct((n, d), F32), ffn_sds],
        scratch_shapes=scratch,
        name="layer_tail")(h, ya, yb, p, *extra, *consts)
    if chain:
        st = ffn[:, :, SUBLANES - 2:, :]
    else:
        st = ffn.reshape(2 * nch, bsz, SUBLANES, cw)[:, :, SUBLANES - 2:, :].transpose(1, 0, 2, 3)
    return ho, st.transpose(0, 2, 1, 3).reshape(bsz, 2, 2 * nch * cw)


def _tail_weights(i, w_out, g_mix_post, g_ffn_pre, g_ffn_post, w_ffn_up, w_ffn_conv, b_ffn_conv,
                  w_ffn_down, w_ple_proj, w_ple_gate, g_ple):
    d = w_out.shape[1]
    taps = jnp.concatenate([w_ffn_conv[i].astype(F32), b_ffn_conv[i].astype(F32)[None]], axis=0)
    row = lambda g: g.astype(F32).reshape(1, d)
    return dict(
        wo=w_out.astype(BF16).reshape(2, SLAB, d), gmp=row(g_mix_post[i]), gfp=row(g_ffn_pre[i]),
        wup=w_ffn_up[i].astype(BF16), wcv=taps, wdn=w_ffn_down[i].astype(BF16),
        gfo=row(g_ffn_post[i]), wpp=w_ple_proj[i].astype(BF16), wpg=w_ple_gate[i].astype(BF16),
        gpl=row(g_ple[i]))


def kernel(x_prompt, x_sample, p_prompt, p_sample, cache_moba_k, cache_moba_v, state_s5_re, state_s5_im,
           state_hgrn, cache_sb_k, cache_sb_v, state_ffn, page_table,
           g_mix_pre, g_mix_post, g_ffn_pre, g_ffn_post, w_ffn_up, w_ffn_conv, b_ffn_conv, w_ffn_down,
           w_ple_proj, w_ple_gate, g_ple,
           w_in_a, w_out_a, s5_a_re, s5_a_im, s5_log_dt, s5_b_re, s5_b_im, s5_c_re, s5_c_im, s5_d,
           s5_w_glu, s5_b_glu,
           w_in_c, w_out_c, hgrn_lb, g_hgrn_norm):
    depth = g_mix_pre.shape[0]
    d_model = x_prompt.shape[-1]
    n_pages = page_table.shape[1]
    past_len = n_pages * cache_moba_k.shape[2]
    page_table = page_table.astype(jnp.int32)

    layers = []
    for i in range(depth):
        j = i // 2
        lw = {}
        if i % 2 == 0:
            lw['w_in'] = w_in_a[j].astype(BF16)
            lw['disc'] = _s5_discretize(s5_a_re[j], s5_a_im[j], s5_log_dt[j])
            lw['wb'], lw['wc'] = _s5_weights(s5_b_re[j], s5_b_im[j], s5_c_re[j], s5_c_im[j])
            lw['d'] = s5_d[j].astype(F32).reshape(S5_WIDTH)
            lw['wglu'] = s5_w_glu[j].astype(BF16)
            lw['bglu'] = s5_b_glu[j].astype(F32)
            w_out = w_out_a[j]
        else:
            lw['w_in'] = w_in_c[j].astype(BF16)
            w_out = w_out_c[j]
        lw['tail'] = _tail_weights(i, w_out, g_mix_post, g_ffn_pre, g_ffn_post, w_ffn_up, w_ffn_conv,
                                   b_ffn_conv, w_ffn_down, w_ple_proj, w_ple_gate, g_ple)
        layers.append(lw)

    def pool2d(pool, j):
        return pool[j].transpose(0, 2, 3, 1).reshape(pool.shape[1], ATT_WIDTH, pool.shape[2])

    def run(x, p, q_start, s5_re0, s5_im0, hgrn0, ffn0, has_past):
        bsz, t, _ = x.shape
        n = bsz * t
        h = x.astype(F32).reshape(n, d_model)
        tables = _rope_tables(t, q_start)
        if t < MATMUL_TILE:
            tables = tuple(jnp.tile(tb, (n // t, 1)) for tb in tables)
        mk, mv, sr, si, hs, sk, sv, fb = [], [], [], [], [], [], [], []
        long_seq = t % MATMUL_TILE == 0

        def kv_out(x, x_t, which):
            if x_t:
                return x_t[which].reshape(bsz, HEADS, HEAD_DIM, t).transpose(0, 3, 1, 2)
            return x.reshape(bsz, t, HEADS, HEAD_DIM)

        for i in range(depth):
            j = i // 2
            lw = layers[i]
            g_pre = g_mix_pre[i].astype(F32)
            if i % 2 == 0:
                u, q, k, v, *kv_t = _inproj(h, g_pre, lw['w_in'], tables, rope_slabs=(1, 2),
                                            t_slabs=(2, 3) if long_seq else (), seq=(bsz, t))
                y_a, hr, hi = _s5(u, s5_re0[j].astype(F32).reshape(bsz, 1, S5_LANES),
                                  s5_im0[j].astype(F32).reshape(bsz, 1, S5_LANES), lw['disc'], lw['wb'],
                                  lw['wc'], lw['d'], lw['wglu'], lw['bglu'], bsz=bsz, t=t)
                if has_past:
                    y_b = _moba_sample(q, k, v, pool2d(cache_moba_k, j), pool2d(cache_moba_v, j),
                                       page_table, bsz=bsz, t=t)
                else:
                    y_b = _moba_prompt(q, k, v, bsz=bsz, t=t)
                mk.append(kv_out(k, kv_t, 0))
                mv.append(kv_out(v, kv_t, 1))
                sr.append(hr.reshape(bsz, S5_GROUPS, S5_STATE))
                si.append(hi.reshape(bsz, S5_GROUPS, S5_STATE))
            else:
                qh, fh, ih, gh, q, k, v, *kv_t = _inproj(h, g_pre, lw['w_in'], tables, rope_slabs=(),
                                                         t_slabs=(5, 6) if long_seq else (), seq=(bsz, t))
                y_a, s_fin = _hgrn(qh, fh, ih, gh, hgrn_lb.astype(F32), hgrn0[j].astype(F32),
                                   g_hgrn_norm[j].astype(F32), bsz=bsz, t=t, layer=i)
                if has_past:
                    y_b = _sb_sample(q, k, v, pool2d(cache_sb_k, j), pool2d(cache_sb_v, j), page_table,
                                     bsz=bsz, t=t)
                else:
                    y_b = _sb_prompt(q, k, v, bsz=bsz, t=t)
                hs.append(s_fin)
                sk.append(kv_out(k, kv_t, 0))
                sv.append(kv_out(v, kv_t, 1))
            h, buf = _tail(h, y_a, y_b, p[i].astype(F32).reshape(n, p.shape[-1]),
                           None if ffn0 is None else ffn0[i], lw['tail'], bsz=bsz, t=t)
            fb.append(buf)
        return (h.reshape(bsz, t, d_model), jnp.stack(mk), jnp.stack(mv), jnp.stack(sr), jnp.stack(si),
                jnp.stack(hs), jnp.stack(sk), jnp.stack(sv), jnp.stack(fb))

    bp = x_prompt.shape[0]
    n_a = (depth + 1) // 2
    n_c = depth // 2
    outs_p = run(x_prompt, p_prompt, 0,
                 jnp.zeros((n_a, bp, S5_GROUPS, S5_STATE), F32), jnp.zeros((n_a, bp, S5_GROUPS, S5_STATE), F32),
                 jnp.zeros((n_c, bp, HGRN_HEADS, HGRN_DIM, HGRN_DIM), F32), None, False)
    outs_s = run(x_sample, p_sample, past_len, state_s5_re, state_s5_im, state_hgrn, state_ffn, True)
    return (outs_p[0], outs_s[0]) + tuple(outs_p[1:]) + tuple(outs_s[1:])
```
